```python
import math
import jax, jax.numpy as jnp
from jax import lax
import numpy as np

D_MODEL = 2048
BATCH = 16
SEQ = 2048
DEPTH = 4

BLOCK = 128
ROPE_THETA = 10000.0
NORM_EPS = 1e-6

DA_HEADS = 4
DA_HEAD_DIM = 128
DA_QK_DIM = 64
DL_HEADS = 6
DL_HEAD_DIM = 128
DL_PATTERNS = ((128, 1), (512, 4), (2048, 16))
RW_HEADS = 12
RW_HEAD_DIM = 64
RW_DECAY_RANK = 64
RW_A_RANK = 64
RW_GATE_RANK = 128
RW_LN_EPS = 64e-5

DA_W = DA_HEADS * DA_HEAD_DIM
DL_W = DL_HEADS * DL_HEAD_DIM
RW_W = RW_HEADS * RW_HEAD_DIM
D_MIX = DA_W + DL_W + RW_W
ATT_SIZES = (DA_W, DA_W, DA_W, DL_W, DL_W, DL_W)
RW_SIZES = (RW_W, RW_W, RW_W, RW_DECAY_RANK, RW_A_RANK, RW_GATE_RANK)
ATT_IN = sum(ATT_SIZES)
RW_IN = sum(RW_SIZES)
D_IN = ATT_IN + RW_IN

D_FF = 5632
N_EXPERTS = 8
TOP_K = 2
D_FF_EXPERT = 5632
N_DENSE = (DEPTH + 1) // 2
N_MOE = DEPTH // 2

kernel_name = "hybrid_diffattn_rwkv7_dilated_moe"


def _rms_norm(x, g, eps=NORM_EPS):
    xf = x.astype(jnp.float32)
    y = xf * lax.rsqrt(jnp.mean(xf * xf, axis=-1, keepdims=True) + eps)
    return (y * g.astype(jnp.float32)).astype(x.dtype)


def _split_cols(p, sizes):
    out, start = [], 0
    for s in sizes:
        out.append(p[..., start:start + s])
        start += s
    return out


def _rope(x, pos):
    d = x.shape[-1]
    half = d // 2
    inv = ROPE_THETA ** (-jnp.arange(half, dtype=jnp.float32) / half)
    ang = pos.astype(jnp.float32)[:, None] * inv[None, :]
    bshape = (1, pos.shape[0]) + (1,) * (x.ndim - 3) + (half,)
    cos, sin = jnp.cos(ang).reshape(bshape), jnp.sin(ang).reshape(bshape)
    xf = x.astype(jnp.float32)
    x1, x2 = xf[..., :half], xf[..., half:]
    return jnp.concatenate([x1 * cos - x2 * sin, x2 * cos + x1 * sin], axis=-1).astype(x.dtype)


def _diff_attention(q, k, v, lam):
    b, s, h, _, dk = q.shape
    nb = s // BLOCK
    scale = dk ** -0.5
    q_blocks = jnp.moveaxis(q.reshape(b, nb, BLOCK, h, 2, dk), 1, 0)
    k_pos = jnp.arange(s)

    def attend_block(args):
        q_blk, blk = args
        sc = jnp.einsum("bqhcd,bkhcd->bhcqk", q_blk, k).astype(jnp.float32) * scale
        q_pos = blk * BLOCK + jnp.arange(BLOCK)
        causal = k_pos[None, :] <= q_pos[:, None]
        pr = jax.nn.softmax(jnp.where(causal, sc, -jnp.inf), axis=-1)
        diff = pr[:, :, 0] - lam * pr[:, :, 1]
        return jnp.einsum("bhqk,bkhd->bqhd", diff.astype(v.dtype), v)

    o = lax.map(attend_block, (q_blocks, jnp.arange(nb)))
    return jnp.moveaxis(o, 0, 1).reshape(b, s, h, v.shape[-1])


def _dilated_branch(q, k, v, window, dilation):
    b, s, h, d = q.shape
    steps = window // dilation
    length = s // dilation
    nb = -(-length // BLOCK)
    lp = nb * BLOCK

    def to_blocks(t):
        t = jnp.swapaxes(t.reshape(b, length, dilation, h, d), 1, 2)
        t = jnp.pad(t, ((0, 0), (0, 0), (0, lp - length), (0, 0), (0, 0)))
        return t.reshape(b, dilation, nb, BLOCK, h, d)

    def with_prev(t):
        prev = jnp.pad(t, ((0, 0), (0, 0), (1, 0), (0, 0), (0, 0), (0, 0)))[:, :, :-1]
        return jnp.concatenate([prev, t], axis=3)

    qb = to_blocks(q)
    kb, vb = with_prev(to_blocks(k)), with_prev(to_blocks(v))
    sc = jnp.einsum("brnqhd,brnkhd->brnhqk", qb, kb).astype(jnp.float32) * (d ** -0.5)
    qi = jnp.arange(BLOCK)[:, None]
    kj = jnp.arange(2 * BLOCK)[None, :]
    offset = qi - kj + BLOCK
    band = (offset >= 0) & (offset <= steps)
    key_idx = jnp.arange(nb)[:, None] * BLOCK - BLOCK + kj
    valid = band[None] & (key_idx >= 0)[:, None, :]
    sc = jnp.where(valid[None, None, :, None], sc, -jnp.inf)
    lse = jax.nn.logsumexp(sc, axis=-1)
    pr = jnp.exp(sc - lse[..., None])
    o = jnp.einsum("brnhqk,brnkhd->brnqhd", pr.astype(v.dtype), vb)
    lse = jnp.swapaxes(lse, 3, 4)

    def from_blocks(t):
        t = t.reshape((b, dilation, lp) + t.shape[4:])[:, :, :length]
        return jnp.swapaxes(t, 1, 2).reshape((b, s) + t.shape[3:])

    return from_blocks(o), from_blocks(lse)


def _dilated_attention(q, k, v):
    outs, lses = [], []
    for window, dilation in DL_PATTERNS:
        o, lse = _dilated_branch(q, k, v, window, dilation)
        outs.append(o.astype(jnp.float32))
        lses.append(lse)
    wts = jax.nn.softmax(jnp.stack(lses, axis=-1), axis=-1)
    return jnp.einsum("bshp,pbshd->bshd", wts, jnp.stack(outs, axis=0))


def _rwkv7_scan(r, w, k, v, a, b):
    bsz, _, h, n = r.shape

    def step(state, inp):
        r_t, w_t, k_t, v_t, a_t, b_t = inp
        sa = jnp.einsum("bhvk,bhk->bhv", state, a_t)
        state = (state * w_t[:, :, None, :] + sa[..., None] * b_t[:, :, None, :]
                 + v_t[..., None] * k_t[:, :, None, :])
        return state, jnp.einsum("bhvk,bhk->bhv", state, r_t)

    xs = tuple(jnp.moveaxis(t.astype(jnp.float32), 1, 0) for t in (r, w, k, v, a, b))
    init = jnp.zeros((bsz, h, n, n), jnp.float32)
    _, y = lax.scan(step, init, xs)
    return jnp.moveaxis(y, 0, 1)


def _swiglu(h, w_gate, w_up, w_down):
    return (jax.nn.silu(h @ w_gate) * (h @ w_up)) @ w_down


def _moe(h, router, w_gate, w_up, w_down):
    b, s, d = h.shape
    t = h.reshape(b * s, d)
    logits = (t @ router).astype(jnp.float32)
    top_v, top_i = lax.top_k(logits, TOP_K)
    top_w = jax.nn.softmax(top_v, axis=-1)
    combine = jnp.einsum("tk,tke->te", top_w, jax.nn.one_hot(top_i, N_EXPERTS, dtype=jnp.float32))
    y = jnp.zeros((b * s, d), jnp.float32)
    for e in range(N_EXPERTS):
        y = y + combine[:, e:e + 1] * _swiglu(t, w_gate[e], w_up[e], w_down[e]).astype(jnp.float32)
    return y.reshape(b, s, d).astype(h.dtype)


def setup_inputs(seed: int = 0) -> dict:
    key = jax.random.key(seed)
    ks = iter(jax.random.split(key, 40))
    f32 = jnp.float32

    def nrm(shape, scale):
        return jax.random.normal(next(ks), shape, f32) * scale

    def gain(shape):
        return 1.0 + 0.02 * jax.random.normal(next(ks), shape, f32)

    out_scale = (2 * DEPTH) ** -0.5
    return {
        "x": nrm((BATCH, SEQ, D_MODEL), 1.0),
        "norm1_g": gain((DEPTH, D_MODEL)),
        "w_in": nrm((DEPTH, D_MODEL, D_IN), D_MODEL ** -0.5),
        "da_q_norm": gain((DEPTH, DA_QK_DIM)),
        "da_k_norm": gain((DEPTH, DA_QK_DIM)),
        "da_lambda": nrm((DEPTH, 4, DA_QK_DIM), 0.1),
        "da_out_norm": gain((DEPTH, DA_HEAD_DIM)),
        "dl_q_norm": gain((DEPTH, DL_HEAD_DIM)),
        "dl_k_norm": gain((DEPTH, DL_HEAD_DIM)),
        "rw_mu": jax.random.uniform(next(ks), (DEPTH, RW_IN), f32),
        "rw_w0": jax.random.uniform(next(ks), (DEPTH, RW_W), f32, -6.0, -1.0),
        "rw_w2": nrm((DEPTH, RW_DECAY_RANK, RW_W), 0.1),
        "rw_a0": nrm((DEPTH, RW_W), 0.1),
        "rw_a2": nrm((DEPTH, RW_A_RANK, RW_W), 0.1),
        "rw_g2": nrm((DEPTH, RW_GATE_RANK, RW_W), RW_GATE_RANK ** -0.5),
        "rw_k_k": 0.85 + 0.02 * jax.random.normal(next(ks), (DEPTH, RW_W), f32),
        "rw_k_a": gain((DEPTH, RW_W)),
        "rw_r_k": nrm((DEPTH, RW_HEADS, RW_HEAD_DIM), 0.1),
        "rw_ln_g": gain((DEPTH, RW_W)),
        "rw_ln_b": nrm((DEPTH, RW_W), 0.02),
        "w_out": nrm((DEPTH, D_MIX, D_MODEL), D_MIX ** -0.5 * out_scale),
        "norm2_g": gain((DEPTH, D_MODEL)),
        "ffn_w_gate": nrm((N_DENSE, D_MODEL, D_FF), D_MODEL ** -0.5),
        "ffn_w_up": nrm((N_DENSE, D_MODEL, D_FF), D_MODEL ** -0.5),
        "ffn_w_down": nrm((N_DENSE, D_FF, D_MODEL), D_FF ** -0.5 * out_scale),
        "moe_router": nrm((N_MOE, D_MODEL, N_EXPERTS), D_MODEL ** -0.5),
        "moe_w_gate": nrm((N_MOE, N_EXPERTS, D_MODEL, D_FF_EXPERT), D_MODEL ** -0.5),
        "moe_w_up": nrm((N_MOE, N_EXPERTS, D_MODEL, D_FF_EXPERT), D_MODEL ** -0.5),
        "moe_w_down": nrm((N_MOE, N_EXPERTS, D_FF_EXPERT, D_MODEL), D_FF_EXPERT ** -0.5 * out_scale),
    }


def reference(x, norm1_g, w_in, da_q_norm, da_k_norm, da_lambda, da_out_norm,
              dl_q_norm, dl_k_norm, rw_mu, rw_w0, rw_w2, rw_a0, rw_a2, rw_g2,
              rw_k_k, rw_k_a, rw_r_k, rw_ln_g, rw_ln_b, w_out, norm2_g,
              ffn_w_gate, ffn_w_up, ffn_w_down, moe_router, moe_w_gate, moe_w_up,
              moe_w_down):
    b, s, _ = x.shape
    f32 = jnp.float32
    pos = jnp.arange(s)
    for l in range(DEPTH):
        h = _rms_norm(x, norm1_g[l])
        p = jnp.einsum("bsd,dc->bsc", h, w_in[l])
        da_q, da_k, da_v, dl_q, dl_k, dl_v = _split_cols(p[..., :ATT_IN], ATT_SIZES)

        qa = _rope(_rms_norm(da_q.reshape(b, s, DA_HEADS, 2, DA_QK_DIM), da_q_norm[l]), pos)
        ka = _rope(_rms_norm(da_k.reshape(b, s, DA_HEADS, 2, DA_QK_DIM), da_k_norm[l]), pos)
        va = da_v.reshape(b, s, DA_HEADS, DA_HEAD_DIM)
        lam_init = 0.8 - 0.6 * math.exp(-0.3 * l)
        lq1, lk1, lq2, lk2 = da_lambda[l].astype(f32)
        lam = jnp.exp(jnp.sum(lq1 * lk1)) - jnp.exp(jnp.sum(lq2 * lk2)) + lam_init
        oa = _diff_attention(qa, ka, va, lam)
        oa = _rms_norm(oa, da_out_norm[l]).astype(f32) * (1.0 - lam_init)

        qc = _rope(_rms_norm(dl_q.reshape(b, s, DL_HEADS, DL_HEAD_DIM), dl_q_norm[l]), pos)
        kc = _rope(_rms_norm(dl_k.reshape(b, s, DL_HEADS, DL_HEAD_DIM), dl_k_norm[l]), pos)
        vc = dl_v.reshape(b, s, DL_HEADS, DL_HEAD_DIM)
        oc = _dilated_attention(qc, kc, vc)

        rw_p = p[..., ATT_IN:]
        prev = jnp.pad(rw_p, ((0, 0), (1, 0), (0, 0)))[:, :-1]
        rw_p = rw_p + rw_mu[l] * (prev - rw_p)
        rr, kr, vr, wl, al, gl = _split_cols(rw_p, RW_SIZES)
        w_log = -jax.nn.softplus(-(rw_w0[l] + jnp.tanh(wl) @ rw_w2[l])) - 0.5
        decay = jnp.exp(-jnp.exp(w_log.astype(f32)))
        a = jax.nn.sigmoid(rw_a0[l] + al @ rw_a2[l])
        g = jax.nn.sigmoid(gl) @ rw_g2[l]

        def heads(t):
            return t.reshape(b, s, RW_HEADS, RW_HEAD_DIM).astype(f32)

        kk = heads(kr * rw_k_k[l])
        kk = kk / jnp.maximum(jnp.sqrt(jnp.sum(kk * kk, axis=-1, keepdims=True)), 1e-12)
        kr = kr * (1.0 + (a - 1.0) * rw_k_a[l])
        r_h, k_h, v_h, a_h = heads(rr), heads(kr), heads(vr), heads(a)
        y = _rwkv7_scan(r_h, heads(decay), k_h, v_h, -kk, kk * a_h)
        mu = jnp.mean(y, axis=-1, keepdims=True)
        var = jnp.mean(jnp.square(y - mu), axis=-1, keepdims=True)
        y = ((y - mu) * lax.rsqrt(var + RW_LN_EPS) * rw_ln_g[l].reshape(RW_HEADS, RW_HEAD_DIM)
             + rw_ln_b[l].reshape(RW_HEADS, RW_HEAD_DIM))
        y = y + jnp.sum(r_h * k_h * rw_r_k[l], axis=-1, keepdims=True) * v_h
        ob = y.reshape(b, s, RW_W) * g.astype(f32)

        mix = jnp.concatenate([oa.reshape(b, s, DA_W), oc.reshape(b, s, DL_W), ob],
                              axis=-1).astype(x.dtype)
        x = x + jnp.einsum("bsc,cd->bsd", mix, w_out[l])

        h2 = _rms_norm(x, norm2_g[l])
        if l % 2 == 0:
            i = l // 2
            x = x + _swiglu(h2, ffn_w_gate[i], ffn_w_up[i], ffn_w_down[i])
        else:
            i = l // 2
            x = x + _moe(h2, moe_router[i], moe_w_gate[i], moe_w_up[i], moe_w_down[i])
    return x
```

```python
import functools
import math

import numpy as np
import jax
import jax.numpy as jnp
from jax import lax
from jax.experimental import pallas as pl
from jax.experimental.pallas import tpu as pltpu

F32 = jnp.float32
BF16 = jnp.bfloat16

LANES = 128
VMEM_LIMIT = 56 * 1024 * 1024

NEG = -1e30
ROPE_THETA = 10000.0
NORM_EPS = 1e-6
RW_LN_EPS = 64e-5
DL_PATTERNS = ((128, 1), (512, 4), (2048, 16))
TOP_K = 2

DA_HEADS, DA_QK = 4, 64
DL_HEADS = 6
RW_HEADS, RW_DIM = 12, 64
DA_W, DL_W, RW_W = 512, 768, 768
CHUNK = 64


def _params(*sem):
    return pltpu.CompilerParams(dimension_semantics=sem, vmem_limit_bytes=VMEM_LIMIT)


def _dot(a, b):
    return jnp.dot(a.astype(BF16), b.astype(BF16), preferred_element_type=F32)


def _dot_nt(a, b):
    return lax.dot_general(a.astype(BF16), b.astype(BF16), (((1,), (1,)), ((), ())),
                           preferred_element_type=F32)


def _dot_tn(a, b):
    return lax.dot_general(a.astype(BF16), b.astype(BF16), (((0,), (0,)), ((), ())),
                           preferred_element_type=F32)


def _split(x):
    hi = x.astype(BF16)
    lo = (x - hi.astype(F32)).astype(BF16)
    return hi, lo


def _dot3(a, b):
    ah, al = _split(a)
    bh, bl = _split(b)
    return _dot(ah, bh) + _dot(ah, bl) + _dot(al, bh)


def _dot_exact_rhs(a, b_bf16):
    ah, al = _split(a)
    return _dot(ah, b_bf16) + _dot(al, b_bf16)


def _dot_exact_lhs(a_bf16, b):
    bh, bl = _split(b)
    r = b - bh.astype(F32) - bl.astype(F32)
    return _dot(a_bf16, bh) + _dot(a_bf16, bl) + _dot(a_bf16, r)


def _group_ones(width, group):
    i = lax.broadcasted_iota(jnp.int32, (width, width), 0) // group
    j = lax.broadcasted_iota(jnp.int32, (width, width), 1) // group
    return (i == j).astype(BF16)


def _rms(x, g):
    return x * lax.rsqrt(jnp.mean(x * x, axis=-1, keepdims=True) + NORM_EPS) * g


def _inproj_kernel(x_ref, g_ref, w_ref, o_ref, xn_ref):
    @pl.when(pl.program_id(1) == 0)
    def _():
        xn_ref[...] = _rms(x_ref[...], g_ref[...]).astype(BF16)

    o_ref[...] = jnp.dot(xn_ref[...], w_ref[...], preferred_element_type=F32)


def _inproj(x, g, w, tm, tn):
    t, d = x.shape
    n = w.shape[1]
    return pl.pallas_call(
        _inproj_kernel,
        grid=(t // tm, n // tn),
        in_specs=[pl.BlockSpec((tm, d), lambda i, j: (i, 0)),
                  pl.BlockSpec((1, d), lambda i, j: (0, 0)),
                  pl.BlockSpec((d, tn), lambda i, j: (0, j))],
        out_specs=pl.BlockSpec((tm, tn), lambda i, j: (i, j)),
        out_shape=jax.ShapeDtypeStruct((t, n), F32),
        scratch_shapes=[pltpu.VMEM((tm, d), BF16)],
        compiler_params=_params("parallel", "arbitrary"),
    )(x, g.reshape(1, d), w)


def _qkprep_kernel(p_ref, g_ref, cos_ref, sin_ref, o_ref, *, group):
    x = p_ref[...]
    ones = _group_ones(LANES, group)
    ms = _dot_exact_rhs(x * x, ones) * (1.0 / group)
    y = x * lax.rsqrt(ms + NORM_EPS) * g_ref[...]
    half = group // 2
    lane = lax.broadcasted_iota(jnp.int32, y.shape, 1)
    fwd = pltpu.roll(y, LANES - half, axis=1)
    bwd = pltpu.roll(y, half, axis=1)
    partner = jnp.where((lane % group) < half, fwd, bwd)
    o_ref[...] = (y * cos_ref[...] + partner * sin_ref[...]).astype(BF16)


def _rope_tables(s, group):
    half = group // 2
    lane = np.arange(LANES)
    inv = ROPE_THETA ** (-jnp.asarray(lane % half, F32) / half)
    ang = jnp.arange(s, dtype=F32)[:, None] * inv[None, :]
    sign = jnp.asarray(np.where((lane % group) < half, -1.0, 1.0), F32)
    return jnp.cos(ang), jnp.sin(ang) * sign[None, :]


def _qkprep(p, col0, ncols, gain, s, group, tm):
    t = p.shape[0]
    cos, sin = _rope_tables(s, group)
    g = jnp.tile(gain.astype(F32), LANES // group).reshape(1, LANES)
    nsb = s // tm
    cb0 = col0 // LANES
    return pl.pallas_call(
        functools.partial(_qkprep_kernel, group=group),
        grid=(t // tm, ncols // LANES),
        in_specs=[pl.BlockSpec((tm, LANES), lambda i, j: (i, cb0 + j)),
                  pl.BlockSpec((1, LANES), lambda i, j: (0, 0)),
                  pl.BlockSpec((tm, LANES), lambda i, j: (i % nsb, 0)),
                  pl.BlockSpec((tm, LANES), lambda i, j: (i % nsb, 0))],
        out_specs=pl.BlockSpec((tm, LANES), lambda i, j: (i, j)),
        out_shape=jax.ShapeDtypeStruct((t, ncols), BF16),
        compiler_params=_params("parallel", "parallel"),
    )(p, g, cos, sin)


def _softmax_step(s, m_ref, l_ref, acc_ref, vb):
    m_old = m_ref[...]
    m_new = jnp.maximum(m_old, jnp.max(s, axis=-1, keepdims=True))
    alpha = jnp.exp(m_old - m_new)
    pr = jnp.exp(s - m_new)
    l_ref[...] = alpha * l_ref[...] + jnp.sum(pr, axis=-1, keepdims=True)
    acc_ref[...] = alpha * acc_ref[...] + _dot(pr, vb)
    m_ref[...] = m_new


def _diffattn_kernel(q_ref, k_ref, v_ref, lam_ref, g_ref, o_ref,
                     m0, l0, a0, m1, l1, a1, *, tq, tk, scale, lam_init):
    qi = pl.program_id(2)
    q = q_ref[...]
    lane = lax.broadcasted_iota(jnp.int32, q.shape, 1)
    zero = jnp.zeros_like(q)
    q0 = jnp.where(lane < DA_QK, q, zero)
    q1 = jnp.where(lane >= DA_QK, q, zero)
    for m, l, a in ((m0, l0, a0), (m1, l1, a1)):
        m[...] = jnp.full(m.shape, NEG, F32)
        l[...] = jnp.zeros(l.shape, F32)
        a[...] = jnp.zeros(a.shape, F32)
    row = qi * tq + lax.broadcasted_iota(jnp.int32, (tq, tk), 0)
    col = lax.broadcasted_iota(jnp.int32, (tq, tk), 1)

    def body(j, carry):
        off = pl.multiple_of(j * tk, tk)
        kb = k_ref[pl.ds(off, tk), :]
        vb = v_ref[pl.ds(off, tk), :].astype(BF16)
        causal = row >= col + off
        s0 = jnp.where(causal, _dot_nt(q0, kb) * scale, NEG)
        s1 = jnp.where(causal, _dot_nt(q1, kb) * scale, NEG)
        _softmax_step(s0, m0, l0, a0, vb)
        _softmax_step(s1, m1, l1, a1, vb)
        return carry

    lax.fori_loop(0, (qi * tq + tq + tk - 1) // tk, body, 0)

    lm = lam_ref[...]
    lam = (jnp.exp(jnp.sum(lm[0:1] * lm[1:2], axis=-1, keepdims=True))
           - jnp.exp(jnp.sum(lm[2:3] * lm[3:4], axis=-1, keepdims=True)) + lam_init)
    o = a0[...] / l0[...] - lam * (a1[...] / l1[...])
    o_ref[...] = (_rms(o, g_ref[...]) * (1.0 - lam_init)).astype(BF16)


def _diffattn(q, k, p, vcol0, lam4, gain, b, s, lam_init, tq, tk):
    t = q.shape[0]
    nq = s // tq
    vb0 = vcol0 // LANES
    return pl.pallas_call(
        functools.partial(_diffattn_kernel, tq=tq, tk=tk, scale=DA_QK ** -0.5, lam_init=lam_init),
        grid=(b, DA_HEADS, nq),
        in_specs=[pl.BlockSpec((tq, LANES), lambda bi, h, qi: (bi * nq + qi, h)),
                  pl.BlockSpec((s, LANES), lambda bi, h, qi: (bi, h)),
                  pl.BlockSpec((s, LANES), lambda bi, h, qi: (bi, vb0 + h)),
                  pl.BlockSpec((4, DA_QK), lambda bi, h, qi: (0, 0)),
                  pl.BlockSpec((1, LANES), lambda bi, h, qi: (0, 0))],
        out_specs=pl.BlockSpec((tq, LANES), lambda bi, h, qi: (bi * nq + qi, h)),
        out_shape=jax.ShapeDtypeStruct((t, DA_W), BF16),
        scratch_shapes=[pltpu.VMEM((tq, 1), F32), pltpu.VMEM((tq, 1), F32), pltpu.VMEM((tq, LANES), F32),
                        pltpu.VMEM((tq, 1), F32), pltpu.VMEM((tq, 1), F32), pltpu.VMEM((tq, LANES), F32)],
        compiler_params=_params("parallel", "parallel", "arbitrary"),
    )(q, k, p, lam4.astype(F32), gain.astype(F32).reshape(1, LANES))


def _dilated_bias(s, tq, tk):
    nd = s // tk
    d = (np.arange(nd)[:, None, None] * tk + np.arange(tq)[None, :, None]
         - np.arange(tk)[None, None, :])
    cnt = np.zeros(d.shape, np.float64)
    for window, dil in DL_PATTERNS:
        cnt += (d >= 0) & (d % dil == 0) & (d <= window)
    with np.errstate(divide="ignore"):
        bias = np.where(cnt > 0, np.log(np.maximum(cnt, 1.0)), NEG)
    return jnp.asarray(bias, F32)


def _dilattn_kernel(q_ref, k_ref, v_ref, bias_ref, o_ref, m0, l0, a0, *, tq, tk, scale):
    qi = pl.program_id(2)
    q = q_ref[...]
    m0[...] = jnp.full(m0.shape, NEG, F32)
    l0[...] = jnp.zeros(l0.shape, F32)
    a0[...] = jnp.zeros(a0.shape, F32)
    nk = (qi * tq + tq) // tk

    def body(j, carry):
        off = pl.multiple_of(j * tk, tk)
        kb = k_ref[pl.ds(off, tk), :]
        vb = v_ref[pl.ds(off, tk), :].astype(BF16)
        s0 = _dot_nt(q, kb) * scale + bias_ref[qi - j]
        _softmax_step(s0, m0, l0, a0, vb)
        return carry

    lax.fori_loop(0, nk, body, 0)
    o_ref[...] = (a0[...] / l0[...]).astype(BF16)


def _dilattn(q, k, p, vcol0, b, s, tq):
    t = q.shape[0]
    nq = s // tq
    vb0 = vcol0 // LANES
    bias = _dilated_bias(s, tq, tq)
    return pl.pallas_call(
        functools.partial(_dilattn_kernel, tq=tq, tk=tq, scale=LANES ** -0.5),
        grid=(b, DL_HEADS, nq),
        in_specs=[pl.BlockSpec((tq, LANES), lambda bi, h, qi: (bi * nq + qi, h)),
                  pl.BlockSpec((s, LANES), lambda bi, h, qi: (bi, h)),
                  pl.BlockSpec((s, LANES), lambda bi, h, qi: (bi, vb0 + h)),
                  pl.BlockSpec((nq, tq, tq), lambda bi, h, qi: (0, 0, 0))],
        out_specs=pl.BlockSpec((tq, LANES), lambda bi, h, qi: (bi * nq + qi, h)),
        out_shape=jax.ShapeDtypeStruct((t, DL_W), BF16),
        scratch_shapes=[pltpu.VMEM((tq, 1), F32), pltpu.VMEM((tq, 1), F32), pltpu.VMEM((tq, LANES), F32)],
        compiler_params=_params("parallel", "parallel", "arbitrary"),
    )(q, k, p, bias)


def _rwprep_kernel(r_ref, k_ref, v_ref, x_ref, rp_ref, kp_ref, vp_ref, xp_ref,
                   mu_ref, w0_ref, a0_ref, kk_ref, ka_ref, w2_ref, a2_ref, g2_ref,
                   ro, lwo, ko, vo, nao, bo, go, *, rows_per_seq):
    i = pl.program_id(0)
    tm = r_ref.shape[0]
    first = (i * tm) % rows_per_seq == 0
    row = lax.broadcasted_iota(jnp.int32, (tm, 1), 0)

    def shifted(cur_ref, prev_ref, mu):
        cur = cur_ref[...]
        last = jnp.where(first, 0.0, prev_ref[7:8, :])
        prev = jnp.where(row == 0, last, pltpu.roll(cur, 1, axis=0))
        return cur + mu * (prev - cur)

    mu = mu_ref[...]
    rr = shifted(r_ref, rp_ref, mu[:, 0:RW_W])
    kr = shifted(k_ref, kp_ref, mu[:, RW_W:2 * RW_W])
    vr = shifted(v_ref, vp_ref, mu[:, 2 * RW_W:3 * RW_W])
    xs = shifted(x_ref, xp_ref, mu[:, 3 * RW_W:])

    z = w0_ref[...] + _dot3(jnp.tanh(xs), w2_ref[...])
    nz = -z
    softplus = jnp.maximum(nz, 0.0) + jnp.log(1.0 + jnp.exp(-jnp.abs(nz)))
    w_log = -softplus - 0.5
    a = jax.nn.sigmoid(a0_ref[...] + _dot3(xs, a2_ref[...]))
    g = _dot3(jax.nn.sigmoid(xs), g2_ref[...])

    ones = _group_ones(RW_W, RW_DIM)
    kkr = kr * kk_ref[...]
    ss = _dot_exact_rhs(kkr * kkr, ones)
    kk = kkr / jnp.maximum(jnp.sqrt(ss), 1e-12)
    ro[...] = rr
    lwo[...] = -jnp.exp(w_log)
    ko[...] = kr * (1.0 + (a - 1.0) * ka_ref[...])
    vo[...] = vr
    nao[...] = -kk
    bo[...] = kk * a
    go[...] = g


def _rwprep(p, col0, s, mu, w0, a0, k_k, k_a, w2, a2, g2, tm):
    t = p.shape[0]
    lr = mu.shape[0] - 3 * RW_W
    cb = col0 // RW_W
    xb = (col0 + 3 * RW_W) // lr
    w2p = jnp.zeros((lr, RW_W), F32).at[0:w2.shape[0]].set(w2)
    a2p = jnp.zeros((lr, RW_W), F32).at[w2.shape[0]:w2.shape[0] + a2.shape[0]].set(a2)
    g2p = jnp.zeros((lr, RW_W), F32).at[lr - g2.shape[0]:].set(g2)
    r8 = tm // 8

    def cur(c, w):
        return pl.BlockSpec((tm, w), lambda i: (i, c))

    def prev(c, w):
        return pl.BlockSpec((8, w), lambda i: (jnp.maximum(i * r8 - 1, 0), c))

    def full(shape):
        return pl.BlockSpec(shape, lambda i: (0, 0))

    vec = lambda a_: a_.astype(F32).reshape(1, -1)
    out = jax.ShapeDtypeStruct((t, RW_W), F32)
    return pl.pallas_call(
        functools.partial(_rwprep_kernel, rows_per_seq=s),
        grid=(t // tm,),
        in_specs=[cur(cb, RW_W), cur(cb + 1, RW_W), cur(cb + 2, RW_W), cur(xb, lr),
                  prev(cb, RW_W), prev(cb + 1, RW_W), prev(cb + 2, RW_W), prev(xb, lr),
                  full((1, 3 * RW_W + lr)), full((1, RW_W)), full((1, RW_W)), full((1, RW_W)),
                  full((1, RW_W)), full((lr, RW_W)), full((lr, RW_W)), full((lr, RW_W))],
        out_specs=[pl.BlockSpec((tm, RW_W), lambda i: (i, 0))] * 7,
        out_shape=[out] * 7,
        compiler_params=_params("parallel"),
    )(p, p, p, p, p, p, p, p, vec(mu), vec(w0), vec(a0), vec(k_k), vec(k_a), w2p, a2p, g2p)


def _rwkv_kernel(r_ref, lw_ref, k_ref, v_ref, a_ref, b_ref, y_ref, rh_ref, yh_ref, p_ref, q_ref,
                 st_ref, *, nchunk):
    L = CHUNK
    W = 2 * L
    lane = lax.broadcasted_iota(jnp.int32, (L, W), 1)
    rowi = lax.broadcasted_iota(jnp.int32, (L, W), 0)
    strict = (lane % L) < rowi
    incl = (lane % L) <= rowi
    rr = lax.broadcasted_iota(jnp.int32, (W, W), 0)
    cc = lax.broadcasted_iota(jnp.int32, (W, W), 1)
    same = (rr // L) == (cc // L)
    eye = rr == cc
    tl = lax.broadcasted_iota(jnp.int32, (L, L), 0)
    sl = lax.broadcasted_iota(jnp.int32, (L, L), 1)
    tril = (sl <= tl).astype(BF16)

    def bd(x):
        return jnp.where(same, jnp.concatenate([x, x], axis=0), 0.0)

    def chunk(c, carry):
        rows = pl.ds(pl.multiple_of(c * L, L), L)
        r, lw, k, v, a, b = (ref[rows, :] for ref in (r_ref, lw_ref, k_ref, v_ref, a_ref, b_ref))
        cin = _dot_exact_lhs(tril, lw)
        clast = cin[L - 1:L, :]
        g_in = jnp.exp(cin)
        g_inv = jnp.exp(-cin)
        g_tail = jnp.exp(clast - cin)
        at = a * jnp.exp(cin - lw)
        rt = r * g_in
        ar = jnp.concatenate([at, rt], axis=0)
        ab = _dot_nt(ar, bd(b * g_inv))
        ak = _dot_nt(ar, bd(k * g_inv))
        a_ab = jnp.where(strict, ab[0:L], 0.0)
        a_rb = jnp.where(incl, ab[L:W], 0.0)
        a_ak = jnp.where(strict, ak[0:L], 0.0)
        a_rk = jnp.where(incl, ak[L:W], 0.0)
        n = bd(a_ab)
        tm = jnp.where(eye, 1.0, n)
        x = n
        for _ in range(5):
            x = _dot(x, x)
            tm = tm + _dot(x, tm)
        v_bd = bd(v)
        akv = _dot(bd(a_ak), v_bd)
        ahat = _dot(tm, bd(at))
        uhat = _dot(tm, akv)
        rh_ref[rows, :] = rt + _dot(a_rb, ahat)
        yh_ref[rows, :] = _dot(a_rb, uhat) + _dot(a_rk, v_bd)
        bbar = bd(b * g_tail)
        kbar = bd(k * g_tail)
        prow = pl.ds(pl.multiple_of(c * W, W), W)
        p_ref[prow, :] = jnp.where(eye, jnp.exp(clast), 0.0) + _dot_tn(bbar, ahat)
        q_ref[prow, :] = _dot_tn(bbar, uhat) + _dot_tn(kbar, v_bd)
        return carry

    lax.fori_loop(0, nchunk, chunk, 0)

    st_ref[...] = jnp.zeros((W, W), F32)

    def step(c, carry):
        rows = pl.ds(pl.multiple_of(c * L, L), L)
        prow = pl.ds(pl.multiple_of(c * W, W), W)
        st = st_ref[...].astype(BF16)
        y_ref[rows, :] = _dot(rh_ref[rows, :], st) + yh_ref[rows, :]
        st_ref[...] = _dot(p_ref[prow, :], st) + q_ref[prow, :]
        return carry

    lax.fori_loop(0, nchunk, step, 0)


def _rwkv(r, lw, k, v, na, bb, b, s):
    t = r.shape[0]
    nchunk = s // CHUNK
    w = 2 * CHUNK
    spec = pl.BlockSpec((s, w), lambda bi, h: (bi, h))
    return pl.pallas_call(
        functools.partial(_rwkv_kernel, nchunk=nchunk),
        grid=(b, RW_W // w),
        in_specs=[spec] * 6,
        out_specs=spec,
        out_shape=jax.ShapeDtypeStruct((t, RW_W), F32),
        scratch_shapes=[pltpu.VMEM((s, w), F32), pltpu.VMEM((s, w), F32),
                        pltpu.VMEM((nchunk * w, w), F32), pltpu.VMEM((nchunk * w, w), F32),
                        pltpu.VMEM((w, w), F32)],
        compiler_params=_params("parallel", "parallel"),
    )(r, lw, k, v, na, bb)


def _rwpost_kernel(y_ref, r_ref, k_ref, v_ref, g_ref, lng_ref, lnb_ref, rk_ref, o_ref):
    ones = _group_ones(RW_W, RW_DIM)
    y = y_ref[...]
    mu = _dot_exact_rhs(y, ones) * (1.0 / RW_DIM)
    yc = y - mu
    var = _dot_exact_rhs(yc * yc, ones) * (1.0 / RW_DIM)
    out = yc * lax.rsqrt(var + RW_LN_EPS) * lng_ref[...] + lnb_ref[...]
    v = v_ref[...]
    bonus = _dot_exact_rhs(r_ref[...] * k_ref[...] * rk_ref[...], ones)
    o_ref[...] = ((out + bonus * v) * g_ref[...]).astype(BF16)


def _rwpost(y, r, k, v, g, ln_g, ln_b, r_k, tm):
    t = y.shape[0]
    big = pl.BlockSpec((tm, RW_W), lambda i: (i, 0))
    small = pl.BlockSpec((1, RW_W), lambda i: (0, 0))
    vec = lambda a_: a_.astype(F32).reshape(1, RW_W)
    return pl.pallas_call(
        _rwpost_kernel,
        grid=(t // tm,),
        in_specs=[big] * 5 + [small] * 3,
        out_specs=big,
        out_shape=jax.ShapeDtypeStruct((t, RW_W), BF16),
        compiler_params=_params("parallel"),
    )(y, r, k, v, g, vec(ln_g), vec(ln_b), vec(r_k))


def _outproj_kernel(x_ref, oa_ref, oc_ref, ob_ref, w_ref, o_ref):
    acc = jnp.dot(oa_ref[...], w_ref[0:DA_W, :], preferred_element_type=F32)
    acc += jnp.dot(oc_ref[...], w_ref[DA_W:DA_W + DL_W, :], preferred_element_type=F32)
    acc += jnp.dot(ob_ref[...], w_ref[DA_W + DL_W:, :], preferred_element_type=F32)
    o_ref[...] = x_ref[...] + acc


def _outproj(x, oa, oc, ob, w, tm, tn):
    t, d = x.shape
    return pl.pallas_call(
        _outproj_kernel,
        grid=(t // tm, d // tn),
        in_specs=[pl.BlockSpec((tm, tn), lambda i, j: (i, j)),
                  pl.BlockSpec((tm, DA_W), lambda i, j: (i, 0)),
                  pl.BlockSpec((tm, DL_W), lambda i, j: (i, 0)),
                  pl.BlockSpec((tm, RW_W), lambda i, j: (i, 0)),
                  pl.BlockSpec((w.shape[0], tn), lambda i, j: (0, j))],
        out_specs=pl.BlockSpec((tm, tn), lambda i, j: (i, j)),
        out_shape=jax.ShapeDtypeStruct((t, d), F32),
        compiler_params=_params("parallel", "arbitrary"),
    )(x, oa, oc, ob, w)


def _swiglu_tile(x, wg, wu):
    gate = jnp.dot(x, wg, preferred_element_type=F32)
    up = jnp.dot(x, wu, preferred_element_type=F32)
    return (gate * jax.nn.sigmoid(gate) * up).astype(BF16)


def _ffn_kernel(x_ref, g_ref, wg_ref, wu_ref, wd_ref, o_ref, xn_ref, acc_ref):
    f = pl.program_id(1)

    @pl.when(f == 0)
    def _():
        xn_ref[...] = _rms(x_ref[...], g_ref[...]).astype(BF16)
        acc_ref[...] = jnp.zeros(acc_ref.shape, F32)

    act = _swiglu_tile(xn_ref[...], wg_ref[...], wu_ref[...])
    acc_ref[...] += jnp.dot(act, wd_ref[...], preferred_element_type=F32)

    @pl.when(f == pl.num_programs(1) - 1)
    def _():
        o_ref[...] = x_ref[...] + acc_ref[...]


def _ffn(x, g, wg, wu, wd, tm, tf):
    t, d = x.shape
    ff = wg.shape[1]
    return pl.pallas_call(
        _ffn_kernel,
        grid=(t // tm, ff // tf),
        in_specs=[pl.BlockSpec((tm, d), lambda i, f: (i, 0)),
                  pl.BlockSpec((1, d), lambda i, f: (0, 0)),
                  pl.BlockSpec((d, tf), lambda i, f: (0, f)),
                  pl.BlockSpec((d, tf), lambda i, f: (0, f)),
                  pl.BlockSpec((tf, d), lambda i, f: (f, 0))],
        out_specs=pl.BlockSpec((tm, d), lambda i, f: (i, 0)),
        out_shape=jax.ShapeDtypeStruct((t, d), F32),
        scratch_shapes=[pltpu.VMEM((tm, d), BF16), pltpu.VMEM((tm, d), F32)],
        compiler_params=_params("parallel", "arbitrary"),
    )(x, g.reshape(1, d), wg, wu, wd)


def _router_kernel(x_ref, g_ref, wr_ref, h_ref, comb_ref, combt_ref, rcol_ref, rrow_ref, cnt_ref,
                   *, n_experts):
    h = _rms(x_ref[...], g_ref[...])
    h_ref[...] = h.astype(BF16)
    logits = _dot3(h, wr_ref[...])
    lane = lax.broadcasted_iota(jnp.int32, logits.shape, 1)
    lg = jnp.where(lane < n_experts, logits, NEG)
    m1 = jnp.max(lg, axis=-1, keepdims=True)
    i1 = jnp.min(jnp.where(lg == m1, lane, LANES), axis=-1, keepdims=True)
    lg2 = jnp.where(lane == i1, NEG, lg)
    m2 = jnp.max(lg2, axis=-1, keepdims=True)
    i2 = jnp.min(jnp.where(lg2 == m2, lane, LANES), axis=-1, keepdims=True)
    e2 = jnp.exp(m2 - m1)
    w1 = 1.0 / (1.0 + e2)
    comb = jnp.where(lane == i1, w1, 0.0) + jnp.where(lane == i2, e2 * w1, 0.0)
    combt = comb.T[0:combt_ref.shape[0], :]
    comb_ref[...] = comb
    combt_ref[...] = combt
    ts = comb.shape[0]
    tt = lax.broadcasted_iota(jnp.int32, (ts, ts), 0)
    uu = lax.broadcasted_iota(jnp.int32, (ts, ts), 1)
    live = jnp.where(comb > 0.0, 1.0, 0.0)
    rcol_ref[...] = _dot((uu < tt).astype(BF16), live)
    rrow_ref[...] = _dot(jnp.where(combt > 0.0, 1.0, 0.0), (tt < uu).astype(BF16))
    cnt_ref[0] = jnp.sum(live, axis=0, keepdims=True).astype(jnp.int32)


def _router(x, g, wr, ts):
    t, d = x.shape
    e = wr.shape[1]
    ep = max(8, e)
    wrp = jnp.zeros((d, LANES), F32).at[:, :e].set(wr)
    nt = t // ts
    return pl.pallas_call(
        functools.partial(_router_kernel, n_experts=e),
        grid=(nt,),
        in_specs=[pl.BlockSpec((ts, d), lambda i: (i, 0)),
                  pl.BlockSpec((1, d), lambda i: (0, 0)),
                  pl.BlockSpec((d, LANES), lambda i: (0, 0))],
        out_specs=[pl.BlockSpec((ts, d), lambda i: (i, 0)),
                   pl.BlockSpec((ts, LANES), lambda i: (i, 0)),
                   pl.BlockSpec((ep, ts), lambda i: (0, i)),
                   pl.BlockSpec((ts, LANES), lambda i: (i, 0)),
                   pl.BlockSpec((ep, ts), lambda i: (0, i)),
                   pl.BlockSpec((1, 1, LANES), lambda i: (i, 0, 0))],
        out_shape=[jax.ShapeDtypeStruct((t, d), BF16),
                   jax.ShapeDtypeStruct((t, LANES), F32),
                   jax.ShapeDtypeStruct((ep, t), F32),
                   jax.ShapeDtypeStruct((t, LANES), F32),
                   jax.ShapeDtypeStruct((ep, t), F32),
                   jax.ShapeDtypeStruct((nt, 1, LANES), jnp.int32)],
        compiler_params=_params("parallel"),
    )(x, g.reshape(1, d), wrp)


def _moe_kernel(cnt_ref, h_ref, x_ref, comb_ref, combt_ref, rcol_ref, rrow_ref, wg_ref, wu_ref, wd_ref,
                o_ref, xc_ref, yc_ref, *, cm, n_experts):
    i, e, f = pl.program_id(0), pl.program_id(1), pl.program_id(2)
    ts = h_ref.shape[0]
    nch = (cnt_ref[i * n_experts + e] + cm - 1) // cm

    @pl.when((e == 0) & (f == 0))
    def _():
        o_ref[...] = x_ref[...]

    @pl.when(f == 0)
    def _():
        pos = rrow_ref[pl.ds(e, 1), :]
        live = combt_ref[pl.ds(e, 1), :] > 0.0
        slot = lax.broadcasted_iota(jnp.int32, (cm, ts), 0).astype(F32)

        def gather(c, carry):
            rows = pl.ds(pl.multiple_of(c * cm, cm), cm)
            base = (c * cm).astype(F32)
            onehot = jnp.where((pos == slot + base) & live, 1.0, 0.0).astype(BF16)
            xc_ref[rows, :] = jnp.dot(onehot, h_ref[...], preferred_element_type=F32).astype(BF16)
            yc_ref[rows, :] = jnp.zeros((cm, yc_ref.shape[1]), F32)
            return carry

        lax.fori_loop(0, nch, gather, 0)

    def ffn(c, carry):
        rows = pl.ds(pl.multiple_of(c * cm, cm), cm)
        act = _swiglu_tile(xc_ref[rows, :], wg_ref[...], wu_ref[...])
        yc_ref[rows, :] += jnp.dot(act, wd_ref[...], preferred_element_type=F32)
        return carry

    lax.fori_loop(0, nch, ffn, 0)

    @pl.when(f == pl.num_programs(2) - 1)
    def _():
        lane = lax.broadcasted_iota(jnp.int32, comb_ref.shape, 1)
        sel = lane == e
        cw = jnp.sum(jnp.where(sel, comb_ref[...], 0.0), axis=-1, keepdims=True)
        pos = jnp.sum(jnp.where(sel, rcol_ref[...], 0.0), axis=-1, keepdims=True)
        slot = lax.broadcasted_iota(jnp.int32, (ts, cm), 1).astype(F32)

        def scatter(c, carry):
            rows = pl.ds(pl.multiple_of(c * cm, cm), cm)
            base = (c * cm).astype(F32)
            onehot = jnp.where((pos == slot + base) & (cw > 0.0), 1.0, 0.0).astype(BF16)
            o_ref[...] += cw * jnp.dot(onehot, yc_ref[rows, :].astype(BF16),
                                       preferred_element_type=F32)
            return carry

        lax.fori_loop(0, nch, scatter, 0)


def _moe(x, h, comb, combt, rcol, rrow, counts, wg, wu, wd, ts, tf, cm):
    t, d = x.shape
    ne, _, ff = wg.shape
    ep = combt.shape[0]
    one = pl.Buffered(1)
    grid_spec = pltpu.PrefetchScalarGridSpec(
        num_scalar_prefetch=1,
        grid=(t // ts, ne, ff // tf),
        in_specs=[pl.BlockSpec((ts, d), lambda i, e, f, c: (i, 0), pipeline_mode=one),
                  pl.BlockSpec((ts, d), lambda i, e, f, c: (i, 0), pipeline_mode=one),
                  pl.BlockSpec((ts, LANES), lambda i, e, f, c: (i, 0)),
                  pl.BlockSpec((ep, ts), lambda i, e, f, c: (0, i)),
                  pl.BlockSpec((ts, LANES), lambda i, e, f, c: (i, 0)),
                  pl.BlockSpec((ep, ts), lambda i, e, f, c: (0, i)),
                  pl.BlockSpec((None, d, tf), lambda i, e, f, c: (e, 0, f)),
                  pl.BlockSpec((None, d, tf), lambda i, e, f, c: (e, 0, f)),
                  pl.BlockSpec((None, tf, d), lambda i, e, f, c: (e, f, 0))],
        out_specs=pl.BlockSpec((ts, d), lambda i, e, f, c: (i, 0), pipeline_mode=one),
        scratch_shapes=[pltpu.VMEM((ts, d), BF16), pltpu.VMEM((ts, d), F32)],
    )
    return pl.pallas_call(
        functools.partial(_moe_kernel, cm=cm, n_experts=ne),
        grid_spec=grid_spec,
        out_shape=jax.ShapeDtypeStruct((t, d), F32),
        compiler_params=_params("parallel", "arbitrary", "arbitrary"),
    )(counts, h, x, comb, combt, rcol, rrow, wg, wu, wd)


def kernel(x, norm1_g, w_in, da_q_norm, da_k_norm, da_lambda, da_out_norm, dl_q_norm, dl_k_norm, rw_mu, rw_w0, rw_w2, rw_a0, rw_a2, rw_g2, rw_k_k, rw_k_a, rw_r_k, rw_ln_g, rw_ln_b, w_out, norm2_g, ffn_w_gate, ffn_w_up, ffn_w_down, moe_router, moe_w_gate, moe_w_up, moe_w_down):
    b, s, d = x.shape
    depth = w_in.shape[0]
    t = b * s
    n_experts = moe_router.shape[-1]
    xt = x.reshape(t, d)

    qa0, ka0, va0 = 0, DA_W, 2 * DA_W
    qb0, kb0, vb0 = 3 * DA_W, 3 * DA_W + DL_W, 3 * DA_W + 2 * DL_W
    rw0 = 3 * DA_W + 3 * DL_W

    tm = min(1024, t)
    tsq = min(512, s)
    tq = min(256, s)

    for l in range(depth):
        p = _inproj(xt, norm1_g[l], w_in[l].astype(BF16), tm, 1280)

        qa = _qkprep(p, qa0, DA_W, da_q_norm[l], s, DA_QK, tsq)
        ka = _qkprep(p, ka0, DA_W, da_k_norm[l], s, DA_QK, tsq)
        lam_init = 0.8 - 0.6 * math.exp(-0.3 * l)
        oa = _diffattn(qa, ka, p, va0, da_lambda[l], da_out_norm[l], b, s, lam_init, tq, tq)

        qc = _qkprep(p, qb0, DL_W, dl_q_norm[l], s, LANES, tsq)
        kc = _qkprep(p, kb0, DL_W, dl_k_norm[l], s, LANES, tsq)
        oc = _dilattn(qc, kc, p, vb0, b, s, tq)

        r, lw, k2, v, na, bb, g = _rwprep(p, rw0, s, rw_mu[l], rw_w0[l], rw_a0[l], rw_k_k[l],
                                          rw_k_a[l], rw_w2[l], rw_a2[l], rw_g2[l], min(256, s))
        y = _rwkv(r, lw, k2, v, na, bb, b, s)
        ob = _rwpost(y, r, k2, v, g, rw_ln_g[l], rw_ln_b[l], rw_r_k[l].reshape(-1), tsq)

        xt = _outproj(xt, oa, oc, ob, w_out[l].astype(BF16), tm, 1024)

        i = l // 2
        if l % 2 == 0:
            xt = _ffn(xt, norm2_g[l], ffn_w_gate[i].astype(BF16), ffn_w_up[i].astype(BF16),
                      ffn_w_down[i].astype(BF16), min(512, t), 512)
        else:
            ts = min(1024, t)
            h, comb, combt, rcol, rrow, cnt = _router(xt, norm2_g[l], moe_router[i], ts)
            counts = cnt[:, 0, :n_experts].reshape(-1)
            xt = _moe(xt, h, comb, combt, rcol, rrow, counts, moe_w_gate[i].astype(BF16),
                      moe_w_up[i].astype(BF16), moe_w_down[i].astype(BF16), ts, 512, 128)
    return xt.reshape(b, s, d)
```

```python
import functools
import math

import numpy as np
import jax
import jax.numpy as jnp
from jax import lax
from jax.experimental import pallas as pl
from jax.experimental.pallas import tpu as pltpu

F32 = jnp.float32
BF16 = jnp.bfloat16

LANES = 128
VMEM_LIMIT = 56 * 1024 * 1024

NEG = -1e30
ROPE_THETA = 10000.0
NORM_EPS = 1e-6
RW_LN_EPS = 64e-5
DL_PATTERNS = ((128, 1), (512, 4), (2048, 16))
TOP_K = 2

DA_HEADS, DA_QK = 4, 64
DL_HEADS = 6
RW_HEADS, RW_DIM = 12, 64
DA_W, DL_W, RW_W = 512, 768, 768
RW_PAIRS = RW_W // LANES
CHUNK = 64


def _params(*sem):
    return pltpu.CompilerParams(dimension_semantics=sem, vmem_limit_bytes=VMEM_LIMIT)


def _dot(a, b):
    return jnp.dot(a.astype(BF16), b.astype(BF16), preferred_element_type=F32)


def _dot_nt(a, b):
    return lax.dot_general(a.astype(BF16), b.astype(BF16), (((1,), (1,)), ((), ())),
                           preferred_element_type=F32)


def _dot_tn(a, b):
    return lax.dot_general(a.astype(BF16), b.astype(BF16), (((0,), (0,)), ((), ())),
                           preferred_element_type=F32)


def _split(x):
    hi = x.astype(BF16)
    lo = (x - hi.astype(F32)).astype(BF16)
    return hi, lo


def _dot3(a, b):
    ah, al = _split(a)
    bh, bl = _split(b)
    return _dot(ah, bh) + _dot(ah, bl) + _dot(al, bh)


def _dot_exact_rhs(a, b_bf16):
    ah, al = _split(a)
    return _dot(ah, b_bf16) + _dot(al, b_bf16)


def _group_ones(width, group):
    i = lax.broadcasted_iota(jnp.int32, (width, width), 0) // group
    j = lax.broadcasted_iota(jnp.int32, (width, width), 1) // group
    return (i == j).astype(BF16)


def _rms(x, g):
    return x * lax.rsqrt(jnp.mean(x * x, axis=-1, keepdims=True) + NORM_EPS) * g


def _cat_blocks(ref, n):
    return jnp.concatenate([ref[c] for c in range(n)], axis=1)


def _put_blocks(ref, val):
    for c in range(ref.shape[0]):
        ref[c] = val[:, c * LANES:(c + 1) * LANES].astype(ref.dtype)


def _inproj_kernel(x_ref, g_ref, w_ref, o_ref, xn_ref):
    @pl.when(pl.program_id(1) == 0)
    def _():
        xn_ref[...] = _rms(x_ref[...], g_ref[...]).astype(BF16)

    _put_blocks(o_ref, jnp.dot(xn_ref[...], w_ref[...], preferred_element_type=F32))


def _inproj(x, g, w, tm, tn):
    t, d = x.shape
    n = w.shape[1]
    return pl.pallas_call(
        _inproj_kernel,
        grid=(t // tm, n // tn),
        in_specs=[pl.BlockSpec((tm, d), lambda i, j: (i, 0)),
                  pl.BlockSpec((1, d), lambda i, j: (0, 0)),
                  pl.BlockSpec((d, tn), lambda i, j: (0, j))],
        out_specs=pl.BlockSpec((tn // LANES, tm, LANES), lambda i, j: (j, i, 0)),
        out_shape=jax.ShapeDtypeStruct((n // LANES, t, LANES), F32),
        scratch_shapes=[pltpu.VMEM((tm, d), BF16)],
        compiler_params=_params("parallel", "arbitrary"),
    )(x, g.reshape(1, d), w)


def _prep_qk(x, gain, cos, sin, group):
    ones = _group_ones(LANES, group)
    ms = _dot_exact_rhs(x * x, ones) * (1.0 / group)
    y = x * lax.rsqrt(ms + NORM_EPS) * gain
    half = group // 2
    if group == LANES:
        partner = pltpu.roll(y, half, axis=1)
    else:
        lane = lax.broadcasted_iota(jnp.int32, y.shape, 1)
        fwd = pltpu.roll(y, LANES - half, axis=1)
        bwd = pltpu.roll(y, half, axis=1)
        partner = jnp.where((lane % group) < half, fwd, bwd)
    return (y * cos + partner * sin).astype(BF16)


def _rope_tables(s, group):
    half = group // 2
    lane = np.arange(LANES)
    inv = ROPE_THETA ** (-jnp.asarray(lane % half, F32) / half)
    ang = jnp.arange(s, dtype=F32)[:, None] * inv[None, :]
    sign = jnp.asarray(np.where((lane % group) < half, -1.0, 1.0), F32)
    return jnp.cos(ang), jnp.sin(ang) * sign[None, :]


def _tile_gain(gain, group):
    return jnp.tile(gain.astype(F32), LANES // group).reshape(1, LANES)


def _store_vt(vt_ref, v_ref, tq):
    for j in range(vt_ref.shape[0]):
        vt_ref[j] = v_ref[j * tq:(j + 1) * tq, :].T.astype(BF16)


def _diffattn_kernel(q_ref, k_ref, v_ref, cos_ref, sin_ref, gq_ref, gk_ref, lam_ref, go_ref, o_ref,
                     kp_ref, vt_ref, s0_ref, s1_ref, a0_ref, a1_ref, *, tq, scale, lam_init):
    qi = pl.program_id(2)

    @pl.when(qi == 0)
    def _():
        kp_ref[...] = _prep_qk(k_ref[...], gk_ref[...], cos_ref[...], sin_ref[...], DA_QK)
        _store_vt(vt_ref, v_ref, tq)

    rows = pl.ds(pl.multiple_of(qi * tq, tq), tq)
    q = _prep_qk(q_ref[...], gq_ref[...], cos_ref[rows, :], sin_ref[rows, :], DA_QK)
    lane = lax.broadcasted_iota(jnp.int32, q.shape, 1)
    zero = jnp.zeros_like(q)
    q0 = jnp.where(lane < DA_QK, q, zero)
    q1 = jnp.where(lane >= DA_QK, q, zero)

    def scores(j, carry):
        m0, m1 = carry
        kb = kp_ref[pl.ds(pl.multiple_of(j * tq, tq), tq), :]
        s0 = _dot_nt(kb, q0) * scale
        s1 = _dot_nt(kb, q1) * scale
        s0_ref[j] = s0
        s1_ref[j] = s1
        return (jnp.maximum(m0, jnp.max(s0, axis=0, keepdims=True)),
                jnp.maximum(m1, jnp.max(s1, axis=0, keepdims=True)))

    mneg = jnp.full((1, tq), NEG, F32)
    m0, m1 = lax.fori_loop(0, qi, scores, (mneg, mneg))
    kb = kp_ref[rows, :]
    causal = (lax.broadcasted_iota(jnp.int32, (tq, tq), 1)
              >= lax.broadcasted_iota(jnp.int32, (tq, tq), 0))
    s0 = jnp.where(causal, _dot_nt(kb, q0) * scale, NEG)
    s1 = jnp.where(causal, _dot_nt(kb, q1) * scale, NEG)
    s0_ref[qi] = s0
    s1_ref[qi] = s1
    m0 = jnp.maximum(m0, jnp.max(s0, axis=0, keepdims=True))
    m1 = jnp.maximum(m1, jnp.max(s1, axis=0, keepdims=True))

    a0_ref[...] = jnp.zeros(a0_ref.shape, F32)
    a1_ref[...] = jnp.zeros(a1_ref.shape, F32)

    def accum(j, carry):
        l0, l1 = carry
        vt = vt_ref[j]
        p0 = jnp.exp(s0_ref[j] - m0)
        p1 = jnp.exp(s1_ref[j] - m1)
        a0_ref[...] += _dot(vt, p0)
        a1_ref[...] += _dot(vt, p1)
        return l0 + jnp.sum(p0, axis=0, keepdims=True), l1 + jnp.sum(p1, axis=0, keepdims=True)

    lzero = jnp.zeros((1, tq), F32)
    l0, l1 = lax.fori_loop(0, qi + 1, accum, (lzero, lzero))

    lm = lam_ref[...]
    lam = (jnp.exp(jnp.sum(lm[0:1] * lm[1:2], axis=-1, keepdims=True))
           - jnp.exp(jnp.sum(lm[2:3] * lm[3:4], axis=-1, keepdims=True)) + lam_init)
    ot = a0_ref[...] / l0 - lam * (a1_ref[...] / l1)
    ot = ot * lax.rsqrt(jnp.mean(ot * ot, axis=0, keepdims=True) + NORM_EPS)
    o_ref[...] = (ot.T * go_ref[...] * (1.0 - lam_init)).astype(BF16)


def _diffattn(p, qcb, kcb, vcb, gq, gk, lam4, gout, b, s, lam_init, tq):
    t = p.shape[1]
    nq = s // tq
    cos, sin = _rope_tables(s, DA_QK)
    blk = lambda cb0, rows, rmap: pl.BlockSpec((None, rows, LANES), lambda bi, h, qi: (cb0 + h, rmap(bi, qi), 0))
    full = lambda shape: pl.BlockSpec(shape, lambda bi, h, qi: (0, 0))
    return pl.pallas_call(
        functools.partial(_diffattn_kernel, tq=tq, scale=DA_QK ** -0.5, lam_init=lam_init),
        grid=(b, DA_HEADS, nq),
        in_specs=[blk(qcb, tq, lambda bi, qi: bi * nq + qi),
                  blk(kcb, s, lambda bi, qi: bi),
                  blk(vcb, s, lambda bi, qi: bi),
                  full((s, LANES)), full((s, LANES)), full((1, LANES)), full((1, LANES)),
                  full((4, DA_QK)), full((1, LANES))],
        out_specs=blk(0, tq, lambda bi, qi: bi * nq + qi),
        out_shape=jax.ShapeDtypeStruct((DA_HEADS, t, LANES), BF16),
        scratch_shapes=[pltpu.VMEM((s, LANES), BF16), pltpu.VMEM((nq, LANES, tq), BF16),
                        pltpu.VMEM((nq, tq, tq), F32), pltpu.VMEM((nq, tq, tq), F32),
                        pltpu.VMEM((LANES, tq), F32), pltpu.VMEM((LANES, tq), F32)],
        compiler_params=_params("parallel", "parallel", "arbitrary"),
    )(p, p, p, cos, sin, _tile_gain(gq, DA_QK), _tile_gain(gk, DA_QK), lam4.astype(F32),
      gout.astype(F32).reshape(1, LANES))


def _dilated_bias(s, tq):
    nd = s // tq
    d = (np.arange(nd)[:, None, None] * tq + np.arange(tq)[None, None, :]
         - np.arange(tq)[None, :, None])
    cnt = np.zeros(d.shape, np.float64)
    for window, dil in DL_PATTERNS:
        cnt += (d >= 0) & (d % dil == 0) & (d <= window)
    bias = np.where(cnt > 0, np.log(np.maximum(cnt, 1.0)), NEG)
    return jnp.asarray(bias, F32)


def _dilattn_kernel(q_ref, k_ref, v_ref, cos_ref, sin_ref, gq_ref, gk_ref, bias_ref, o_ref,
                    kp_ref, vt_ref, s_ref, a_ref, *, tq, scale):
    qi = pl.program_id(2)

    @pl.when(qi == 0)
    def _():
        kp_ref[...] = _prep_qk(k_ref[...], gk_ref[...], cos_ref[...], sin_ref[...], LANES)
        _store_vt(vt_ref, v_ref, tq)

    rows = pl.ds(pl.multiple_of(qi * tq, tq), tq)
    q = _prep_qk(q_ref[...], gq_ref[...], cos_ref[rows, :], sin_ref[rows, :], LANES)

    def scores(j, m):
        kb = kp_ref[pl.ds(pl.multiple_of(j * tq, tq), tq), :]
        sc = _dot_nt(kb, q) * scale + bias_ref[qi - j]
        s_ref[j] = sc
        return jnp.maximum(m, jnp.max(sc, axis=0, keepdims=True))

    m = lax.fori_loop(0, qi + 1, scores, jnp.full((1, tq), NEG, F32))
    a_ref[...] = jnp.zeros(a_ref.shape, F32)

    def accum(j, l):
        pr = jnp.exp(s_ref[j] - m)
        a_ref[...] += _dot(vt_ref[j], pr)
        return l + jnp.sum(pr, axis=0, keepdims=True)

    l = lax.fori_loop(0, qi + 1, accum, jnp.zeros((1, tq), F32))
    o_ref[...] = (a_ref[...] / l).T.astype(BF16)


def _dilattn(p, qcb, kcb, vcb, gq, gk, b, s, tq):
    t = p.shape[1]
    nq = s // tq
    cos, sin = _rope_tables(s, LANES)
    bias = _dilated_bias(s, tq)
    blk = lambda cb0, rows, rmap: pl.BlockSpec((None, rows, LANES), lambda bi, h, qi: (cb0 + h, rmap(bi, qi), 0))
    full = lambda shape: pl.BlockSpec(shape, lambda bi, h, qi: (0,) * len(shape))
    return pl.pallas_call(
        functools.partial(_dilattn_kernel, tq=tq, scale=LANES ** -0.5),
        grid=(b, DL_HEADS, nq),
        in_specs=[blk(qcb, tq, lambda bi, qi: bi * nq + qi),
                  blk(kcb, s, lambda bi, qi: bi),
                  blk(vcb, s, lambda bi, qi: bi),
                  full((s, LANES)), full((s, LANES)), full((1, LANES)), full((1, LANES)),
                  full((nq, tq, tq))],
        out_specs=blk(0, tq, lambda bi, qi: bi * nq + qi),
        out_shape=jax.ShapeDtypeStruct((DL_HEADS, t, LANES), BF16),
        scratch_shapes=[pltpu.VMEM((s, LANES), BF16), pltpu.VMEM((nq, LANES, tq), BF16),
                        pltpu.VMEM((nq, tq, tq), F32), pltpu.VMEM((LANES, tq), F32)],
        compiler_params=_params("parallel", "parallel", "arbitrary"),
    )(p, p, p, cos, sin, _tile_gain(gq, LANES), _tile_gain(gk, LANES), bias)


def _rwprep_kernel(r_ref, k_ref, v_ref, x_ref, rp_ref, kp_ref, vp_ref, xp_ref,
                   mu_ref, w0_ref, a0_ref, kk_ref, ka_ref, w2_ref, a2_ref, g2_ref,
                   ro, lwo, ko, vo, nao, bo, go, *, rows_per_seq):
    i = pl.program_id(0)
    tm = r_ref.shape[1]
    first = (i * tm) % rows_per_seq == 0
    row = lax.broadcasted_iota(jnp.int32, (tm, 1), 0)

    def shifted(cur_ref, prev_ref, mu):
        n = cur_ref.shape[0]
        cur = _cat_blocks(cur_ref, n)
        last = jnp.concatenate([prev_ref[c, 7:8, :] for c in range(n)], axis=1)
        last = jnp.where(first, 0.0, last)
        prev = jnp.where(row == 0, last, pltpu.roll(cur, 1, axis=0))
        return cur + mu * (prev - cur)

    mu = mu_ref[...]
    rr = shifted(r_ref, rp_ref, mu[:, 0:RW_W])
    kr = shifted(k_ref, kp_ref, mu[:, RW_W:2 * RW_W])
    vr = shifted(v_ref, vp_ref, mu[:, 2 * RW_W:3 * RW_W])
    xs = shifted(x_ref, xp_ref, mu[:, 3 * RW_W:])

    z = w0_ref[...] + _dot3(jnp.tanh(xs), w2_ref[...])
    nz = -z
    softplus = jnp.maximum(nz, 0.0) + jnp.log(1.0 + jnp.exp(-jnp.abs(nz)))
    w_log = -softplus - 0.5
    a = jax.nn.sigmoid(a0_ref[...] + _dot3(xs, a2_ref[...]))
    g = _dot3(jax.nn.sigmoid(xs), g2_ref[...])

    ones = _group_ones(RW_W, RW_DIM)
    kkr = kr * kk_ref[...]
    ss = _dot_exact_rhs(kkr * kkr, ones)
    kk = kkr / jnp.maximum(jnp.sqrt(ss), 1e-12)
    _put_blocks(ro, rr)
    _put_blocks(lwo, -jnp.exp(w_log))
    _put_blocks(ko, kr * (1.0 + (a - 1.0) * ka_ref[...]))
    _put_blocks(vo, vr)
    _put_blocks(nao, -kk)
    _put_blocks(bo, kk * a)
    _put_blocks(go, g)


def _rwprep(p, cb0, s, mu, w0, a0, k_k, k_a, w2, a2, g2, tm):
    t = p.shape[1]
    lr = mu.shape[0] - 3 * RW_W
    nx = lr // LANES
    w2p = jnp.zeros((lr, RW_W), F32).at[0:w2.shape[0]].set(w2)
    a2p = jnp.zeros((lr, RW_W), F32).at[w2.shape[0]:w2.shape[0] + a2.shape[0]].set(a2)
    g2p = jnp.zeros((lr, RW_W), F32).at[lr - g2.shape[0]:].set(g2)
    r8 = tm // 8
    gb = cb0 // RW_PAIRS
    xb = (cb0 + 3 * RW_PAIRS) // nx

    def cur(c, n):
        return pl.BlockSpec((n, tm, LANES), lambda i: (c, i, 0))

    def prev(c, n):
        return pl.BlockSpec((n, 8, LANES), lambda i: (c, jnp.maximum(i * r8 - 1, 0), 0))

    def full(shape):
        return pl.BlockSpec(shape, lambda i: (0, 0))

    vec = lambda a_: a_.astype(F32).reshape(1, -1)
    out = jax.ShapeDtypeStruct((RW_PAIRS, t, LANES), F32)
    return pl.pallas_call(
        functools.partial(_rwprep_kernel, rows_per_seq=s),
        grid=(t // tm,),
        in_specs=[cur(gb, RW_PAIRS), cur(gb + 1, RW_PAIRS), cur(gb + 2, RW_PAIRS), cur(xb, nx),
                  prev(gb, RW_PAIRS), prev(gb + 1, RW_PAIRS), prev(gb + 2, RW_PAIRS), prev(xb, nx),
                  full((1, 3 * RW_W + lr)), full((1, RW_W)), full((1, RW_W)), full((1, RW_W)),
                  full((1, RW_W)), full((lr, RW_W)), full((lr, RW_W)), full((lr, RW_W))],
        out_specs=[pl.BlockSpec((RW_PAIRS, tm, LANES), lambda i: (0, i, 0))] * 7,
        out_shape=[out] * 7,
        compiler_params=_params("parallel"),
    )(p, p, p, p, p, p, p, p, vec(mu), vec(w0), vec(a0), vec(k_k), vec(k_a), w2p, a2p, g2p)


def _rwkv_kernel(r_ref, lw_ref, k_ref, v_ref, a_ref, b_ref, y_ref, rh_ref, yh_ref, p_ref, q_ref,
                 st_ref, *, nchunk, unroll):
    L = CHUNK
    W = 2 * L
    npair = r_ref.shape[0]

    @pl.when(pl.program_id(1) == 0)
    def _():
        st_ref[...] = jnp.zeros(st_ref.shape, F32)

    lane = lax.broadcasted_iota(jnp.int32, (L, W), 1)
    rowi = lax.broadcasted_iota(jnp.int32, (L, W), 0)
    strict = (lane % L) < rowi
    incl = (lane % L) <= rowi
    rr = lax.broadcasted_iota(jnp.int32, (W, W), 0)
    cc = lax.broadcasted_iota(jnp.int32, (W, W), 1)
    same = (rr // L) == (cc // L)
    eye = rr == cc
    tl = lax.broadcasted_iota(jnp.int32, (L, L), 0)
    sl = lax.broadcasted_iota(jnp.int32, (L, L), 1)
    tril = (sl <= tl).astype(BF16)

    def bd(x):
        return jnp.where(same, jnp.concatenate([x, x], axis=0), 0.0)

    def group(gi, carry):
        us = [gi * unroll + i for i in range(unroll)]
        prs = [u // nchunk for u in us]
        rws = [pl.ds(pl.multiple_of((u % nchunk) * L, L), L) for u in us]
        G = range(unroll)
        ld = lambda ref: [ref[prs[i], rws[i], :] for i in G]
        r, lw, k, v, a, b = ld(r_ref), ld(lw_ref), ld(k_ref), ld(v_ref), ld(a_ref), ld(b_ref)

        def csum(x):
            hi, lo = _split(x)
            rest = x - hi.astype(F32) - lo.astype(F32)
            return jnp.concatenate([hi, lo, rest.astype(BF16)], axis=1)

        c3 = [_dot(tril, csum(lw[i])) for i in G]
        cin = [c[:, 0:W] + c[:, W:2 * W] + c[:, 2 * W:] for c in c3]
        clast = [c[L - 1:L, :] for c in cin]
        g_inv = [jnp.exp(-c) for c in cin]
        g_tail = [jnp.exp(clast[i] - cin[i]) for i in G]
        at = [a[i] * jnp.exp(cin[i] - lw[i]) for i in G]
        rt = [r[i] * jnp.exp(cin[i]) for i in G]
        abk = [_dot_nt(jnp.concatenate([at[i], rt[i]], axis=0),
                       jnp.concatenate([bd(b[i] * g_inv[i]), bd(k[i] * g_inv[i])], axis=0)) for i in G]
        a_ab = [jnp.where(strict, m[0:L, 0:W], 0.0) for m in abk]
        a_rb = [jnp.where(incl, m[L:W, 0:W], 0.0) for m in abk]
        a_ak = [jnp.where(strict, m[0:L, W:], 0.0) for m in abk]
        a_rk = [jnp.where(incl, m[L:W, W:], 0.0) for m in abk]
        n = [bd(m) for m in a_ab]
        tm = [jnp.where(eye, 1.0, m) for m in n]
        x = [_dot(m, m) for m in n]
        for j in range(5):
            if j < 4:
                xx = [_dot(x[i], jnp.concatenate([x[i], tm[i]], axis=1)) for i in G]
                x = [m[:, 0:W] for m in xx]
                tm = [tm[i] + xx[i][:, W:] for i in G]
            else:
                tm = [tm[i] + _dot(x[i], tm[i]) for i in G]
        v_bd = [bd(m) for m in v]
        kv = [_dot(jnp.concatenate([bd(a_ak[i]), a_rk[i]], axis=0), v_bd[i]) for i in G]
        au = [_dot(tm[i], jnp.concatenate([bd(at[i]), kv[i][0:W]], axis=1)) for i in G]
        ry = [_dot(a_rb[i], au[i]) for i in G]
        zero = jnp.zeros((W, W), F32)
        pq = [_dot_tn(jnp.concatenate([bd(b[i] * g_tail[i]), bd(k[i] * g_tail[i])], axis=0),
                      jnp.concatenate([au[i], jnp.concatenate([zero, v_bd[i]], axis=1)], axis=0))
              for i in G]
        for i in G:
            rh_ref[prs[i], rws[i], :] = rt[i] + ry[i][:, 0:W]
            yh_ref[prs[i], rws[i], :] = ry[i][:, W:] + kv[i][W:]
            p_ref[us[i]] = jnp.where(eye, jnp.exp(clast[i]), 0.0) + pq[i][:, 0:W]
            q_ref[us[i]] = pq[i][:, W:]
        return carry

    lax.fori_loop(0, npair * nchunk // unroll, group, 0)

    def step(c, carry):
        rows = pl.ds(pl.multiple_of(c * L, L), L)
        for pr in range(npair):
            st = st_ref[pr].astype(BF16)
            y_ref[pr, rows, :] = _dot(rh_ref[pr, rows, :], st) + yh_ref[pr, rows, :]
            st_ref[pr] = _dot(p_ref[pr * nchunk + c], st) + q_ref[pr * nchunk + c]
        return carry

    lax.fori_loop(0, nchunk, step, 0)


def _rwkv(r, lw, k, v, na, bb, b, s, sb, unroll):
    npair, t, w = r.shape
    nchunk = sb // CHUNK
    nsb = s // sb
    spec = pl.BlockSpec((npair, sb, w), lambda bi, si: (0, bi * nsb + si, 0))
    return pl.pallas_call(
        functools.partial(_rwkv_kernel, nchunk=nchunk, unroll=unroll),
        grid=(b, nsb),
        in_specs=[spec] * 6,
        out_specs=spec,
        out_shape=jax.ShapeDtypeStruct((npair, t, w), F32),
        scratch_shapes=[pltpu.VMEM((npair, sb, w), F32), pltpu.VMEM((npair, sb, w), F32),
                        pltpu.VMEM((npair * nchunk, w, w), F32), pltpu.VMEM((npair * nchunk, w, w), F32),
                        pltpu.VMEM((npair, w, w), F32)],
        compiler_params=_params("parallel", "arbitrary"),
    )(r, lw, k, v, na, bb)


def _rwpost_kernel(y_ref, r_ref, k_ref, v_ref, g_ref, lng_ref, lnb_ref, rk_ref, o_ref):
    ones = _group_ones(RW_W, RW_DIM)
    y = _cat_blocks(y_ref, RW_PAIRS)
    mu = _dot_exact_rhs(y, ones) * (1.0 / RW_DIM)
    yc = y - mu
    var = _dot_exact_rhs(yc * yc, ones) * (1.0 / RW_DIM)
    out = yc * lax.rsqrt(var + RW_LN_EPS) * lng_ref[...] + lnb_ref[...]
    v = _cat_blocks(v_ref, RW_PAIRS)
    bonus = _dot_exact_rhs(_cat_blocks(r_ref, RW_PAIRS) * _cat_blocks(k_ref, RW_PAIRS) * rk_ref[...], ones)
    _put_blocks(o_ref, (out + bonus * v) * _cat_blocks(g_ref, RW_PAIRS))


def _rwpost(y, r, k, v, g, ln_g, ln_b, r_k, tm):
    t = y.shape[1]
    big = pl.BlockSpec((RW_PAIRS, tm, LANES), lambda i: (0, i, 0))
    small = pl.BlockSpec((1, RW_W), lambda i: (0, 0))
    vec = lambda a_: a_.astype(F32).reshape(1, RW_W)
    return pl.pallas_call(
        _rwpost_kernel,
        grid=(t // tm,),
        in_specs=[big] * 5 + [small] * 3,
        out_specs=big,
        out_shape=jax.ShapeDtypeStruct((RW_PAIRS, t, LANES), BF16),
        compiler_params=_params("parallel"),
    )(y, r, k, v, g, vec(ln_g), vec(ln_b), vec(r_k))


def _outproj_kernel(x_ref, oa_ref, oc_ref, ob_ref, w_ref, o_ref):
    mix = jnp.concatenate([_cat_blocks(oa_ref, oa_ref.shape[0]), _cat_blocks(oc_ref, oc_ref.shape[0]),
                           _cat_blocks(ob_ref, ob_ref.shape[0])], axis=1)
    o_ref[...] = x_ref[...] + jnp.dot(mix, w_ref[...], preferred_element_type=F32)


def _outproj(x, oa, oc, ob, w, tm, tn):
    t, d = x.shape
    blocks = lambda a_: pl.BlockSpec((a_.shape[0], tm, LANES), lambda i, j: (0, i, 0))
    return pl.pallas_call(
        _outproj_kernel,
        grid=(t // tm, d // tn),
        in_specs=[pl.BlockSpec((tm, tn), lambda i, j: (i, j)),
                  blocks(oa), blocks(oc), blocks(ob),
                  pl.BlockSpec((w.shape[0], tn), lambda i, j: (0, j))],
        out_specs=pl.BlockSpec((tm, tn), lambda i, j: (i, j)),
        out_shape=jax.ShapeDtypeStruct((t, d), F32),
        compiler_params=_params("parallel", "arbitrary"),
    )(x, oa, oc, ob, w)


def _swiglu_tile(x, wg, wu):
    gate = jnp.dot(x, wg, preferred_element_type=F32)
    up = jnp.dot(x, wu, preferred_element_type=F32)
    return (gate * jax.nn.sigmoid(gate) * up).astype(BF16)


def _ffn_kernel(x_ref, g_ref, wg_ref, wu_ref, wd_ref, o_ref, xn_ref, acc_ref):
    f = pl.program_id(1)

    @pl.when(f == 0)
    def _():
        xn_ref[...] = _rms(x_ref[...], g_ref[...]).astype(BF16)
        acc_ref[...] = jnp.zeros(acc_ref.shape, F32)

    act = _swiglu_tile(xn_ref[...], wg_ref[...], wu_ref[...])
    acc_ref[...] += jnp.dot(act, wd_ref[...], preferred_element_type=F32)

    @pl.when(f == pl.num_programs(1) - 1)
    def _():
        o_ref[...] = x_ref[...] + acc_ref[...]


def _ffn(x, g, wg, wu, wd, tm, tf):
    t, d = x.shape
    ff = wg.shape[1]
    return pl.pallas_call(
        _ffn_kernel,
        grid=(t // tm, ff // tf),
        in_specs=[pl.BlockSpec((tm, d), lambda i, f: (i, 0)),
                  pl.BlockSpec((1, d), lambda i, f: (0, 0)),
                  pl.BlockSpec((d, tf), lambda i, f: (0, f)),
                  pl.BlockSpec((d, tf), lambda i, f: (0, f)),
                  pl.BlockSpec((tf, d), lambda i, f: (f, 0))],
        out_specs=pl.BlockSpec((tm, d), lambda i, f: (i, 0)),
        out_shape=jax.ShapeDtypeStruct((t, d), F32),
        scratch_shapes=[pltpu.VMEM((tm, d), BF16), pltpu.VMEM((tm, d), F32)],
        compiler_params=_params("parallel", "arbitrary"),
    )(x, g.reshape(1, d), wg, wu, wd)


def _router_kernel(x_ref, g_ref, wr_ref, h_ref, comb_ref, combt_ref, rcol_ref, rrow_ref, cnt_ref,
                   *, n_experts):
    h = _rms(x_ref[...], g_ref[...])
    h_ref[...] = h.astype(BF16)
    logits = _dot3(h, wr_ref[...])
    lane = lax.broadcasted_iota(jnp.int32, logits.shape, 1)
    lg = jnp.where(lane < n_experts, logits, NEG)
    m1 = jnp.max(lg, axis=-1, keepdims=True)
    i1 = jnp.min(jnp.where(lg == m1, lane, LANES), axis=-1, keepdims=True)
    lg2 = jnp.where(lane == i1, NEG, lg)
    m2 = jnp.max(lg2, axis=-1, keepdims=True)
    i2 = jnp.min(jnp.where(lg2 == m2, lane, LANES), axis=-1, keepdims=True)
    e2 = jnp.exp(m2 - m1)
    w1 = 1.0 / (1.0 + e2)
    comb = jnp.where(lane == i1, w1, 0.0) + jnp.where(lane == i2, e2 * w1, 0.0)
    combt = comb.T[0:combt_ref.shape[0], :]
    comb_ref[...] = comb
    combt_ref[...] = combt
    ts = comb.shape[0]
    tt = lax.broadcasted_iota(jnp.int32, (ts, ts), 0)
    uu = lax.broadcasted_iota(jnp.int32, (ts, ts), 1)
    live = jnp.where(comb > 0.0, 1.0, 0.0)
    rcol_ref[...] = _dot((uu < tt).astype(BF16), live)
    rrow_ref[...] = _dot(jnp.where(combt > 0.0, 1.0, 0.0), (tt < uu).astype(BF16))
    cnt_ref[0] = jnp.sum(live, axis=0, keepdims=True).astype(jnp.int32)


def _router(x, g, wr, ts):
    t, d = x.shape
    e = wr.shape[1]
    ep = max(8, e)
    wrp = jnp.zeros((d, LANES), F32).at[:, :e].set(wr)
    nt = t // ts
    return pl.pallas_call(
        functools.partial(_router_kernel, n_experts=e),
        grid=(nt,),
        in_specs=[pl.BlockSpec((ts, d), lambda i: (i, 0)),
                  pl.BlockSpec((1, d), lambda i: (0, 0)),
                  pl.BlockSpec((d, LANES), lambda i: (0, 0))],
        out_specs=[pl.BlockSpec((ts, d), lambda i: (i, 0)),
                   pl.BlockSpec((ts, LANES), lambda i: (i, 0)),
                   pl.BlockSpec((ep, ts), lambda i: (0, i)),
                   pl.BlockSpec((ts, LANES), lambda i: (i, 0)),
                   pl.BlockSpec((ep, ts), lambda i: (0, i)),
                   pl.BlockSpec((1, 1, LANES), lambda i: (i, 0, 0))],
        out_shape=[jax.ShapeDtypeStruct((t, d), BF16),
                   jax.ShapeDtypeStruct((t, LANES), F32),
                   jax.ShapeDtypeStruct((ep, t), F32),
                   jax.ShapeDtypeStruct((t, LANES), F32),
                   jax.ShapeDtypeStruct((ep, t), F32),
                   jax.ShapeDtypeStruct((nt, 1, LANES), jnp.int32)],
        compiler_params=_params("parallel"),
    )(x, g.reshape(1, d), wrp)


def _moe_kernel(cnt_ref, h_ref, x_ref, comb_ref, combt_ref, rcol_ref, rrow_ref, wg_ref, wu_ref, wd_ref,
                o_ref, xc_ref, yc_ref, *, cm, n_experts):
    i, e, f = pl.program_id(0), pl.program_id(1), pl.program_id(2)
    ts = h_ref.shape[0]
    nch = (cnt_ref[i * n_experts + e] + cm - 1) // cm

    @pl.when((e == 0) & (f == 0))
    def _():
        o_ref[...] = x_ref[...]

    @pl.when(f == 0)
    def _():
        pos = rrow_ref[pl.ds(e, 1), :]
        live = combt_ref[pl.ds(e, 1), :] > 0.0
        slot = lax.broadcasted_iota(jnp.int32, (cm, ts), 0).astype(F32)

        def gather(c, carry):
            rows = pl.ds(pl.multiple_of(c * cm, cm), cm)
            base = (c * cm).astype(F32)
            onehot = jnp.where((pos == slot + base) & live, 1.0, 0.0).astype(BF16)
            xc_ref[rows, :] = jnp.dot(onehot, h_ref[...], preferred_element_type=F32).astype(BF16)
            yc_ref[rows, :] = jnp.zeros((cm, yc_ref.shape[1]), F32)
            return carry

        lax.fori_loop(0, nch, gather, 0)

    def ffn(c, carry):
        rows = pl.ds(pl.multiple_of(c * cm, cm), cm)
        act = _swiglu_tile(xc_ref[rows, :], wg_ref[...], wu_ref[...])
        yc_ref[rows, :] += jnp.dot(act, wd_ref[...], preferred_element_type=F32)
        return carry

    lax.fori_loop(0, nch, ffn, 0)

    @pl.when(f == pl.num_programs(2) - 1)
    def _():
        lane = lax.broadcasted_iota(jnp.int32, comb_ref.shape, 1)
        sel = lane == e
        cw = jnp.sum(jnp.where(sel, comb_ref[...], 0.0), axis=-1, keepdims=True)
        pos = jnp.sum(jnp.where(sel, rcol_ref[...], 0.0), axis=-1, keepdims=True)
        slot = lax.broadcasted_iota(jnp.int32, (ts, cm), 1).astype(F32)

        def scatter(c, carry):
            rows = pl.ds(pl.multiple_of(c * cm, cm), cm)
            base = (c * cm).astype(F32)
            onehot = jnp.where((pos == slot + base) & (cw > 0.0), 1.0, 0.0).astype(BF16)
            o_ref[...] += cw * jnp.dot(onehot, yc_ref[rows, :].astype(BF16),
                                       preferred_element_type=F32)
            return carry

        lax.fori_loop(0, nch, scatter, 0)


def _moe(x, h, comb, combt, rcol, rrow, counts, wg, wu, wd, ts, tf, cm):
    t, d = x.shape
    ne, _, ff = wg.shape
    ep = combt.shape[0]
    one = pl.Buffered(1)
    grid_spec = pltpu.PrefetchScalarGridSpec(
        num_scalar_prefetch=1,
        grid=(t // ts, ne, ff // tf),
        in_specs=[pl.BlockSpec((ts, d), lambda i, e, f, c: (i, 0), pipeline_mode=one),
                  pl.BlockSpec((ts, d), lambda i, e, f, c: (i, 0), pipeline_mode=one),
                  pl.BlockSpec((ts, LANES), lambda i, e, f, c: (i, 0)),
                  pl.BlockSpec((ep, ts), lambda i, e, f, c: (0, i)),
                  pl.BlockSpec((ts, LANES), lambda i, e, f, c: (i, 0)),
                  pl.BlockSpec((ep, ts), lambda i, e, f, c: (0, i)),
                  pl.BlockSpec((None, d, tf), lambda i, e, f, c: (e, 0, f)),
                  pl.BlockSpec((None, d, tf), lambda i, e, f, c: (e, 0, f)),
                  pl.BlockSpec((None, tf, d), lambda i, e, f, c: (e, f, 0))],
        out_specs=pl.BlockSpec((ts, d), lambda i, e, f, c: (i, 0), pipeline_mode=one),
        scratch_shapes=[pltpu.VMEM((ts, d), BF16), pltpu.VMEM((ts, d), F32)],
    )
    return pl.pallas_call(
        functools.partial(_moe_kernel, cm=cm, n_experts=ne),
        grid_spec=grid_spec,
        out_shape=jax.ShapeDtypeStruct((t, d), F32),
        compiler_params=_params("parallel", "arbitrary", "arbitrary"),
    )(counts, h, x, comb, combt, rcol, rrow, wg, wu, wd)


def kernel(x, norm1_g, w_in, da_q_norm, da_k_norm, da_lambda, da_out_norm, dl_q_norm, dl_k_norm, rw_mu, rw_w0, rw_w2, rw_a0, rw_a2, rw_g2, rw_k_k, rw_k_a, rw_r_k, rw_ln_g, rw_ln_b, w_out, norm2_g, ffn_w_gate, ffn_w_up, ffn_w_down, moe_router, moe_w_gate, moe_w_up, moe_w_down):
    b, s, d = x.shape
    depth = w_in.shape[0]
    t = b * s
    n_experts = moe_router.shape[-1]
    xt = x.reshape(t, d)

    qa, ka, va = 0, DA_HEADS, 2 * DA_HEADS
    qb, kb, vb = 3 * DA_HEADS, 3 * DA_HEADS + DL_HEADS, 3 * DA_HEADS + 2 * DL_HEADS
    rw = 3 * DA_HEADS + 3 * DL_HEADS

    tm = min(1024, t)
    tq = min(256, s)

    for l in range(depth):
        p = _inproj(xt, norm1_g[l], w_in[l].astype(BF16), tm, 1280)

        lam_init = 0.8 - 0.6 * math.exp(-0.3 * l)
        oa = _diffattn(p, qa, ka, va, da_q_norm[l], da_k_norm[l], da_lambda[l], da_out_norm[l],
                       b, s, lam_init, tq)
        oc = _dilattn(p, qb, kb, vb, dl_q_norm[l], dl_k_norm[l], b, s, tq)

        r, lw, k2, v, na, bb, g = _rwprep(p, rw, s, rw_mu[l], rw_w0[l], rw_a0[l], rw_k_k[l],
                                          rw_k_a[l], rw_w2[l], rw_a2[l], rw_g2[l], min(256, s))
        y = _rwkv(r, lw, k2, v, na, bb, b, s, min(512, s), 12)
        ob = _rwpost(y, r, k2, v, g, rw_ln_g[l], rw_ln_b[l], rw_r_k[l].reshape(-1), min(512, s))

        xt = _outproj(xt, oa, oc, ob, w_out[l].astype(BF16), tm, 1024)

        i = l // 2
        if l % 2 == 0:
            xt = _ffn(xt, norm2_g[l], ffn_w_gate[i].astype(BF16), ffn_w_up[i].astype(BF16),
                      ffn_w_down[i].astype(BF16), min(512, t), 512)
        else:
            ts = min(1024, t)
            h, comb, combt, rcol, rrow, cnt = _router(xt, norm2_g[l], moe_router[i], ts)
            counts = cnt[:, 0, :n_experts].reshape(-1)
            xt = _moe(xt, h, comb, combt, rcol, rrow, counts, moe_w_gate[i].astype(BF16),
                      moe_w_up[i].astype(BF16), moe_w_down[i].astype(BF16), ts, 512, 128)
    return xt.reshape(b, s, d)
```

```python
import functools
import math

import numpy as np
import jax
import jax.numpy as jnp
from jax import lax
from jax.experimental import pallas as pl
from jax.experimental.pallas import tpu as pltpu

F32 = jnp.float32
BF16 = jnp.bfloat16

LANES = 128
VMEM_LIMIT = 56 * 1024 * 1024

NEG = -1e30
ROPE_THETA = 10000.0
NORM_EPS = 1e-6
RW_LN_EPS = 64e-5
DL_PATTERNS = ((128, 1), (512, 4), (2048, 16))
TOP_K = 2

DA_HEADS, DA_QK = 4, 64
DL_HEADS = 6
RW_HEADS, RW_DIM = 12, 64
DA_W, DL_W, RW_W = 512, 768, 768
RW_PAIRS = RW_W // LANES
CHUNK = 64


def _params(*sem):
    return pltpu.CompilerParams(dimension_semantics=sem, vmem_limit_bytes=VMEM_LIMIT)


def _dot(a, b):
    return jnp.dot(a.astype(BF16), b.astype(BF16), preferred_element_type=F32)


def _dot_nt(a, b):
    return lax.dot_general(a.astype(BF16), b.astype(BF16), (((1,), (1,)), ((), ())),
                           preferred_element_type=F32)


def _dot_tn(a, b):
    return lax.dot_general(a.astype(BF16), b.astype(BF16), (((0,), (0,)), ((), ())),
                           preferred_element_type=F32)


def _split(x):
    hi = x.astype(BF16)
    lo = (x - hi.astype(F32)).astype(BF16)
    return hi, lo


def _dot3(a, b):
    ah, al = _split(a)
    bh, bl = _split(b)
    return _dot(ah, bh) + _dot(ah, bl) + _dot(al, bh)


def _dot_exact_rhs(a, b_bf16):
    ah, al = _split(a)
    return _dot(ah, b_bf16) + _dot(al, b_bf16)


def _group_ones(width, group):
    i = lax.broadcasted_iota(jnp.int32, (width, width), 0) // group
    j = lax.broadcasted_iota(jnp.int32, (width, width), 1) // group
    return (i == j).astype(BF16)


def _rms(x, g):
    return x * lax.rsqrt(jnp.mean(x * x, axis=-1, keepdims=True) + NORM_EPS) * g


def _cat_blocks(ref, n):
    return jnp.concatenate([ref[c] for c in range(n)], axis=1)


def _put_blocks(ref, val):
    for c in range(ref.shape[0]):
        ref[c] = val[:, c * LANES:(c + 1) * LANES].astype(ref.dtype)


def _inproj_kernel(x_ref, g_ref, w_ref, o_ref, xn_ref):
    @pl.when(pl.program_id(1) == 0)
    def _():
        xn_ref[...] = _rms(x_ref[...], g_ref[...]).astype(BF16)

    _put_blocks(o_ref, jnp.dot(xn_ref[...], w_ref[...], preferred_element_type=F32))


def _inproj(x, g, w, tm, tn):
    t, d = x.shape
    n = w.shape[1]
    return pl.pallas_call(
        _inproj_kernel,
        grid=(t // tm, n // tn),
        in_specs=[pl.BlockSpec((tm, d), lambda i, j: (i, 0)),
                  pl.BlockSpec((1, d), lambda i, j: (0, 0)),
                  pl.BlockSpec((d, tn), lambda i, j: (0, j))],
        out_specs=pl.BlockSpec((tn // LANES, tm, LANES), lambda i, j: (j, i, 0)),
        out_shape=jax.ShapeDtypeStruct((n // LANES, t, LANES), F32),
        scratch_shapes=[pltpu.VMEM((tm, d), BF16)],
        compiler_params=_params("parallel", "arbitrary"),
    )(x, g.reshape(1, d), w)


def _prep_qk(x, gain, cos, sin, group):
    ones = _group_ones(LANES, group)
    ms = _dot_exact_rhs(x * x, ones) * (1.0 / group)
    y = x * lax.rsqrt(ms + NORM_EPS) * gain
    half = group // 2
    if group == LANES:
        partner = pltpu.roll(y, half, axis=1)
    else:
        lane = lax.broadcasted_iota(jnp.int32, y.shape, 1)
        fwd = pltpu.roll(y, LANES - half, axis=1)
        bwd = pltpu.roll(y, half, axis=1)
        partner = jnp.where((lane % group) < half, fwd, bwd)
    return (y * cos + partner * sin).astype(BF16)


def _rope_tables(s, group):
    half = group // 2
    lane = np.arange(LANES)
    inv = ROPE_THETA ** (-jnp.asarray(lane % half, F32) / half)
    ang = jnp.arange(s, dtype=F32)[:, None] * inv[None, :]
    sign = jnp.asarray(np.where((lane % group) < half, -1.0, 1.0), F32)
    return jnp.cos(ang), jnp.sin(ang) * sign[None, :]


def _tile_gain(gain, group):
    return jnp.tile(gain.astype(F32), LANES // group).reshape(1, LANES)


def _block_pairs(nq):
    return [(j, qi) for j in range(nq) for qi in range(j, nq)]


def _vt_blocks(v_ref, tq, nq):
    return [v_ref[j * tq:(j + 1) * tq, :].T.astype(BF16) for j in range(nq)]


def _diffattn_kernel(q_ref, k_ref, v_ref, cos_ref, sin_ref, gq_ref, gk_ref, lam_ref, go_ref, o_ref,
                     s0_ref, s1_ref, *, tq, scale, lam_init):
    nq = q_ref.shape[0] // tq
    cos, sin = cos_ref[...], sin_ref[...]
    kp = _prep_qk(k_ref[...], gk_ref[...], cos, sin, DA_QK)
    qp = _prep_qk(q_ref[...], gq_ref[...], cos, sin, DA_QK)
    lane = lax.broadcasted_iota(jnp.int32, qp.shape, 1)
    zero = jnp.zeros_like(qp)
    qs = (jnp.where(lane < DA_QK, qp, zero), jnp.where(lane >= DA_QK, qp, zero))
    vt = _vt_blocks(v_ref, tq, nq)
    blk = lambda x, i: x[i * tq:(i + 1) * tq, :]
    causal = (lax.broadcasted_iota(jnp.int32, (tq, tq), 1)
              >= lax.broadcasted_iota(jnp.int32, (tq, tq), 0))
    pairs = _block_pairs(nq)
    s_refs = (s0_ref, s1_ref)

    m = [[jnp.full((1, tq), NEG, F32) for _ in range(nq)] for _ in range(2)]
    for idx, (j, qi) in enumerate(pairs):
        for c in range(2):
            sc = _dot_nt(blk(kp, j), blk(qs[c], qi)) * scale
            if j == qi:
                sc = jnp.where(causal, sc, NEG)
            s_refs[c][idx] = sc
            m[c][qi] = jnp.maximum(m[c][qi], jnp.max(sc, axis=0, keepdims=True))

    l = [[jnp.zeros((1, tq), F32) for _ in range(nq)] for _ in range(2)]
    acc = [[None] * nq for _ in range(2)]
    for idx, (j, qi) in enumerate(pairs):
        for c in range(2):
            pr = jnp.exp(s_refs[c][idx] - m[c][qi])
            l[c][qi] = l[c][qi] + jnp.sum(pr, axis=0, keepdims=True)
            d = _dot(vt[j], pr)
            acc[c][qi] = d if acc[c][qi] is None else acc[c][qi] + d

    lm = lam_ref[...]
    lam = (jnp.exp(jnp.sum(lm[0:1] * lm[1:2], axis=-1, keepdims=True))
           - jnp.exp(jnp.sum(lm[2:3] * lm[3:4], axis=-1, keepdims=True)) + lam_init)
    for qi in range(nq):
        ot = acc[0][qi] / l[0][qi] - lam * (acc[1][qi] / l[1][qi])
        ot = ot * lax.rsqrt(jnp.mean(ot * ot, axis=0, keepdims=True) + NORM_EPS)
        o_ref[qi * tq:(qi + 1) * tq, :] = (ot.T * go_ref[...] * (1.0 - lam_init)).astype(BF16)


def _diffattn(p, qcb, kcb, vcb, gq, gk, lam4, gout, b, s, lam_init, tq):
    t = p.shape[1]
    nq = s // tq
    npair = nq * (nq + 1) // 2
    cos, sin = _rope_tables(s, DA_QK)
    blk = lambda cb0: pl.BlockSpec((None, s, LANES), lambda bi, h: (cb0 + h, bi, 0))
    full = lambda shape: pl.BlockSpec(shape, lambda bi, h: (0, 0))
    return pl.pallas_call(
        functools.partial(_diffattn_kernel, tq=tq, scale=DA_QK ** -0.5, lam_init=lam_init),
        grid=(b, DA_HEADS),
        in_specs=[blk(qcb), blk(kcb), blk(vcb),
                  full((s, LANES)), full((s, LANES)), full((1, LANES)), full((1, LANES)),
                  full((4, DA_QK)), full((1, LANES))],
        out_specs=blk(0),
        out_shape=jax.ShapeDtypeStruct((DA_HEADS, t, LANES), BF16),
        scratch_shapes=[pltpu.VMEM((npair, tq, tq), F32), pltpu.VMEM((npair, tq, tq), F32)],
        compiler_params=_params("parallel", "parallel"),
    )(p, p, p, cos, sin, _tile_gain(gq, DA_QK), _tile_gain(gk, DA_QK), lam4.astype(F32),
      gout.astype(F32).reshape(1, LANES))


def _dilated_bias(s, tq):
    nd = s // tq
    d = (np.arange(nd)[:, None, None] * tq + np.arange(tq)[None, None, :]
         - np.arange(tq)[None, :, None])
    cnt = np.zeros(d.shape, np.float64)
    for window, dil in DL_PATTERNS:
        cnt += (d >= 0) & (d % dil == 0) & (d <= window)
    bias = np.where(cnt > 0, np.log(np.maximum(cnt, 1.0)), NEG)
    return jnp.asarray(bias, F32)


def _dilattn_kernel(q_ref, k_ref, v_ref, cos_ref, sin_ref, gq_ref, gk_ref, bias_ref, o_ref, s_ref,
                    *, tq, scale):
    nq = q_ref.shape[0] // tq
    cos, sin = cos_ref[...], sin_ref[...]
    kp = _prep_qk(k_ref[...], gk_ref[...], cos, sin, LANES)
    qp = _prep_qk(q_ref[...], gq_ref[...], cos, sin, LANES)
    vt = _vt_blocks(v_ref, tq, nq)
    blk = lambda x, i: x[i * tq:(i + 1) * tq, :]
    pairs = _block_pairs(nq)

    m = [jnp.full((1, tq), NEG, F32) for _ in range(nq)]
    for idx, (j, qi) in enumerate(pairs):
        sc = _dot_nt(blk(kp, j), blk(qp, qi)) * scale + bias_ref[qi - j]
        s_ref[idx] = sc
        m[qi] = jnp.maximum(m[qi], jnp.max(sc, axis=0, keepdims=True))

    l = [jnp.zeros((1, tq), F32) for _ in range(nq)]
    acc = [None] * nq
    for idx, (j, qi) in enumerate(pairs):
        pr = jnp.exp(s_ref[idx] - m[qi])
        l[qi] = l[qi] + jnp.sum(pr, axis=0, keepdims=True)
        d = _dot(vt[j], pr)
        acc[qi] = d if acc[qi] is None else acc[qi] + d

    for qi in range(nq):
        o_ref[qi * tq:(qi + 1) * tq, :] = (acc[qi] / l[qi]).T.astype(BF16)


def _dilattn(p, qcb, kcb, vcb, gq, gk, b, s, tq):
    t = p.shape[1]
    nq = s // tq
    cos, sin = _rope_tables(s, LANES)
    bias = _dilated_bias(s, tq)
    blk = lambda cb0: pl.BlockSpec((None, s, LANES), lambda bi, h: (cb0 + h, bi, 0))
    full = lambda shape: pl.BlockSpec(shape, lambda bi, h: (0,) * len(shape))
    return pl.pallas_call(
        functools.partial(_dilattn_kernel, tq=tq, scale=LANES ** -0.5),
        grid=(b, DL_HEADS),
        in_specs=[blk(qcb), blk(kcb), blk(vcb),
                  full((s, LANES)), full((s, LANES)), full((1, LANES)), full((1, LANES)),
                  full((nq, tq, tq))],
        out_specs=blk(0),
        out_shape=jax.ShapeDtypeStruct((DL_HEADS, t, LANES), BF16),
        scratch_shapes=[pltpu.VMEM((nq * (nq + 1) // 2, tq, tq), F32)],
        compiler_params=_params("parallel", "parallel"),
    )(p, p, p, cos, sin, _tile_gain(gq, LANES), _tile_gain(gk, LANES), bias)


def _rwprep_kernel(r_ref, k_ref, v_ref, x_ref, rp_ref, kp_ref, vp_ref, xp_ref,
                   mu_ref, w0_ref, a0_ref, kk_ref, ka_ref, w2_ref, a2_ref, g2_ref,
                   ro, lwo, ko, vo, nao, bo, go, *, rows_per_seq):
    i = pl.program_id(0)
    tm = r_ref.shape[1]
    first = (i * tm) % rows_per_seq == 0
    row = lax.broadcasted_iota(jnp.int32, (tm, 1), 0)

    def shifted(cur_ref, prev_ref, mu):
        n = cur_ref.shape[0]
        cur = _cat_blocks(cur_ref, n)
        last = jnp.concatenate([prev_ref[c, 7:8, :] for c in range(n)], axis=1)
        last = jnp.where(first, 0.0, last)
        prev = jnp.where(row == 0, last, pltpu.roll(cur, 1, axis=0))
        return cur + mu * (prev - cur)

    mu = mu_ref[...]
    rr = shifted(r_ref, rp_ref, mu[:, 0:RW_W])
    kr = shifted(k_ref, kp_ref, mu[:, RW_W:2 * RW_W])
    vr = shifted(v_ref, vp_ref, mu[:, 2 * RW_W:3 * RW_W])
    xs = shifted(x_ref, xp_ref, mu[:, 3 * RW_W:])

    z = w0_ref[...] + _dot3(jnp.tanh(xs), w2_ref[...])
    nz = -z
    softplus = jnp.maximum(nz, 0.0) + jnp.log(1.0 + jnp.exp(-jnp.abs(nz)))
    w_log = -softplus - 0.5
    a = jax.nn.sigmoid(a0_ref[...] + _dot3(xs, a2_ref[...]))
    g = _dot3(jax.nn.sigmoid(xs), g2_ref[...])

    ones = _group_ones(RW_W, RW_DIM)
    kkr = kr * kk_ref[...]
    ss = _dot_exact_rhs(kkr * kkr, ones)
    kk = kkr / jnp.maximum(jnp.sqrt(ss), 1e-12)
    _put_blocks(ro, rr)
    _put_blocks(lwo, -jnp.exp(w_log))
    _put_blocks(ko, kr * (1.0 + (a - 1.0) * ka_ref[...]))
    _put_blocks(vo, vr)
    _put_blocks(nao, -kk)
    _put_blocks(bo, kk * a)
    _put_blocks(go, g)


def _rwprep(p, cb0, s, mu, w0, a0, k_k, k_a, w2, a2, g2, tm):
    t = p.shape[1]
    lr = mu.shape[0] - 3 * RW_W
    nx = lr // LANES
    w2p = jnp.zeros((lr, RW_W), F32).at[0:w2.shape[0]].set(w2)
    a2p = jnp.zeros((lr, RW_W), F32).at[w2.shape[0]:w2.shape[0] + a2.shape[0]].set(a2)
    g2p = jnp.zeros((lr, RW_W), F32).at[lr - g2.shape[0]:].set(g2)
    r8 = tm // 8
    gb = cb0 // RW_PAIRS
    xb = (cb0 + 3 * RW_PAIRS) // nx

    def cur(c, n):
        return pl.BlockSpec((n, tm, LANES), lambda i: (c, i, 0))

    def prev(c, n):
        return pl.BlockSpec((n, 8, LANES), lambda i: (c, jnp.maximum(i * r8 - 1, 0), 0))

    def full(shape):
        return pl.BlockSpec(shape, lambda i: (0, 0))

    vec = lambda a_: a_.astype(F32).reshape(1, -1)
    out = jax.ShapeDtypeStruct((RW_PAIRS, t, LANES), F32)
    return pl.pallas_call(
        functools.partial(_rwprep_kernel, rows_per_seq=s),
        grid=(t // tm,),
        in_specs=[cur(gb, RW_PAIRS), cur(gb + 1, RW_PAIRS), cur(gb + 2, RW_PAIRS), cur(xb, nx),
                  prev(gb, RW_PAIRS), prev(gb + 1, RW_PAIRS), prev(gb + 2, RW_PAIRS), prev(xb, nx),
                  full((1, 3 * RW_W + lr)), full((1, RW_W)), full((1, RW_W)), full((1, RW_W)),
                  full((1, RW_W)), full((lr, RW_W)), full((lr, RW_W)), full((lr, RW_W))],
        out_specs=[pl.BlockSpec((RW_PAIRS, tm, LANES), lambda i: (0, i, 0))] * 7,
        out_shape=[out] * 7,
        compiler_params=_params("parallel"),
    )(p, p, p, p, p, p, p, p, vec(mu), vec(w0), vec(a0), vec(k_k), vec(k_a), w2p, a2p, g2p)


def _rwkv_kernel(r_ref, lw_ref, k_ref, v_ref, a_ref, b_ref, y_ref, rh_ref, yh_ref, p_ref, q_ref,
                 st_ref, *, nchunk, unroll):
    L = CHUNK
    W = 2 * L
    npair = r_ref.shape[0]

    @pl.when(pl.program_id(1) == 0)
    def _():
        st_ref[...] = jnp.zeros(st_ref.shape, F32)

    lane = lax.broadcasted_iota(jnp.int32, (L, W), 1)
    rowi = lax.broadcasted_iota(jnp.int32, (L, W), 0)
    strict = (lane % L) < rowi
    incl = (lane % L) <= rowi
    rr = lax.broadcasted_iota(jnp.int32, (W, W), 0)
    cc = lax.broadcasted_iota(jnp.int32, (W, W), 1)
    same = (rr // L) == (cc // L)
    eye = rr == cc
    tl = lax.broadcasted_iota(jnp.int32, (L, L), 0)
    sl = lax.broadcasted_iota(jnp.int32, (L, L), 1)
    tril = (sl <= tl).astype(BF16)

    def bd(x):
        return jnp.where(same, jnp.concatenate([x, x], axis=0), 0.0)

    def group(gi, carry):
        us = [gi * unroll + i for i in range(unroll)]
        prs = [u // nchunk for u in us]
        rws = [pl.ds(pl.multiple_of((u % nchunk) * L, L), L) for u in us]
        G = range(unroll)
        ld = lambda ref: [ref[prs[i], rws[i], :] for i in G]
        r, lw, k, v, a, b = ld(r_ref), ld(lw_ref), ld(k_ref), ld(v_ref), ld(a_ref), ld(b_ref)

        def csum(x):
            hi, lo = _split(x)
            rest = x - hi.astype(F32) - lo.astype(F32)
            return jnp.concatenate([hi, lo, rest.astype(BF16)], axis=1)

        c3 = [_dot(tril, csum(lw[i])) for i in G]
        cin = [c[:, 0:W] + c[:, W:2 * W] + c[:, 2 * W:] for c in c3]
        clast = [c[L - 1:L, :] for c in cin]
        g_inv = [jnp.exp(-c) for c in cin]
        g_tail = [jnp.exp(clast[i] - cin[i]) for i in G]
        at = [a[i] * jnp.exp(cin[i] - lw[i]) for i in G]
        rt = [r[i] * jnp.exp(cin[i]) for i in G]
        abk = [_dot_nt(jnp.concatenate([at[i], rt[i]], axis=0),
                       jnp.concatenate([bd(b[i] * g_inv[i]), bd(k[i] * g_inv[i])], axis=0)) for i in G]
        a_ab = [jnp.where(strict, m[0:L, 0:W], 0.0) for m in abk]
        a_rb = [jnp.where(incl, m[L:W, 0:W], 0.0) for m in abk]
        a_ak = [jnp.where(strict, m[0:L, W:], 0.0) for m in abk]
        a_rk = [jnp.where(incl, m[L:W, W:], 0.0) for m in abk]
        n = [bd(m) for m in a_ab]
        tm = [jnp.where(eye, 1.0, m) for m in n]
        x = [_dot(m, m) for m in n]
        for j in range(5):
            if j < 4:
                xx = [_dot(x[i], jnp.concatenate([x[i], tm[i]], axis=1)) for i in G]
                x = [m[:, 0:W] for m in xx]
                tm = [tm[i] + xx[i][:, W:] for i in G]
            else:
                tm = [tm[i] + _dot(x[i], tm[i]) for i in G]
        v_bd = [bd(m) for m in v]
        kv = [_dot(jnp.concatenate([bd(a_ak[i]), a_rk[i]], axis=0), v_bd[i]) for i in G]
        au = [_dot(tm[i], jnp.concatenate([bd(at[i]), kv[i][0:W]], axis=1)) for i in G]
        ry = [_dot(a_rb[i], au[i]) for i in G]
        zero = jnp.zeros((W, W), F32)
        pq = [_dot_tn(jnp.concatenate([bd(b[i] * g_tail[i]), bd(k[i] * g_tail[i])], axis=0),
                      jnp.concatenate([au[i], jnp.concatenate([zero, v_bd[i]], axis=1)], axis=0))
              for i in G]
        for i in G:
            rh_ref[prs[i], rws[i], :] = rt[i] + ry[i][:, 0:W]
            yh_ref[prs[i], rws[i], :] = ry[i][:, W:] + kv[i][W:]
            p_ref[us[i]] = jnp.where(eye, jnp.exp(clast[i]), 0.0) + pq[i][:, 0:W]
            q_ref[us[i]] = pq[i][:, W:]
        return carry

    lax.fori_loop(0, npair * nchunk // unroll, group, 0)

    def step(c, carry):
        rows = pl.ds(pl.multiple_of(c * L, L), L)
        for pr in range(npair):
            st = st_ref[pr].astype(BF16)
            y_ref[pr, rows, :] = _dot(rh_ref[pr, rows, :], st) + yh_ref[pr, rows, :]
            st_ref[pr] = _dot(p_ref[pr * nchunk + c], st) + q_ref[pr * nchunk + c]
        return carry

    lax.fori_loop(0, nchunk, step, 0)


def _rwkv(r, lw, k, v, na, bb, b, s, sb, unroll):
    npair, t, w = r.shape
    nchunk = sb // CHUNK
    nsb = s // sb
    spec = pl.BlockSpec((npair, sb, w), lambda bi, si: (0, bi * nsb + si, 0))
    return pl.pallas_call(
        functools.partial(_rwkv_kernel, nchunk=nchunk, unroll=unroll),
        grid=(b, nsb),
        in_specs=[spec] * 6,
        out_specs=spec,
        out_shape=jax.ShapeDtypeStruct((npair, t, w), F32),
        scratch_shapes=[pltpu.VMEM((npair, sb, w), F32), pltpu.VMEM((npair, sb, w), F32),
                        pltpu.VMEM((npair * nchunk, w, w), F32), pltpu.VMEM((npair * nchunk, w, w), F32),
                        pltpu.VMEM((npair, w, w), F32)],
        compiler_params=_params("parallel", "arbitrary"),
    )(r, lw, k, v, na, bb)


def _rwpost_kernel(y_ref, r_ref, k_ref, v_ref, g_ref, lng_ref, lnb_ref, rk_ref, o_ref):
    ones = _group_ones(RW_W, RW_DIM)
    y = _cat_blocks(y_ref, RW_PAIRS)
    mu = _dot_exact_rhs(y, ones) * (1.0 / RW_DIM)
    yc = y - mu
    var = _dot_exact_rhs(yc * yc, ones) * (1.0 / RW_DIM)
    out = yc * lax.rsqrt(var + RW_LN_EPS) * lng_ref[...] + lnb_ref[...]
    v = _cat_blocks(v_ref, RW_PAIRS)
    bonus = _dot_exact_rhs(_cat_blocks(r_ref, RW_PAIRS) * _cat_blocks(k_ref, RW_PAIRS) * rk_ref[...], ones)
    _put_blocks(o_ref, (out + bonus * v) * _cat_blocks(g_ref, RW_PAIRS))


def _rwpost(y, r, k, v, g, ln_g, ln_b, r_k, tm):
    t = y.shape[1]
    big = pl.BlockSpec((RW_PAIRS, tm, LANES), lambda i: (0, i, 0))
    small = pl.BlockSpec((1, RW_W), lambda i: (0, 0))
    vec = lambda a_: a_.astype(F32).reshape(1, RW_W)
    return pl.pallas_call(
        _rwpost_kernel,
        grid=(t // tm,),
        in_specs=[big] * 5 + [small] * 3,
        out_specs=big,
        out_shape=jax.ShapeDtypeStruct((RW_PAIRS, t, LANES), BF16),
        compiler_params=_params("parallel"),
    )(y, r, k, v, g, vec(ln_g), vec(ln_b), vec(r_k))


def _outproj_kernel(x_ref, oa_ref, oc_ref, ob_ref, w_ref, o_ref):
    mix = jnp.concatenate([_cat_blocks(oa_ref, oa_ref.shape[0]), _cat_blocks(oc_ref, oc_ref.shape[0]),
                           _cat_blocks(ob_ref, ob_ref.shape[0])], axis=1)
    o_ref[...] = x_ref[...] + jnp.dot(mix, w_ref[...], preferred_element_type=F32)


def _outproj(x, oa, oc, ob, w, tm, tn):
    t, d = x.shape
    blocks = lambda a_: pl.BlockSpec((a_.shape[0], tm, LANES), lambda i, j: (0, i, 0))
    return pl.pallas_call(
        _outproj_kernel,
        grid=(t // tm, d // tn),
        in_specs=[pl.BlockSpec((tm, tn), lambda i, j: (i, j)),
                  blocks(oa), blocks(oc), blocks(ob),
                  pl.BlockSpec((w.shape[0], tn), lambda i, j: (0, j))],
        out_specs=pl.BlockSpec((tm, tn), lambda i, j: (i, j)),
        out_shape=jax.ShapeDtypeStruct((t, d), F32),
        compiler_params=_params("parallel", "arbitrary"),
    )(x, oa, oc, ob, w)


FFN_SPLIT = 2


def _swiglu_rows(x_ref, wg_ref, wu_ref, wd_ref):
    n = x_ref.shape[0] // FFN_SPLIT
    sl = [slice(i * n, (i + 1) * n) for i in range(FFN_SPLIT)]
    wg, wu, wd = wg_ref[...], wu_ref[...], wd_ref[...]
    gu = [(jnp.dot(x_ref[r, :], wg, preferred_element_type=F32),
           jnp.dot(x_ref[r, :], wu, preferred_element_type=F32)) for r in sl]
    act = [(g * jax.nn.sigmoid(g) * u).astype(BF16) for g, u in gu]
    return [(r, jnp.dot(a, wd, preferred_element_type=F32)) for r, a in zip(sl, act)]


def _ffn_kernel(x_ref, g_ref, wg_ref, wu_ref, wd_ref, o_ref, xn_ref, acc_ref):
    f = pl.program_id(1)

    @pl.when(f == 0)
    def _():
        xn_ref[...] = _rms(x_ref[...], g_ref[...]).astype(BF16)
        acc_ref[...] = jnp.zeros(acc_ref.shape, F32)

    for r, y in _swiglu_rows(xn_ref, wg_ref, wu_ref, wd_ref):
        acc_ref[r, :] += y

    @pl.when(f == pl.num_programs(1) - 1)
    def _():
        o_ref[...] = x_ref[...] + acc_ref[...]


def _ffn(x, g, wg, wu, wd, tm, tf):
    t, d = x.shape
    ff = wg.shape[1]
    return pl.pallas_call(
        _ffn_kernel,
        grid=(t // tm, ff // tf),
        in_specs=[pl.BlockSpec((tm, d), lambda i, f: (i, 0)),
                  pl.BlockSpec((1, d), lambda i, f: (0, 0)),
                  pl.BlockSpec((d, tf), lambda i, f: (0, f)),
                  pl.BlockSpec((d, tf), lambda i, f: (0, f)),
                  pl.BlockSpec((tf, d), lambda i, f: (f, 0))],
        out_specs=pl.BlockSpec((tm, d), lambda i, f: (i, 0)),
        out_shape=jax.ShapeDtypeStruct((t, d), F32),
        scratch_shapes=[pltpu.VMEM((tm, d), BF16), pltpu.VMEM((tm, d), F32)],
        compiler_params=_params("parallel", "arbitrary"),
    )(x, g.reshape(1, d), wg, wu, wd)


def _router_kernel(x_ref, g_ref, wr_ref, h_ref, comb_ref, combt_ref, rcol_ref, rrow_ref, cnt_ref,
                   *, n_experts):
    h = _rms(x_ref[...], g_ref[...])
    h_ref[...] = h.astype(BF16)
    logits = _dot3(h, wr_ref[...])
    lane = lax.broadcasted_iota(jnp.int32, logits.shape, 1)
    lg = jnp.where(lane < n_experts, logits, NEG)
    m1 = jnp.max(lg, axis=-1, keepdims=True)
    i1 = jnp.min(jnp.where(lg == m1, lane, LANES), axis=-1, keepdims=True)
    lg2 = jnp.where(lane == i1, NEG, lg)
    m2 = jnp.max(lg2, axis=-1, keepdims=True)
    i2 = jnp.min(jnp.where(lg2 == m2, lane, LANES), axis=-1, keepdims=True)
    e2 = jnp.exp(m2 - m1)
    w1 = 1.0 / (1.0 + e2)
    comb = jnp.where(lane == i1, w1, 0.0) + jnp.where(lane == i2, e2 * w1, 0.0)
    combt = comb.T[0:combt_ref.shape[0], :]
    comb_ref[...] = comb
    combt_ref[...] = combt
    ts = comb.shape[0]
    tt = lax.broadcasted_iota(jnp.int32, (ts, ts), 0)
    uu = lax.broadcasted_iota(jnp.int32, (ts, ts), 1)
    live = jnp.where(comb > 0.0, 1.0, 0.0)
    rcol_ref[...] = _dot((uu < tt).astype(BF16), live)
    rrow_ref[...] = _dot(jnp.where(combt > 0.0, 1.0, 0.0), (tt < uu).astype(BF16))
    cnt_ref[0] = jnp.sum(live, axis=0, keepdims=True).astype(jnp.int32)


def _router(x, g, wr, ts):
    t, d = x.shape
    e = wr.shape[1]
    ep = max(8, e)
    wrp = jnp.zeros((d, LANES), F32).at[:, :e].set(wr)
    nt = t // ts
    return pl.pallas_call(
        functools.partial(_router_kernel, n_experts=e),
        grid=(nt,),
        in_specs=[pl.BlockSpec((ts, d), lambda i: (i, 0)),
                  pl.BlockSpec((1, d), lambda i: (0, 0)),
                  pl.BlockSpec((d, LANES), lambda i: (0, 0))],
        out_specs=[pl.BlockSpec((ts, d), lambda i: (i, 0)),
                   pl.BlockSpec((ts, LANES), lambda i: (i, 0)),
                   pl.BlockSpec((ep, ts), lambda i: (0, i)),
                   pl.BlockSpec((ts, LANES), lambda i: (i, 0)),
                   pl.BlockSpec((ep, ts), lambda i: (0, i)),
                   pl.BlockSpec((1, 1, LANES), lambda i: (i, 0, 0))],
        out_shape=[jax.ShapeDtypeStruct((t, d), BF16),
                   jax.ShapeDtypeStruct((t, LANES), F32),
                   jax.ShapeDtypeStruct((ep, t), F32),
                   jax.ShapeDtypeStruct((t, LANES), F32),
                   jax.ShapeDtypeStruct((ep, t), F32),
                   jax.ShapeDtypeStruct((nt, 1, LANES), jnp.int32)],
        compiler_params=_params("parallel"),
    )(x, g.reshape(1, d), wrp)


MOE_BM = 256
MOE_BMF = 512
NO_MATCH = -(1 << 20)


def _moe_plan(counts, nt, ne, t, bm, bmf):
    i32 = jnp.int32
    cnt = counts.reshape(nt, ne).astype(i32)
    tot = jnp.sum(cnt, axis=0)
    ptot = (tot + bmf - 1) // bmf * bmf
    eend = jnp.cumsum(ptot)
    ebase = eend - ptot
    seg0 = ebase[None, :] + jnp.cumsum(cnt, axis=0) - cnt
    seg1 = seg0 + cnt
    nrows = 2 * t + ne * bmf
    nblk = nrows // bmf
    blk_exp = jnp.minimum(jnp.searchsorted(eend, jnp.arange(nblk, dtype=i32) * bmf, side="right"),
                          ne - 1).astype(i32)
    nvalid = (eend[-1] // bmf).astype(i32).reshape(1)
    npairs = nrows // bm + nt * ne
    g = jnp.arange(npairs, dtype=i32)

    def pairs(c0, c1, seg_tile, seg_exp, dummy_blk):
        n = jnp.where(c1 > c0, (c1 - 1) // bm - c0 // bm + 1, 0)
        pend = jnp.cumsum(n)
        k = jnp.minimum(jnp.searchsorted(pend, g, side="right"), c0.shape[0] - 1)
        valid = g < pend[-1]
        blk = c0[k] // bm + (g - (pend[k] - n[k]))
        blk = jnp.where(valid, blk, dummy_blk)
        delta = jnp.where(valid, c0[k] - blk * bm, NO_MATCH)
        return (blk.astype(i32), jnp.where(valid, seg_tile[k], nt - 1).astype(i32),
                seg_exp[k].astype(i32), delta.astype(i32))

    tiles = jnp.arange(nt, dtype=i32)
    exps = jnp.arange(ne, dtype=i32)
    c1g = seg1.at[nt - 1].set(eend)
    g_blk, g_tile, g_exp, g_delta = pairs(seg0.T.reshape(-1), c1g.T.reshape(-1),
                                          jnp.tile(tiles, ne), jnp.repeat(exps, nt), nrows // bm)
    g_first = jnp.concatenate([jnp.ones((1,), i32), (g_blk[1:] != g_blk[:-1]).astype(i32)])
    s_blk, s_tile, s_exp, s_delta = pairs(seg0.reshape(-1), seg1.reshape(-1),
                                          jnp.repeat(tiles, ne), jnp.tile(exps, nt), 0)
    s_first = jnp.concatenate([jnp.ones((1,), i32), (s_tile[1:] != s_tile[:-1]).astype(i32)])
    return dict(nrows=nrows, blk_exp=blk_exp, nvalid=nvalid,
                gather=(g_blk, g_tile, g_exp, g_delta, g_first),
                combine=(s_blk, s_tile, s_exp, s_delta, s_first))


def _moe_gather_kernel(blk_ref, tile_ref, exp_ref, delta_ref, first_ref, h_ref, rrow_ref, combt_ref,
                       o_ref):
    g = pl.program_id(0)
    e = exp_ref[g]
    bm, ts = o_ref.shape[0], h_ref.shape[0]
    pos = rrow_ref[pl.ds(e, 1), :] + delta_ref[g].astype(F32)
    live = combt_ref[pl.ds(e, 1), :] > 0.0
    slot = lax.broadcasted_iota(jnp.int32, (bm, ts), 0).astype(F32)
    onehot = jnp.where((pos == slot) & live, 1.0, 0.0).astype(BF16)
    val = jnp.dot(onehot, h_ref[...], preferred_element_type=F32).astype(BF16)

    @pl.when(first_ref[g] == 1)
    def _():
        o_ref[...] = val

    @pl.when(first_ref[g] == 0)
    def _():
        o_ref[...] += val


def _moe_gather(plan, h, rrow, combt, ts, bm):
    t, d = h.shape
    ep = combt.shape[0]
    blk, tile, exp, delta, first = plan["gather"]
    grid_spec = pltpu.PrefetchScalarGridSpec(
        num_scalar_prefetch=5,
        grid=(blk.shape[0],),
        in_specs=[pl.BlockSpec((ts, d), lambda g, b_, t_, e_, d_, f_: (t_[g], 0)),
                  pl.BlockSpec((ep, ts), lambda g, b_, t_, e_, d_, f_: (0, t_[g])),
                  pl.BlockSpec((ep, ts), lambda g, b_, t_, e_, d_, f_: (0, t_[g]))],
        out_specs=pl.BlockSpec((bm, d), lambda g, b_, t_, e_, d_, f_: (b_[g], 0)),
    )
    return pl.pallas_call(
        _moe_gather_kernel,
        grid_spec=grid_spec,
        out_shape=jax.ShapeDtypeStruct((plan["nrows"] + MOE_BMF, d), BF16),
        compiler_params=_params("arbitrary"),
    )(blk, tile, exp, delta, first, h, rrow, combt)


def _moe_ffn_kernel(bexp_ref, nv_ref, x_ref, wg_ref, wu_ref, wd_ref, o_ref, acc_ref):
    i, f = pl.program_id(0), pl.program_id(1)
    valid = i < nv_ref[0]

    @pl.when(valid)
    def _():
        @pl.when(f == 0)
        def _():
            acc_ref[...] = jnp.zeros(acc_ref.shape, F32)

        for r, y in _swiglu_rows(x_ref, wg_ref, wu_ref, wd_ref):
            acc_ref[r, :] += y

    @pl.when(f == pl.num_programs(1) - 1)
    def _():
        @pl.when(valid)
        def _():
            o_ref[...] = acc_ref[...].astype(BF16)

        @pl.when(jnp.logical_not(valid))
        def _():
            o_ref[...] = jnp.zeros(o_ref.shape, BF16)


def _moe_ffn(plan, xs, wg, wu, wd, bmf, tf):
    d = xs.shape[1]
    ne, _, ff = wg.shape
    nblk = plan["blk_exp"].shape[0]
    grid_spec = pltpu.PrefetchScalarGridSpec(
        num_scalar_prefetch=2,
        grid=(nblk, ff // tf),
        in_specs=[pl.BlockSpec((bmf, d), lambda i, f, be, nv: (i, 0)),
                  pl.BlockSpec((None, d, tf), lambda i, f, be, nv: (be[i], 0, f)),
                  pl.BlockSpec((None, d, tf), lambda i, f, be, nv: (be[i], 0, f)),
                  pl.BlockSpec((None, tf, d), lambda i, f, be, nv: (be[i], f, 0))],
        out_specs=pl.BlockSpec((bmf, d), lambda i, f, be, nv: (i, 0)),
        scratch_shapes=[pltpu.VMEM((bmf, d), F32)],
    )
    return pl.pallas_call(
        _moe_ffn_kernel,
        grid_spec=grid_spec,
        out_shape=jax.ShapeDtypeStruct((nblk * bmf, d), BF16),
        compiler_params=_params("parallel", "arbitrary"),
    )(plan["blk_exp"], plan["nvalid"], xs, wg, wu, wd)


def _moe_combine_kernel(blk_ref, tile_ref, exp_ref, delta_ref, first_ref, y_ref, x_ref, comb_ref,
                        rcol_ref, o_ref):
    g = pl.program_id(0)
    e = exp_ref[g]
    ts, bm = x_ref.shape[0], y_ref.shape[0]

    @pl.when(first_ref[g] == 1)
    def _():
        o_ref[...] = x_ref[...]

    lane = lax.broadcasted_iota(jnp.int32, comb_ref.shape, 1)
    sel = lane == e
    cw = jnp.sum(jnp.where(sel, comb_ref[...], 0.0), axis=-1, keepdims=True)
    pos = (jnp.sum(jnp.where(sel, rcol_ref[...], 0.0), axis=-1, keepdims=True)
           + delta_ref[g].astype(F32))
    slot = lax.broadcasted_iota(jnp.int32, (ts, bm), 1).astype(F32)
    onehot = jnp.where((pos == slot) & (cw > 0.0), 1.0, 0.0).astype(BF16)
    o_ref[...] += cw * jnp.dot(onehot, y_ref[...], preferred_element_type=F32)


def _moe_combine(plan, ys, x, comb, rcol, ts, bm):
    t, d = x.shape
    blk, tile, exp, delta, first = plan["combine"]
    grid_spec = pltpu.PrefetchScalarGridSpec(
        num_scalar_prefetch=5,
        grid=(blk.shape[0],),
        in_specs=[pl.BlockSpec((bm, d), lambda g, b_, t_, e_, d_, f_: (b_[g], 0)),
                  pl.BlockSpec((ts, d), lambda g, b_, t_, e_, d_, f_: (t_[g], 0)),
                  pl.BlockSpec((ts, LANES), lambda g, b_, t_, e_, d_, f_: (t_[g], 0)),
                  pl.BlockSpec((ts, LANES), lambda g, b_, t_, e_, d_, f_: (t_[g], 0))],
        out_specs=pl.BlockSpec((ts, d), lambda g, b_, t_, e_, d_, f_: (t_[g], 0)),
    )
    return pl.pallas_call(
        _moe_combine_kernel,
        grid_spec=grid_spec,
        out_shape=jax.ShapeDtypeStruct((t, d), F32),
        compiler_params=_params("arbitrary"),
    )(blk, tile, exp, delta, first, ys, x, comb, rcol)


def _moe(x, h, comb, combt, rcol, rrow, counts, wg, wu, wd, ts, tf, bm=MOE_BM, bmf=MOE_BMF):
    t = x.shape[0]
    plan = _moe_plan(counts, t // ts, wg.shape[0], t, bm, bmf)
    xs = _moe_gather(plan, h, rrow, combt, ts, bm)
    ys = _moe_ffn(plan, xs, wg, wu, wd, bmf, tf)
    return _moe_combine(plan, ys, x, comb, rcol, ts, bm)


def kernel(x, norm1_g, w_in, da_q_norm, da_k_norm, da_lambda, da_out_norm, dl_q_norm, dl_k_norm, rw_mu, rw_w0, rw_w2, rw_a0, rw_a2, rw_g2, rw_k_k, rw_k_a, rw_r_k, rw_ln_g, rw_ln_b, w_out, norm2_g, ffn_w_gate, ffn_w_up, ffn_w_down, moe_router, moe_w_gate, moe_w_up, moe_w_down):
    b, s, d = x.shape
    depth = w_in.shape[0]
    t = b * s
    n_experts = moe_router.shape[-1]
    xt = x.reshape(t, d)

    qa, ka, va = 0, DA_HEADS, 2 * DA_HEADS
    qb, kb, vb = 3 * DA_HEADS, 3 * DA_HEADS + DL_HEADS, 3 * DA_HEADS + 2 * DL_HEADS
    rw = 3 * DA_HEADS + 3 * DL_HEADS

    tm = min(1024, t)
    tq = min(256, s)

    for l in range(depth):
        p = _inproj(xt, norm1_g[l], w_in[l].astype(BF16), tm, 1280)

        lam_init = 0.8 - 0.6 * math.exp(-0.3 * l)
        oa = _diffattn(p, qa, ka, va, da_q_norm[l], da_k_norm[l], da_lambda[l], da_out_norm[l],
                       b, s, lam_init, tq)
        oc = _dilattn(p, qb, kb, vb, dl_q_norm[l], dl_k_norm[l], b, s, tq)

        r, lw, k2, v, na, bb, g = _rwprep(p, rw, s, rw_mu[l], rw_w0[l], rw_a0[l], rw_k_k[l],
                                          rw_k_a[l], rw_w2[l], rw_a2[l], rw_g2[l], min(256, s))
        y = _rwkv(r, lw, k2, v, na, bb, b, s, min(512, s), 12)
        ob = _rwpost(y, r, k2, v, g, rw_ln_g[l], rw_ln_b[l], rw_r_k[l].reshape(-1), min(512, s))

        xt = _outproj(xt, oa, oc, ob, w_out[l].astype(BF16), tm, 1024)

        i = l // 2
        if l % 2 == 0:
            xt = _ffn(xt, norm2_g[l], ffn_w_gate[i].astype(BF16), ffn_w_up[i].astype(BF16),
                      ffn_w_down[i].astype(BF16), min(512, t), 512)
        else:
            ts = min(1024, t)
            h, comb, combt, rcol, rrow, cnt = _router(xt, norm2_g[l], moe_router[i], ts)
            counts = cnt[:, 0, :n_experts].reshape(-1)
            xt = _moe(xt, h, comb, combt, rcol, rrow, counts, moe_w_gate[i].astype(BF16),
                      moe_w_up[i].astype(BF16), moe_w_down[i].astype(BF16), ts, 512)
    return xt.reshape(b, s, d)
```

```python
import functools
import math

import numpy as np
import jax
import jax.numpy as jnp
from jax import lax
from jax.experimental import pallas as pl
from jax.experimental.pallas import tpu as pltpu

F32 = jnp.float32
BF16 = jnp.bfloat16

LANES = 128
VMEM_LIMIT = 56 * 1024 * 1024

NEG = -1e30
ROPE_THETA = 10000.0
NORM_EPS = 1e-6
RW_LN_EPS = 64e-5
DL_PATTERNS = ((128, 1), (512, 4), (2048, 16))
TOP_K = 2

DA_HEADS, DA_QK = 4, 64
DL_HEADS = 6
RW_HEADS, RW_DIM = 12, 64
DA_W, DL_W, RW_W = 512, 768, 768
RW_PAIRS = RW_W // LANES
CHUNK = 64


def _params(*sem):
    return pltpu.CompilerParams(dimension_semantics=sem, vmem_limit_bytes=VMEM_LIMIT)


def _dot(a, b):
    return jnp.dot(a.astype(BF16), b.astype(BF16), preferred_element_type=F32)


def _dot_nt(a, b):
    return lax.dot_general(a.astype(BF16), b.astype(BF16), (((1,), (1,)), ((), ())),
                           preferred_element_type=F32)


def _dot_tn(a, b):
    return lax.dot_general(a.astype(BF16), b.astype(BF16), (((0,), (0,)), ((), ())),
                           preferred_element_type=F32)


def _split(x):
    hi = x.astype(BF16)
    lo = (x - hi.astype(F32)).astype(BF16)
    return hi, lo


def _dot3(a, b):
    ah, al = _split(a)
    bh, bl = _split(b)
    return _dot(ah, bh) + _dot(ah, bl) + _dot(al, bh)


def _dot_exact_rhs(a, b_bf16):
    ah, al = _split(a)
    return _dot(ah, b_bf16) + _dot(al, b_bf16)


def _group_ones(width, group):
    i = lax.broadcasted_iota(jnp.int32, (width, width), 0) // group
    j = lax.broadcasted_iota(jnp.int32, (width, width), 1) // group
    return (i == j).astype(BF16)


def _rms(x, g):
    return x * lax.rsqrt(jnp.mean(x * x, axis=-1, keepdims=True) + NORM_EPS) * g


def _cat_blocks(ref, n):
    return jnp.concatenate([ref[c] for c in range(n)], axis=1)


def _put_blocks(ref, val):
    for c in range(ref.shape[0]):
        ref[c] = val[:, c * LANES:(c + 1) * LANES].astype(ref.dtype)


def _inproj_kernel(x_ref, g_ref, w_ref, o_ref, xn_ref):
    @pl.when(pl.program_id(1) == 0)
    def _():
        xn_ref[...] = _rms(x_ref[...], g_ref[...]).astype(BF16)

    _put_blocks(o_ref, jnp.dot(xn_ref[...], w_ref[...], preferred_element_type=F32))


def _inproj(x, g, w, li, tm, tn):
    t, d = x.shape
    n = w.shape[2]
    return pl.pallas_call(
        _inproj_kernel,
        grid=(t // tm, n // tn),
        in_specs=[pl.BlockSpec((tm, d), lambda i, j: (i, 0)),
                  pl.BlockSpec((1, d), lambda i, j: (0, 0)),
                  pl.BlockSpec((None, d, tn), lambda i, j: (li, 0, j))],
        out_specs=pl.BlockSpec((tn // LANES, tm, LANES), lambda i, j: (j, i, 0)),
        out_shape=jax.ShapeDtypeStruct((n // LANES, t, LANES), F32),
        scratch_shapes=[pltpu.VMEM((tm, d), BF16)],
        compiler_params=_params("parallel", "arbitrary"),
    )(x, g.reshape(1, d), w)


def _prep_qk(x, gain, cos, sin, group):
    ones = _group_ones(LANES, group)
    ms = _dot_exact_rhs(x * x, ones) * (1.0 / group)
    y = x * lax.rsqrt(ms + NORM_EPS) * gain
    half = group // 2
    if group == LANES:
        partner = pltpu.roll(y, half, axis=1)
    else:
        lane = lax.broadcasted_iota(jnp.int32, y.shape, 1)
        fwd = pltpu.roll(y, LANES - half, axis=1)
        bwd = pltpu.roll(y, half, axis=1)
        partner = jnp.where((lane % group) < half, fwd, bwd)
    return (y * cos + partner * sin).astype(BF16)


def _rope_tables(s, group):
    half = group // 2
    lane = np.arange(LANES)
    inv = ROPE_THETA ** (-jnp.asarray(lane % half, F32) / half)
    ang = jnp.arange(s, dtype=F32)[:, None] * inv[None, :]
    sign = jnp.asarray(np.where((lane % group) < half, -1.0, 1.0), F32)
    return jnp.cos(ang), jnp.sin(ang) * sign[None, :]


def _tile_gain(gain, group):
    return jnp.tile(gain.astype(F32), LANES // group).reshape(1, LANES)


def _block_pairs(nq):
    return [(j, qi) for j in range(nq) for qi in range(j, nq)]


def _vt_blocks(v_ref, tq, nq):
    return [v_ref[j * tq:(j + 1) * tq, :].T.astype(BF16) for j in range(nq)]


def _diffattn_kernel(q_ref, k_ref, v_ref, cos_ref, sin_ref, gq_ref, gk_ref, lam_ref, go_ref, o_ref,
                     s0_ref, s1_ref, *, tq, scale, lam_init):
    nq = q_ref.shape[0] // tq
    cos, sin = cos_ref[...], sin_ref[...]
    kp = _prep_qk(k_ref[...], gk_ref[...], cos, sin, DA_QK)
    qp = _prep_qk(q_ref[...], gq_ref[...], cos, sin, DA_QK)
    lane = lax.broadcasted_iota(jnp.int32, qp.shape, 1)
    zero = jnp.zeros_like(qp)
    qs = (jnp.where(lane < DA_QK, qp, zero), jnp.where(lane >= DA_QK, qp, zero))
    vt = _vt_blocks(v_ref, tq, nq)
    blk = lambda x, i: x[i * tq:(i + 1) * tq, :]
    causal = (lax.broadcasted_iota(jnp.int32, (tq, tq), 1)
              >= lax.broadcasted_iota(jnp.int32, (tq, tq), 0))
    pairs = _block_pairs(nq)
    s_refs = (s0_ref, s1_ref)

    m = [[jnp.full((1, tq), NEG, F32) for _ in range(nq)] for _ in range(2)]
    for idx, (j, qi) in enumerate(pairs):
        for c in range(2):
            sc = _dot_nt(blk(kp, j), blk(qs[c], qi)) * scale
            if j == qi:
                sc = jnp.where(causal, sc, NEG)
            s_refs[c][idx] = sc
            m[c][qi] = jnp.maximum(m[c][qi], jnp.max(sc, axis=0, keepdims=True))

    l = [[jnp.zeros((1, tq), F32) for _ in range(nq)] for _ in range(2)]
    acc = [[None] * nq for _ in range(2)]
    for idx, (j, qi) in enumerate(pairs):
        for c in range(2):
            pr = jnp.exp(s_refs[c][idx] - m[c][qi])
            l[c][qi] = l[c][qi] + jnp.sum(pr, axis=0, keepdims=True)
            d = _dot(vt[j], pr)
            acc[c][qi] = d if acc[c][qi] is None else acc[c][qi] + d

    lm = lam_ref[...]
    lam = (jnp.exp(jnp.sum(lm[0:1] * lm[1:2], axis=-1, keepdims=True))
           - jnp.exp(jnp.sum(lm[2:3] * lm[3:4], axis=-1, keepdims=True)) + lam_init)
    for qi in range(nq):
        ot = acc[0][qi] / l[0][qi] - lam * (acc[1][qi] / l[1][qi])
        ot = ot * lax.rsqrt(jnp.mean(ot * ot, axis=0, keepdims=True) + NORM_EPS)
        o_ref[qi * tq:(qi + 1) * tq, :] = (ot.T * go_ref[...] * (1.0 - lam_init)).astype(BF16)


def _diffattn(p, qcb, kcb, vcb, gq, gk, lam4, gout, b, s, lam_init, tq):
    t = p.shape[1]
    nq = s // tq
    npair = nq * (nq + 1) // 2
    cos, sin = _rope_tables(s, DA_QK)
    blk = lambda cb0: pl.BlockSpec((None, s, LANES), lambda bi, h: (cb0 + h, bi, 0))
    full = lambda shape: pl.BlockSpec(shape, lambda bi, h: (0, 0))
    return pl.pallas_call(
        functools.partial(_diffattn_kernel, tq=tq, scale=DA_QK ** -0.5, lam_init=lam_init),
        grid=(b, DA_HEADS),
        in_specs=[blk(qcb), blk(kcb), blk(vcb),
                  full((s, LANES)), full((s, LANES)), full((1, LANES)), full((1, LANES)),
                  full((4, DA_QK)), full((1, LANES))],
        out_specs=blk(0),
        out_shape=jax.ShapeDtypeStruct((DA_HEADS, t, LANES), BF16),
        scratch_shapes=[pltpu.VMEM((npair, tq, tq), F32), pltpu.VMEM((npair, tq, tq), F32)],
        compiler_params=_params("parallel", "parallel"),
    )(p, p, p, cos, sin, _tile_gain(gq, DA_QK), _tile_gain(gk, DA_QK), lam4.astype(F32),
      gout.astype(F32).reshape(1, LANES))


def _dilated_bias(s, tq):
    nd = s // tq
    d = (np.arange(nd)[:, None, None] * tq + np.arange(tq)[None, None, :]
         - np.arange(tq)[None, :, None])
    cnt = np.zeros(d.shape, np.float64)
    for window, dil in DL_PATTERNS:
        cnt += (d >= 0) & (d % dil == 0) & (d <= window)
    bias = np.where(cnt > 0, np.log(np.maximum(cnt, 1.0)), NEG)
    return jnp.asarray(bias, F32)


def _dilattn_kernel(q_ref, k_ref, v_ref, cos_ref, sin_ref, gq_ref, gk_ref, bias_ref, o_ref, s_ref,
                    *, tq, scale):
    nq = q_ref.shape[0] // tq
    cos, sin = cos_ref[...], sin_ref[...]
    kp = _prep_qk(k_ref[...], gk_ref[...], cos, sin, LANES)
    qp = _prep_qk(q_ref[...], gq_ref[...], cos, sin, LANES)
    vt = _vt_blocks(v_ref, tq, nq)
    blk = lambda x, i: x[i * tq:(i + 1) * tq, :]
    pairs = _block_pairs(nq)

    m = [jnp.full((1, tq), NEG, F32) for _ in range(nq)]
    for idx, (j, qi) in enumerate(pairs):
        sc = _dot_nt(blk(kp, j), blk(qp, qi)) * scale + bias_ref[qi - j]
        s_ref[idx] = sc
        m[qi] = jnp.maximum(m[qi], jnp.max(sc, axis=0, keepdims=True))

    l = [jnp.zeros((1, tq), F32) for _ in range(nq)]
    acc = [None] * nq
    for idx, (j, qi) in enumerate(pairs):
        pr = jnp.exp(s_ref[idx] - m[qi])
        l[qi] = l[qi] + jnp.sum(pr, axis=0, keepdims=True)
        d = _dot(vt[j], pr)
        acc[qi] = d if acc[qi] is None else acc[qi] + d

    for qi in range(nq):
        o_ref[qi * tq:(qi + 1) * tq, :] = (acc[qi] / l[qi]).T.astype(BF16)


def _dilattn(p, qcb, kcb, vcb, gq, gk, b, s, tq):
    t = p.shape[1]
    nq = s // tq
    cos, sin = _rope_tables(s, LANES)
    bias = _dilated_bias(s, tq)
    blk = lambda cb0: pl.BlockSpec((None, s, LANES), lambda bi, h: (cb0 + h, bi, 0))
    full = lambda shape: pl.BlockSpec(shape, lambda bi, h: (0,) * len(shape))
    return pl.pallas_call(
        functools.partial(_dilattn_kernel, tq=tq, scale=LANES ** -0.5),
        grid=(b, DL_HEADS),
        in_specs=[blk(qcb), blk(kcb), blk(vcb),
                  full((s, LANES)), full((s, LANES)), full((1, LANES)), full((1, LANES)),
                  full((nq, tq, tq))],
        out_specs=blk(0),
        out_shape=jax.ShapeDtypeStruct((DL_HEADS, t, LANES), BF16),
        scratch_shapes=[pltpu.VMEM((nq * (nq + 1) // 2, tq, tq), F32)],
        compiler_params=_params("parallel", "parallel"),
    )(p, p, p, cos, sin, _tile_gain(gq, LANES), _tile_gain(gk, LANES), bias)


def _rwprep_kernel(r_ref, k_ref, v_ref, x_ref, rp_ref, kp_ref, vp_ref, xp_ref,
                   mu_ref, w0_ref, a0_ref, kk_ref, ka_ref, w2_ref, a2_ref, g2_ref,
                   ro, lwo, ko, vo, nao, bo, go, *, rows_per_seq):
    i = pl.program_id(0)
    tm = r_ref.shape[1]
    first = (i * tm) % rows_per_seq == 0
    row = lax.broadcasted_iota(jnp.int32, (tm, 1), 0)

    def shifted(cur_ref, prev_ref, mu):
        n = cur_ref.shape[0]
        cur = _cat_blocks(cur_ref, n)
        last = jnp.concatenate([prev_ref[c, 7:8, :] for c in range(n)], axis=1)
        last = jnp.where(first, 0.0, last)
        prev = jnp.where(row == 0, last, pltpu.roll(cur, 1, axis=0))
        return cur + mu * (prev - cur)

    mu = mu_ref[...]
    rr = shifted(r_ref, rp_ref, mu[:, 0:RW_W])
    kr = shifted(k_ref, kp_ref, mu[:, RW_W:2 * RW_W])
    vr = shifted(v_ref, vp_ref, mu[:, 2 * RW_W:3 * RW_W])
    xs = shifted(x_ref, xp_ref, mu[:, 3 * RW_W:])

    z = w0_ref[...] + _dot3(jnp.tanh(xs), w2_ref[...])
    nz = -z
    softplus = jnp.maximum(nz, 0.0) + jnp.log(1.0 + jnp.exp(-jnp.abs(nz)))
    w_log = -softplus - 0.5
    a = jax.nn.sigmoid(a0_ref[...] + _dot3(xs, a2_ref[...]))
    g = _dot3(jax.nn.sigmoid(xs), g2_ref[...])

    ones = _group_ones(RW_W, RW_DIM)
    kkr = kr * kk_ref[...]
    ss = _dot_exact_rhs(kkr * kkr, ones)
    kk = kkr / jnp.maximum(jnp.sqrt(ss), 1e-12)
    _put_blocks(ro, rr)
    _put_blocks(lwo, -jnp.exp(w_log))
    _put_blocks(ko, kr * (1.0 + (a - 1.0) * ka_ref[...]))
    _put_blocks(vo, vr)
    _put_blocks(nao, -kk)
    _put_blocks(bo, kk * a)
    _put_blocks(go, g)


def _rwprep(p, cb0, s, mu, w0, a0, k_k, k_a, w2, a2, g2, tm):
    t = p.shape[1]
    lr = mu.shape[0] - 3 * RW_W
    nx = lr // LANES
    w2p = jnp.zeros((lr, RW_W), F32).at[0:w2.shape[0]].set(w2)
    a2p = jnp.zeros((lr, RW_W), F32).at[w2.shape[0]:w2.shape[0] + a2.shape[0]].set(a2)
    g2p = jnp.zeros((lr, RW_W), F32).at[lr - g2.shape[0]:].set(g2)
    r8 = tm // 8
    gb = cb0 // RW_PAIRS
    xb = (cb0 + 3 * RW_PAIRS) // nx

    def cur(c, n):
        return pl.BlockSpec((n, tm, LANES), lambda i: (c, i, 0))

    def prev(c, n):
        return pl.BlockSpec((n, 8, LANES), lambda i: (c, jnp.maximum(i * r8 - 1, 0), 0))

    def full(shape):
        return pl.BlockSpec(shape, lambda i: (0, 0))

    vec = lambda a_: a_.astype(F32).reshape(1, -1)
    out = jax.ShapeDtypeStruct((RW_PAIRS, t, LANES), F32)
    return pl.pallas_call(
        functools.partial(_rwprep_kernel, rows_per_seq=s),
        grid=(t // tm,),
        in_specs=[cur(gb, RW_PAIRS), cur(gb + 1, RW_PAIRS), cur(gb + 2, RW_PAIRS), cur(xb, nx),
                  prev(gb, RW_PAIRS), prev(gb + 1, RW_PAIRS), prev(gb + 2, RW_PAIRS), prev(xb, nx),
                  full((1, 3 * RW_W + lr)), full((1, RW_W)), full((1, RW_W)), full((1, RW_W)),
                  full((1, RW_W)), full((lr, RW_W)), full((lr, RW_W)), full((lr, RW_W))],
        out_specs=[pl.BlockSpec((RW_PAIRS, tm, LANES), lambda i: (0, i, 0))] * 7,
        out_shape=[out] * 7,
        compiler_params=_params("parallel"),
    )(p, p, p, p, p, p, p, p, vec(mu), vec(w0), vec(a0), vec(k_k), vec(k_a), w2p, a2p, g2p)


def _rwkv_kernel(r_ref, lw_ref, k_ref, v_ref, a_ref, b_ref, y_ref, rh_ref, yh_ref, p_ref, q_ref,
                 st_ref, *, nchunk, unroll):
    L = CHUNK
    W = 2 * L
    npair = r_ref.shape[0]

    @pl.when(pl.program_id(1) == 0)
    def _():
        st_ref[...] = jnp.zeros(st_ref.shape, F32)

    lane = lax.broadcasted_iota(jnp.int32, (L, W), 1)
    rowi = lax.broadcasted_iota(jnp.int32, (L, W), 0)
    strict = (lane % L) < rowi
    incl = (lane % L) <= rowi
    rr = lax.broadcasted_iota(jnp.int32, (W, W), 0)
    cc = lax.broadcasted_iota(jnp.int32, (W, W), 1)
    same = (rr // L) == (cc // L)
    eye = rr == cc
    tl = lax.broadcasted_iota(jnp.int32, (L, L), 0)
    sl = lax.broadcasted_iota(jnp.int32, (L, L), 1)
    tril = (sl <= tl).astype(BF16)

    def bd(x):
        return jnp.where(same, jnp.concatenate([x, x], axis=0), 0.0)

    def group(gi, carry):
        us = [gi * unroll + i for i in range(unroll)]
        prs = [u // nchunk for u in us]
        rws = [pl.ds(pl.multiple_of((u % nchunk) * L, L), L) for u in us]
        G = range(unroll)
        ld = lambda ref: [ref[prs[i], rws[i], :] for i in G]
        r, lw, k, v, a, b = ld(r_ref), ld(lw_ref), ld(k_ref), ld(v_ref), ld(a_ref), ld(b_ref)

        def csum(x):
            hi, lo = _split(x)
            rest = x - hi.astype(F32) - lo.astype(F32)
            return jnp.concatenate([hi, lo, rest.astype(BF16)], axis=1)

        c3 = [_dot(tril, csum(lw[i])) for i in G]
        cin = [c[:, 0:W] + c[:, W:2 * W] + c[:, 2 * W:] for c in c3]
        clast = [c[L - 1:L, :] for c in cin]
        g_inv = [jnp.exp(-c) for c in cin]
        g_tail = [jnp.exp(clast[i] - cin[i]) for i in G]
        at = [a[i] * jnp.exp(cin[i] - lw[i]) for i in G]
        rt = [r[i] * jnp.exp(cin[i]) for i in G]
        abk = [_dot_nt(jnp.concatenate([at[i], rt[i]], axis=0),
                       jnp.concatenate([bd(b[i] * g_inv[i]), bd(k[i] * g_inv[i])], axis=0)) for i in G]
        a_ab = [jnp.where(strict, m[0:L, 0:W], 0.0) for m in abk]
        a_rb = [jnp.where(incl, m[L:W, 0:W], 0.0) for m in abk]
        a_ak = [jnp.where(strict, m[0:L, W:], 0.0) for m in abk]
        a_rk = [jnp.where(incl, m[L:W, W:], 0.0) for m in abk]
        n = [bd(m) for m in a_ab]
        tm = [jnp.where(eye, 1.0, m) for m in n]
        x = [_dot(m, m) for m in n]
        for j in range(5):
            if j < 4:
                xx = [_dot(x[i], jnp.concatenate([x[i], tm[i]], axis=1)) for i in G]
                x = [m[:, 0:W] for m in xx]
                tm = [tm[i] + xx[i][:, W:] for i in G]
            else:
                tm = [tm[i] + _dot(x[i], tm[i]) for i in G]
        v_bd = [bd(m) for m in v]
        kv = [_dot(jnp.concatenate([bd(a_ak[i]), a_rk[i]], axis=0), v_bd[i]) for i in G]
        au = [_dot(tm[i], jnp.concatenate([bd(at[i]), kv[i][0:W]], axis=1)) for i in G]
        ry = [_dot(a_rb[i], au[i]) for i in G]
        zero = jnp.zeros((W, W), F32)
        pq = [_dot_tn(jnp.concatenate([bd(b[i] * g_tail[i]), bd(k[i] * g_tail[i])], axis=0),
                      jnp.concatenate([au[i], jnp.concatenate([zero, v_bd[i]], axis=1)], axis=0))
              for i in G]
        for i in G:
            rh_ref[prs[i], rws[i], :] = rt[i] + ry[i][:, 0:W]
            yh_ref[prs[i], rws[i], :] = ry[i][:, W:] + kv[i][W:]
            p_ref[us[i]] = jnp.where(eye, jnp.exp(clast[i]), 0.0) + pq[i][:, 0:W]
            q_ref[us[i]] = pq[i][:, W:]
        return carry

    lax.fori_loop(0, npair * nchunk // unroll, group, 0)

    def step(c, carry):
        rows = pl.ds(pl.multiple_of(c * L, L), L)
        for pr in range(npair):
            st = st_ref[pr].astype(BF16)
            y_ref[pr, rows, :] = _dot(rh_ref[pr, rows, :], st) + yh_ref[pr, rows, :]
            st_ref[pr] = _dot(p_ref[pr * nchunk + c], st) + q_ref[pr * nchunk + c]
        return carry

    lax.fori_loop(0, nchunk, step, 0)


def _rwkv(r, lw, k, v, na, bb, b, s, sb, unroll):
    npair, t, w = r.shape
    nchunk = sb // CHUNK
    nsb = s // sb
    spec = pl.BlockSpec((npair, sb, w), lambda bi, si: (0, bi * nsb + si, 0))
    return pl.pallas_call(
        functools.partial(_rwkv_kernel, nchunk=nchunk, unroll=unroll),
        grid=(b, nsb),
        in_specs=[spec] * 6,
        out_specs=spec,
        out_shape=jax.ShapeDtypeStruct((npair, t, w), F32),
        scratch_shapes=[pltpu.VMEM((npair, sb, w), F32), pltpu.VMEM((npair, sb, w), F32),
                        pltpu.VMEM((npair * nchunk, w, w), F32), pltpu.VMEM((npair * nchunk, w, w), F32),
                        pltpu.VMEM((npair, w, w), F32)],
        compiler_params=_params("parallel", "arbitrary"),
    )(r, lw, k, v, na, bb)


def _rwpost_kernel(y_ref, r_ref, k_ref, v_ref, g_ref, lng_ref, lnb_ref, rk_ref, o_ref):
    ones = _group_ones(RW_W, RW_DIM)
    y = _cat_blocks(y_ref, RW_PAIRS)
    mu = _dot_exact_rhs(y, ones) * (1.0 / RW_DIM)
    yc = y - mu
    var = _dot_exact_rhs(yc * yc, ones) * (1.0 / RW_DIM)
    out = yc * lax.rsqrt(var + RW_LN_EPS) * lng_ref[...] + lnb_ref[...]
    v = _cat_blocks(v_ref, RW_PAIRS)
    bonus = _dot_exact_rhs(_cat_blocks(r_ref, RW_PAIRS) * _cat_blocks(k_ref, RW_PAIRS) * rk_ref[...], ones)
    _put_blocks(o_ref, (out + bonus * v) * _cat_blocks(g_ref, RW_PAIRS))


def _rwpost(y, r, k, v, g, ln_g, ln_b, r_k, tm):
    t = y.shape[1]
    big = pl.BlockSpec((RW_PAIRS, tm, LANES), lambda i: (0, i, 0))
    small = pl.BlockSpec((1, RW_W), lambda i: (0, 0))
    vec = lambda a_: a_.astype(F32).reshape(1, RW_W)
    return pl.pallas_call(
        _rwpost_kernel,
        grid=(t // tm,),
        in_specs=[big] * 5 + [small] * 3,
        out_specs=big,
        out_shape=jax.ShapeDtypeStruct((RW_PAIRS, t, LANES), BF16),
        compiler_params=_params("parallel"),
    )(y, r, k, v, g, vec(ln_g), vec(ln_b), vec(r_k))


def _outproj_kernel(x_ref, oa_ref, oc_ref, ob_ref, w_ref, o_ref):
    mix = jnp.concatenate([_cat_blocks(oa_ref, oa_ref.shape[0]), _cat_blocks(oc_ref, oc_ref.shape[0]),
                           _cat_blocks(ob_ref, ob_ref.shape[0])], axis=1)
    o_ref[...] = x_ref[...] + jnp.dot(mix, w_ref[...], preferred_element_type=F32)


def _outproj(x, oa, oc, ob, w, li, tm, tn):
    t, d = x.shape
    blocks = lambda a_: pl.BlockSpec((a_.shape[0], tm, LANES), lambda i, j: (0, i, 0))
    return pl.pallas_call(
        _outproj_kernel,
        grid=(t // tm, d // tn),
        in_specs=[pl.BlockSpec((tm, tn), lambda i, j: (i, j)),
                  blocks(oa), blocks(oc), blocks(ob),
                  pl.BlockSpec((None, w.shape[1], tn), lambda i, j: (li, 0, j))],
        out_specs=pl.BlockSpec((tm, tn), lambda i, j: (i, j)),
        out_shape=jax.ShapeDtypeStruct((t, d), F32),
        compiler_params=_params("parallel", "arbitrary"),
    )(x, oa, oc, ob, w)


FFN_SPLIT = 2


def _swiglu_rows(x_ref, wg_ref, wu_ref, wd_ref):
    n = x_ref.shape[0] // FFN_SPLIT
    sl = [slice(i * n, (i + 1) * n) for i in range(FFN_SPLIT)]
    wg, wu, wd = wg_ref[...], wu_ref[...], wd_ref[...]
    gu = [(jnp.dot(x_ref[r, :], wg, preferred_element_type=F32),
           jnp.dot(x_ref[r, :], wu, preferred_element_type=F32)) for r in sl]
    act = [(g * jax.nn.sigmoid(g) * u).astype(BF16) for g, u in gu]
    return [(r, jnp.dot(a, wd, preferred_element_type=F32)) for r, a in zip(sl, act)]


def _ffn_kernel(x_ref, g_ref, wg_ref, wu_ref, wd_ref, o_ref, xn_ref, acc_ref):
    f = pl.program_id(1)

    @pl.when(f == 0)
    def _():
        xn_ref[...] = _rms(x_ref[...], g_ref[...]).astype(BF16)
        acc_ref[...] = jnp.zeros(acc_ref.shape, F32)

    for r, y in _swiglu_rows(xn_ref, wg_ref, wu_ref, wd_ref):
        acc_ref[r, :] += y

    @pl.when(f == pl.num_programs(1) - 1)
    def _():
        o_ref[...] = x_ref[...] + acc_ref[...]


def _ffn(x, g, wg, wu, wd, li, tm, tf):
    t, d = x.shape
    ff = wg.shape[2]
    one = pl.Buffered(1)
    return pl.pallas_call(
        _ffn_kernel,
        grid=(t // tm, ff // tf),
        in_specs=[pl.BlockSpec((tm, d), lambda i, f: (i, 0), pipeline_mode=one),
                  pl.BlockSpec((1, d), lambda i, f: (0, 0)),
                  pl.BlockSpec((None, d, tf), lambda i, f: (li, 0, f)),
                  pl.BlockSpec((None, d, tf), lambda i, f: (li, 0, f)),
                  pl.BlockSpec((None, tf, d), lambda i, f: (li, f, 0))],
        out_specs=pl.BlockSpec((tm, d), lambda i, f: (i, 0), pipeline_mode=one),
        out_shape=jax.ShapeDtypeStruct((t, d), F32),
        scratch_shapes=[pltpu.VMEM((tm, d), BF16), pltpu.VMEM((tm, d), F32)],
        compiler_params=_params("parallel", "arbitrary"),
    )(x, g.reshape(1, d), wg, wu, wd)


def _router_kernel(x_ref, g_ref, wr_ref, h_ref, comb_ref, combt_ref, rcol_ref, rrow_ref, cnt_ref,
                   *, n_experts):
    h = _rms(x_ref[...], g_ref[...])
    h_ref[...] = h.astype(BF16)
    logits = _dot3(h, wr_ref[...])
    lane = lax.broadcasted_iota(jnp.int32, logits.shape, 1)
    lg = jnp.where(lane < n_experts, logits, NEG)
    m1 = jnp.max(lg, axis=-1, keepdims=True)
    i1 = jnp.min(jnp.where(lg == m1, lane, LANES), axis=-1, keepdims=True)
    lg2 = jnp.where(lane == i1, NEG, lg)
    m2 = jnp.max(lg2, axis=-1, keepdims=True)
    i2 = jnp.min(jnp.where(lg2 == m2, lane, LANES), axis=-1, keepdims=True)
    e2 = jnp.exp(m2 - m1)
    w1 = 1.0 / (1.0 + e2)
    comb = jnp.where(lane == i1, w1, 0.0) + jnp.where(lane == i2, e2 * w1, 0.0)
    combt = comb.T[0:combt_ref.shape[0], :]
    comb_ref[...] = comb
    combt_ref[...] = combt
    ts = comb.shape[0]
    tt = lax.broadcasted_iota(jnp.int32, (ts, ts), 0)
    uu = lax.broadcasted_iota(jnp.int32, (ts, ts), 1)
    live = jnp.where(comb > 0.0, 1.0, 0.0)
    rcol_ref[...] = _dot((uu < tt).astype(BF16), live)
    rrow_ref[...] = _dot(jnp.where(combt > 0.0, 1.0, 0.0), (tt < uu).astype(BF16))
    cnt_ref[0] = jnp.sum(live, axis=0, keepdims=True).astype(jnp.int32)


def _router(x, g, wr, ts):
    t, d = x.shape
    e = wr.shape[1]
    ep = max(8, e)
    wrp = jnp.zeros((d, LANES), F32).at[:, :e].set(wr)
    nt = t // ts
    return pl.pallas_call(
        functools.partial(_router_kernel, n_experts=e),
        grid=(nt,),
        in_specs=[pl.BlockSpec((ts, d), lambda i: (i, 0)),
                  pl.BlockSpec((1, d), lambda i: (0, 0)),
                  pl.BlockSpec((d, LANES), lambda i: (0, 0))],
        out_specs=[pl.BlockSpec((ts, d), lambda i: (i, 0)),
                   pl.BlockSpec((ts, LANES), lambda i: (i, 0)),
                   pl.BlockSpec((ep, ts), lambda i: (0, i)),
                   pl.BlockSpec((ts, LANES), lambda i: (i, 0)),
                   pl.BlockSpec((ep, ts), lambda i: (0, i)),
                   pl.BlockSpec((1, 1, LANES), lambda i: (i, 0, 0))],
        out_shape=[jax.ShapeDtypeStruct((t, d), BF16),
                   jax.ShapeDtypeStruct((t, LANES), F32),
                   jax.ShapeDtypeStruct((ep, t), F32),
                   jax.ShapeDtypeStruct((t, LANES), F32),
                   jax.ShapeDtypeStruct((ep, t), F32),
                   jax.ShapeDtypeStruct((nt, 1, LANES), jnp.int32)],
        compiler_params=_params("parallel"),
    )(x, g.reshape(1, d), wrp)


MOE_BM = 256
MOE_BMF = 512
NO_MATCH = -(1 << 20)


def _moe_plan(counts, nt, ne, t, bm, bmf):
    i32 = jnp.int32
    cnt = counts.reshape(nt, ne).astype(i32)
    tot = jnp.sum(cnt, axis=0)
    ptot = (tot + bmf - 1) // bmf * bmf
    eend = jnp.cumsum(ptot)
    ebase = eend - ptot
    seg0 = ebase[None, :] + jnp.cumsum(cnt, axis=0) - cnt
    seg1 = seg0 + cnt
    nrows = 2 * t + ne * bmf
    nblk = nrows // bmf
    count_le = lambda ends, v: jnp.sum((ends[None, :] <= v[:, None]).astype(i32), axis=1)
    blk_exp = jnp.minimum(count_le(eend, jnp.arange(nblk, dtype=i32) * bmf), ne - 1)
    nvalid = (eend[-1] // bmf).astype(i32).reshape(1)
    npairs = nrows // bm + nt * ne
    g = jnp.arange(npairs, dtype=i32)

    def pairs(c0, c1, seg_tile, seg_exp, dummy_blk):
        n = jnp.where(c1 > c0, (c1 - 1) // bm - c0 // bm + 1, 0)
        pend = jnp.cumsum(n)
        k = jnp.minimum(count_le(pend, g), c0.shape[0] - 1)
        valid = g < pend[-1]
        blk = c0[k] // bm + (g - (pend[k] - n[k]))
        blk = jnp.where(valid, blk, dummy_blk)
        delta = jnp.where(valid, c0[k] - blk * bm, NO_MATCH)
        return (blk.astype(i32), jnp.where(valid, seg_tile[k], nt - 1).astype(i32),
                seg_exp[k].astype(i32), delta.astype(i32))

    tiles = jnp.arange(nt, dtype=i32)
    exps = jnp.arange(ne, dtype=i32)
    c1g = seg1.at[nt - 1].set(eend)
    g_blk, g_tile, g_exp, g_delta = pairs(seg0.T.reshape(-1), c1g.T.reshape(-1),
                                          jnp.tile(tiles, ne), jnp.repeat(exps, nt), nrows // bm)
    g_first = jnp.concatenate([jnp.ones((1,), i32), (g_blk[1:] != g_blk[:-1]).astype(i32)])
    s_blk, s_tile, s_exp, s_delta = pairs(seg0.reshape(-1), seg1.reshape(-1),
                                          jnp.repeat(tiles, ne), jnp.tile(exps, nt), 0)
    s_first = jnp.concatenate([jnp.ones((1,), i32), (s_tile[1:] != s_tile[:-1]).astype(i32)])
    return dict(nrows=nrows, blk_exp=blk_exp, nvalid=nvalid,
                gather=(g_blk, g_tile, g_exp, g_delta, g_first),
                combine=(s_blk, s_tile, s_exp, s_delta, s_first))


def _moe_gather_kernel(blk_ref, tile_ref, exp_ref, delta_ref, first_ref, h_ref, rrow_ref, combt_ref,
                       o_ref):
    g = pl.program_id(0)
    e = exp_ref[g]
    bm, ts = o_ref.shape[0], h_ref.shape[0]
    pos = rrow_ref[pl.ds(e, 1), :] + delta_ref[g].astype(F32)
    live = combt_ref[pl.ds(e, 1), :] > 0.0
    slot = lax.broadcasted_iota(jnp.int32, (bm, ts), 0).astype(F32)
    onehot = jnp.where((pos == slot) & live, 1.0, 0.0).astype(BF16)
    val = jnp.dot(onehot, h_ref[...], preferred_element_type=F32).astype(BF16)

    @pl.when(first_ref[g] == 1)
    def _():
        o_ref[...] = val

    @pl.when(first_ref[g] == 0)
    def _():
        o_ref[...] += val


def _moe_gather(plan, h, rrow, combt, ts, bm):
    t, d = h.shape
    ep = combt.shape[0]
    blk, tile, exp, delta, first = plan["gather"]
    grid_spec = pltpu.PrefetchScalarGridSpec(
        num_scalar_prefetch=5,
        grid=(blk.shape[0],),
        in_specs=[pl.BlockSpec((ts, d), lambda g, b_, t_, e_, d_, f_: (t_[g], 0)),
                  pl.BlockSpec((ep, ts), lambda g, b_, t_, e_, d_, f_: (0, t_[g])),
                  pl.BlockSpec((ep, ts), lambda g, b_, t_, e_, d_, f_: (0, t_[g]))],
        out_specs=pl.BlockSpec((bm, d), lambda g, b_, t_, e_, d_, f_: (b_[g], 0)),
    )
    return pl.pallas_call(
        _moe_gather_kernel,
        grid_spec=grid_spec,
        out_shape=jax.ShapeDtypeStruct((plan["nrows"] + MOE_BMF, d), BF16),
        compiler_params=_params("arbitrary"),
    )(blk, tile, exp, delta, first, h, rrow, combt)


def _moe_ffn_kernel(bexp_ref, nv_ref, x_ref, wg_ref, wu_ref, wd_ref, o_ref, acc_ref):
    i, f = pl.program_id(0), pl.program_id(1)
    valid = i < nv_ref[0]

    @pl.when(valid)
    def _():
        @pl.when(f == 0)
        def _():
            acc_ref[...] = jnp.zeros(acc_ref.shape, F32)

        for r, y in _swiglu_rows(x_ref, wg_ref, wu_ref, wd_ref):
            acc_ref[r, :] += y

    @pl.when(f == pl.num_programs(1) - 1)
    def _():
        @pl.when(valid)
        def _():
            o_ref[...] = acc_ref[...].astype(BF16)

        @pl.when(jnp.logical_not(valid))
        def _():
            o_ref[...] = jnp.zeros(o_ref.shape, BF16)


def _moe_ffn(plan, xs, wg, wu, wd, li, bmf, tf):
    d = xs.shape[1]
    ff = wg.shape[3]
    nblk = plan["blk_exp"].shape[0]
    grid_spec = pltpu.PrefetchScalarGridSpec(
        num_scalar_prefetch=2,
        grid=(nblk, ff // tf),
        in_specs=[pl.BlockSpec((bmf, d), lambda i, f, be, nv: (i, 0)),
                  pl.BlockSpec((None, None, d, tf), lambda i, f, be, nv: (li, be[i], 0, f)),
                  pl.BlockSpec((None, None, d, tf), lambda i, f, be, nv: (li, be[i], 0, f)),
                  pl.BlockSpec((None, None, tf, d), lambda i, f, be, nv: (li, be[i], f, 0))],
        out_specs=pl.BlockSpec((bmf, d), lambda i, f, be, nv: (i, 0)),
        scratch_shapes=[pltpu.VMEM((bmf, d), F32)],
    )
    return pl.pallas_call(
        _moe_ffn_kernel,
        grid_spec=grid_spec,
        out_shape=jax.ShapeDtypeStruct((nblk * bmf, d), BF16),
        compiler_params=_params("parallel", "arbitrary"),
    )(plan["blk_exp"], plan["nvalid"], xs, wg, wu, wd)


def _moe_combine_kernel(blk_ref, tile_ref, exp_ref, delta_ref, first_ref, y_ref, x_ref, comb_ref,
                        rcol_ref, o_ref):
    g = pl.program_id(0)
    e = exp_ref[g]
    ts, bm = x_ref.shape[0], y_ref.shape[0]

    @pl.when(first_ref[g] == 1)
    def _():
        o_ref[...] = x_ref[...]

    lane = lax.broadcasted_iota(jnp.int32, comb_ref.shape, 1)
    sel = lane == e
    cw = jnp.sum(jnp.where(sel, comb_ref[...], 0.0), axis=-1, keepdims=True)
    pos = (jnp.sum(jnp.where(sel, rcol_ref[...], 0.0), axis=-1, keepdims=True)
           + delta_ref[g].astype(F32))
    slot = lax.broadcasted_iota(jnp.int32, (ts, bm), 1).astype(F32)
    onehot = jnp.where((pos == slot) & (cw > 0.0), 1.0, 0.0).astype(BF16)
    o_ref[...] += cw * jnp.dot(onehot, y_ref[...], preferred_element_type=F32)


def _moe_combine(plan, ys, x, comb, rcol, ts, bm):
    t, d = x.shape
    blk, tile, exp, delta, first = plan["combine"]
    grid_spec = pltpu.PrefetchScalarGridSpec(
        num_scalar_prefetch=5,
        grid=(blk.shape[0],),
        in_specs=[pl.BlockSpec((bm, d), lambda g, b_, t_, e_, d_, f_: (b_[g], 0)),
                  pl.BlockSpec((ts, d), lambda g, b_, t_, e_, d_, f_: (t_[g], 0)),
                  pl.BlockSpec((ts, LANES), lambda g, b_, t_, e_, d_, f_: (t_[g], 0)),
                  pl.BlockSpec((ts, LANES), lambda g, b_, t_, e_, d_, f_: (t_[g], 0))],
        out_specs=pl.BlockSpec((ts, d), lambda g, b_, t_, e_, d_, f_: (t_[g], 0)),
    )
    return pl.pallas_call(
        _moe_combine_kernel,
        grid_spec=grid_spec,
        out_shape=jax.ShapeDtypeStruct((t, d), F32),
        compiler_params=_params("arbitrary"),
    )(blk, tile, exp, delta, first, ys, x, comb, rcol)


def _moe(x, h, comb, combt, rcol, rrow, counts, wg, wu, wd, li, ts, tf, bm=MOE_BM, bmf=MOE_BMF):
    t = x.shape[0]
    plan = _moe_plan(counts, t // ts, wg.shape[1], t, bm, bmf)
    xs = _moe_gather(plan, h, rrow, combt, ts, bm)
    ys = _moe_ffn(plan, xs, wg, wu, wd, li, bmf, tf)
    return _moe_combine(plan, ys, x, comb, rcol, ts, bm)


def kernel(x, norm1_g, w_in, da_q_norm, da_k_norm, da_lambda, da_out_norm, dl_q_norm, dl_k_norm, rw_mu, rw_w0, rw_w2, rw_a0, rw_a2, rw_g2, rw_k_k, rw_k_a, rw_r_k, rw_ln_g, rw_ln_b, w_out, norm2_g, ffn_w_gate, ffn_w_up, ffn_w_down, moe_router, moe_w_gate, moe_w_up, moe_w_down):
    b, s, d = x.shape
    depth = w_in.shape[0]
    t = b * s
    n_experts = moe_router.shape[-1]
    xt = x.reshape(t, d)

    qa, ka, va = 0, DA_HEADS, 2 * DA_HEADS
    qb, kb, vb = 3 * DA_HEADS, 3 * DA_HEADS + DL_HEADS, 3 * DA_HEADS + 2 * DL_HEADS
    rw = 3 * DA_HEADS + 3 * DL_HEADS

    tm = min(1024, t)
    tq = min(256, s)
    tf = 1408
    assert ffn_w_gate.shape[2] % tf == 0 and moe_w_gate.shape[3] % tf == 0

    w_in, w_out = w_in.astype(BF16), w_out.astype(BF16)
    ffn_w = [w.astype(BF16) for w in (ffn_w_gate, ffn_w_up, ffn_w_down)]
    moe_w = [w.astype(BF16) for w in (moe_w_gate, moe_w_up, moe_w_down)]

    for l in range(depth):
        p = _inproj(xt, norm1_g[l], w_in, l, tm, 1280)

        lam_init = 0.8 - 0.6 * math.exp(-0.3 * l)
        oa = _diffattn(p, qa, ka, va, da_q_norm[l], da_k_norm[l], da_lambda[l], da_out_norm[l],
                       b, s, lam_init, tq)
        oc = _dilattn(p, qb, kb, vb, dl_q_norm[l], dl_k_norm[l], b, s, tq)

        r, lw, k2, v, na, bb, g = _rwprep(p, rw, s, rw_mu[l], rw_w0[l], rw_a0[l], rw_k_k[l],
                                          rw_k_a[l], rw_w2[l], rw_a2[l], rw_g2[l], min(256, s))
        y = _rwkv(r, lw, k2, v, na, bb, b, s, min(512, s), 12)
        ob = _rwpost(y, r, k2, v, g, rw_ln_g[l], rw_ln_b[l], rw_r_k[l].reshape(-1), min(512, s))

        xt = _outproj(xt, oa, oc, ob, w_out, l, tm, 1024)

        i = l // 2
        if l % 2 == 0:
            xt = _ffn(xt, norm2_g[l], *ffn_w, i, min(512, t), tf)
        else:
            ts = min(1024, t)
            h, comb, combt, rcol, rrow, cnt = _router(xt, norm2_g[l], moe_router[i], ts)
            counts = cnt[:, 0, :n_experts].reshape(-1)
            xt = _moe(xt, h, comb, combt, rcol, rrow, counts, *moe_w, i, ts, tf)
    return xt.reshape(b, s, d)
```

```python
import functools
import math

import numpy as np
import jax
import jax.numpy as jnp
from jax import lax
from jax.experimental import pallas as pl
from jax.experimental.pallas import tpu as pltpu

F32 = jnp.float32
BF16 = jnp.bfloat16

LANES = 128
VMEM_LIMIT = 56 * 1024 * 1024

NEG = -1e30
ROPE_THETA = 10000.0
NORM_EPS = 1e-6
RW_LN_EPS = 64e-5
DL_PATTERNS = ((128, 1), (512, 4), (2048, 16))
TOP_K = 2

DA_HEADS, DA_QK = 4, 64
DL_HEADS = 6
RW_HEADS, RW_DIM = 12, 64
DA_W, DL_W, RW_W = 512, 768, 768
RW_PAIRS = RW_W // LANES
CHUNK = 64


def _params(*sem):
    return pltpu.CompilerParams(dimension_semantics=sem, vmem_limit_bytes=VMEM_LIMIT)


def _dot(a, b):
    return jnp.dot(a.astype(BF16), b.astype(BF16), preferred_element_type=F32)


def _dot_nt(a, b):
    return lax.dot_general(a.astype(BF16), b.astype(BF16), (((1,), (1,)), ((), ())),
                           preferred_element_type=F32)


def _dot_tn(a, b):
    return lax.dot_general(a.astype(BF16), b.astype(BF16), (((0,), (0,)), ((), ())),
                           preferred_element_type=F32)


def _split(x):
    hi = x.astype(BF16)
    lo = (x - hi.astype(F32)).astype(BF16)
    return hi, lo


def _dot3(a, b, b_parts=None):
    ah, al = _split(a)
    bh, bl = _split(b) if b_parts is None else b_parts
    return _dot(ah, bh) + _dot(ah, bl) + _dot(al, bh)


def _dot_exact_rhs(a, b_bf16):
    ah, al = _split(a)
    return _dot(ah, b_bf16) + _dot(al, b_bf16)


def _group_ones(width, group):
    i = lax.broadcasted_iota(jnp.int32, (width, width), 0) // group
    j = lax.broadcasted_iota(jnp.int32, (width, width), 1) // group
    return (i == j).astype(BF16)


def _rms(x, g):
    return x * lax.rsqrt(jnp.mean(x * x, axis=-1, keepdims=True) + NORM_EPS) * g


def _cat_blocks(ref, n):
    return jnp.concatenate([ref[c] for c in range(n)], axis=1)


def _put_blocks(ref, val):
    for c in range(ref.shape[0]):
        ref[c] = val[:, c * LANES:(c + 1) * LANES].astype(ref.dtype)


def _inproj_kernel(x_ref, g_ref, w_ref, o_ref, xn_ref):
    @pl.when(pl.program_id(1) == 0)
    def _():
        xn_ref[...] = _rms(x_ref[...], g_ref[...]).astype(BF16)

    _put_blocks(o_ref, jnp.dot(xn_ref[...], w_ref[...], preferred_element_type=F32))


def _inproj(x, g, w, li, tm, tn):
    t, d = x.shape
    n = w.shape[2]
    return pl.pallas_call(
        _inproj_kernel,
        grid=(t // tm, n // tn),
        in_specs=[pl.BlockSpec((tm, d), lambda i, j: (i, 0)),
                  pl.BlockSpec((1, d), lambda i, j: (0, 0)),
                  pl.BlockSpec((None, d, tn), lambda i, j: (li, 0, j))],
        out_specs=pl.BlockSpec((tn // LANES, tm, LANES), lambda i, j: (j, i, 0)),
        out_shape=jax.ShapeDtypeStruct((n // LANES, t, LANES), F32),
        scratch_shapes=[pltpu.VMEM((tm, d), BF16)],
        compiler_params=_params("parallel", "arbitrary"),
    )(x, g.reshape(1, d), w)


LOG2E = math.log2(math.e)


def _prep_qk(x, gain, cos, sin, group, post=1.0):
    ones = _group_ones(LANES, group)
    ms = _dot_exact_rhs(x * x, ones) * (1.0 / group)
    y = x * lax.rsqrt(ms + NORM_EPS) * gain
    half = group // 2
    if group == LANES:
        partner = pltpu.roll(y, half, axis=1)
    else:
        lane = lax.broadcasted_iota(jnp.int32, y.shape, 1)
        fwd = pltpu.roll(y, LANES - half, axis=1)
        bwd = pltpu.roll(y, half, axis=1)
        partner = jnp.where((lane % group) < half, fwd, bwd)
    return ((y * cos + partner * sin) * post).astype(BF16)


def _rope_tables(s, group):
    half = group // 2
    lane = np.arange(LANES)
    inv = ROPE_THETA ** (-jnp.asarray(lane % half, F32) / half)
    ang = jnp.arange(s, dtype=F32)[:, None] * inv[None, :]
    sign = jnp.asarray(np.where((lane % group) < half, -1.0, 1.0), F32)
    return jnp.cos(ang), jnp.sin(ang) * sign[None, :]


def _tile_gain(gain, group):
    return jnp.tile(gain.astype(F32), LANES // group).reshape(1, LANES)


def _block_pairs(nq):
    return [(j, qi) for j in range(nq) for qi in range(j, nq)]


def _vt_blocks(v_ref, tq, nq):
    return [v_ref[j * tq:(j + 1) * tq, :].T.astype(BF16) for j in range(nq)]


def _diffattn_kernel(q_ref, k_ref, v_ref, cos_ref, sin_ref, gq_ref, gk_ref, lam_ref, go_ref, o_ref,
                     s0_ref, s1_ref, *, tq, scale, lam_init):
    nq = q_ref.shape[0] // tq
    cos, sin = cos_ref[...], sin_ref[...]
    kp = _prep_qk(k_ref[...], gk_ref[...], cos, sin, DA_QK)
    qp = _prep_qk(q_ref[...], gq_ref[...], cos, sin, DA_QK, scale * LOG2E)
    lane = lax.broadcasted_iota(jnp.int32, qp.shape, 1)
    zero = jnp.zeros_like(qp)
    qs = (jnp.where(lane < DA_QK, qp, zero), jnp.where(lane >= DA_QK, qp, zero))
    vt = _vt_blocks(v_ref, tq, nq)
    blk = lambda x, i: x[i * tq:(i + 1) * tq, :]
    causal = (lax.broadcasted_iota(jnp.int32, (tq, tq), 1)
              >= lax.broadcasted_iota(jnp.int32, (tq, tq), 0))
    pairs = _block_pairs(nq)
    s_refs = (s0_ref, s1_ref)

    m = [[jnp.full((1, tq), NEG, F32) for _ in range(nq)] for _ in range(2)]
    for idx, (j, qi) in enumerate(pairs):
        for c in range(2):
            sc = _dot_nt(blk(kp, j), blk(qs[c], qi))
            if j == qi:
                sc = jnp.where(causal, sc, NEG)
            s_refs[c][idx] = sc
            m[c][qi] = jnp.maximum(m[c][qi], jnp.max(sc, axis=0, keepdims=True))

    l = [[jnp.zeros((1, tq), F32) for _ in range(nq)] for _ in range(2)]
    acc = [[None] * nq for _ in range(2)]
    for idx, (j, qi) in enumerate(pairs):
        for c in range(2):
            pr = jnp.exp2(s_refs[c][idx] - m[c][qi])
            l[c][qi] = l[c][qi] + jnp.sum(pr, axis=0, keepdims=True)
            d = _dot(vt[j], pr)
            acc[c][qi] = d if acc[c][qi] is None else acc[c][qi] + d

    lm = lam_ref[...]
    lam = (jnp.exp(jnp.sum(lm[0:1] * lm[1:2], axis=-1, keepdims=True))
           - jnp.exp(jnp.sum(lm[2:3] * lm[3:4], axis=-1, keepdims=True)) + lam_init)
    for qi in range(nq):
        ot = acc[0][qi] / l[0][qi] - lam * (acc[1][qi] / l[1][qi])
        ot = ot * lax.rsqrt(jnp.mean(ot * ot, axis=0, keepdims=True) + NORM_EPS)
        o_ref[qi * tq:(qi + 1) * tq, :] = (ot.T * go_ref[...] * (1.0 - lam_init)).astype(BF16)


def _diffattn(p, qcb, kcb, vcb, gq, gk, lam4, gout, b, s, lam_init, tq):
    t = p.shape[1]
    nq = s // tq
    npair = nq * (nq + 1) // 2
    cos, sin = _rope_tables(s, DA_QK)
    blk = lambda cb0: pl.BlockSpec((None, s, LANES), lambda bi, h: (cb0 + h, bi, 0))
    full = lambda shape: pl.BlockSpec(shape, lambda bi, h: (0, 0))
    return pl.pallas_call(
        functools.partial(_diffattn_kernel, tq=tq, scale=DA_QK ** -0.5, lam_init=lam_init),
        grid=(b, DA_HEADS),
        in_specs=[blk(qcb), blk(kcb), blk(vcb),
                  full((s, LANES)), full((s, LANES)), full((1, LANES)), full((1, LANES)),
                  full((4, DA_QK)), full((1, LANES))],
        out_specs=blk(0),
        out_shape=jax.ShapeDtypeStruct((DA_HEADS, t, LANES), BF16),
        scratch_shapes=[pltpu.VMEM((npair, tq, tq), F32), pltpu.VMEM((npair, tq, tq), F32)],
        compiler_params=_params("parallel", "parallel"),
    )(p, p, p, cos, sin, _tile_gain(gq, DA_QK), _tile_gain(gk, DA_QK), lam4.astype(F32),
      gout.astype(F32).reshape(1, LANES))


def _dilated_bias(s, tq):
    nd = s // tq
    d = (np.arange(nd)[:, None, None] * tq + np.arange(tq)[None, None, :]
         - np.arange(tq)[None, :, None])
    cnt = np.zeros(d.shape, np.float64)
    for window, dil in DL_PATTERNS:
        cnt += (d >= 0) & (d % dil == 0) & (d <= window)
    bias = np.where(cnt > 0, np.log2(np.maximum(cnt, 1.0)), NEG)
    return jnp.asarray(bias, F32)


def _dilattn_kernel(q_ref, k_ref, v_ref, cos_ref, sin_ref, gq_ref, gk_ref, bias_ref, o_ref, s_ref,
                    *, tq, scale):
    nq = q_ref.shape[0] // tq
    cos, sin = cos_ref[...], sin_ref[...]
    kp = _prep_qk(k_ref[...], gk_ref[...], cos, sin, LANES)
    qp = _prep_qk(q_ref[...], gq_ref[...], cos, sin, LANES, scale * LOG2E)
    vt = _vt_blocks(v_ref, tq, nq)
    blk = lambda x, i: x[i * tq:(i + 1) * tq, :]
    pairs = _block_pairs(nq)

    m = [jnp.full((1, tq), NEG, F32) for _ in range(nq)]
    for idx, (j, qi) in enumerate(pairs):
        sc = _dot_nt(blk(kp, j), blk(qp, qi)) + bias_ref[qi - j]
        s_ref[idx] = sc
        m[qi] = jnp.maximum(m[qi], jnp.max(sc, axis=0, keepdims=True))

    l = [jnp.zeros((1, tq), F32) for _ in range(nq)]
    acc = [None] * nq
    for idx, (j, qi) in enumerate(pairs):
        pr = jnp.exp2(s_ref[idx] - m[qi])
        l[qi] = l[qi] + jnp.sum(pr, axis=0, keepdims=True)
        d = _dot(vt[j], pr)
        acc[qi] = d if acc[qi] is None else acc[qi] + d

    for qi in range(nq):
        o_ref[qi * tq:(qi + 1) * tq, :] = (acc[qi] / l[qi]).T.astype(BF16)


def _dilattn(p, qcb, kcb, vcb, gq, gk, b, s, tq):
    t = p.shape[1]
    nq = s // tq
    cos, sin = _rope_tables(s, LANES)
    bias = _dilated_bias(s, tq)
    blk = lambda cb0: pl.BlockSpec((None, s, LANES), lambda bi, h: (cb0 + h, bi, 0))
    full = lambda shape: pl.BlockSpec(shape, lambda bi, h: (0,) * len(shape))
    return pl.pallas_call(
        functools.partial(_dilattn_kernel, tq=tq, scale=LANES ** -0.5),
        grid=(b, DL_HEADS),
        in_specs=[blk(qcb), blk(kcb), blk(vcb),
                  full((s, LANES)), full((s, LANES)), full((1, LANES)), full((1, LANES)),
                  full((nq, tq, tq))],
        out_specs=blk(0),
        out_shape=jax.ShapeDtypeStruct((DL_HEADS, t, LANES), BF16),
        scratch_shapes=[pltpu.VMEM((nq * (nq + 1) // 2, tq, tq), F32)],
        compiler_params=_params("parallel", "parallel"),
    )(p, p, p, cos, sin, _tile_gain(gq, LANES), _tile_gain(gk, LANES), bias)


def _rwprep_kernel(r_ref, k_ref, v_ref, x_ref, rp_ref, kp_ref, vp_ref, xp_ref,
                   mu_ref, w0_ref, a0_ref, kk_ref, ka_ref, w2_ref, a2_ref, g2_ref,
                   ro, lwo, ko, vo, nao, bo, go, *, rows_per_seq):
    i = pl.program_id(0)
    tm = r_ref.shape[1]
    first = (i * tm) % rows_per_seq == 0
    row = lax.broadcasted_iota(jnp.int32, (tm, 1), 0)

    def shifted(cur_ref, prev_ref, mu):
        n = cur_ref.shape[0]
        cur = _cat_blocks(cur_ref, n)
        last = jnp.concatenate([prev_ref[c, 7:8, :] for c in range(n)], axis=1)
        last = jnp.where(first, 0.0, last)
        prev = jnp.where(row == 0, last, pltpu.roll(cur, 1, axis=0))
        return cur + mu * (prev - cur)

    mu = mu_ref[...]
    rr = shifted(r_ref, rp_ref, mu[:, 0:RW_W])
    kr = shifted(k_ref, kp_ref, mu[:, RW_W:2 * RW_W])
    vr = shifted(v_ref, vp_ref, mu[:, 2 * RW_W:3 * RW_W])
    xs = shifted(x_ref, xp_ref, mu[:, 3 * RW_W:])

    z = w0_ref[...] + _dot3(jnp.tanh(xs), None, (w2_ref[0], w2_ref[1]))
    nz = -z
    softplus = jnp.maximum(nz, 0.0) + jnp.log(1.0 + jnp.exp(-jnp.abs(nz)))
    w_log = -softplus - 0.5
    a = jax.nn.sigmoid(a0_ref[...] + _dot3(xs, None, (a2_ref[0], a2_ref[1])))
    g = _dot3(jax.nn.sigmoid(xs), None, (g2_ref[0], g2_ref[1]))

    ones = _group_ones(LANES, RW_DIM)
    kkr = kr * kk_ref[...]
    sq = kkr * kkr
    ss = jnp.concatenate([_dot_exact_rhs(sq[:, c * LANES:(c + 1) * LANES], ones)
                          for c in range(RW_PAIRS)], axis=1)
    kk = kkr / jnp.maximum(jnp.sqrt(ss), 1e-12)
    _put_blocks(ro, rr)
    _put_blocks(lwo, -jnp.exp(w_log))
    _put_blocks(ko, kr * (1.0 + (a - 1.0) * ka_ref[...]))
    _put_blocks(vo, vr)
    _put_blocks(nao, -kk)
    _put_blocks(bo, kk * a)
    _put_blocks(go, g)


def _rwprep(p, cb0, s, mu, w0, a0, k_k, k_a, w2, a2, g2, tm):
    t = p.shape[1]
    lr = mu.shape[0] - 3 * RW_W
    nx = lr // LANES
    hilo = lambda w_: jnp.stack(_split(w_))
    w2p = hilo(jnp.zeros((lr, RW_W), F32).at[0:w2.shape[0]].set(w2))
    a2p = hilo(jnp.zeros((lr, RW_W), F32).at[w2.shape[0]:w2.shape[0] + a2.shape[0]].set(a2))
    g2p = hilo(jnp.zeros((lr, RW_W), F32).at[lr - g2.shape[0]:].set(g2))
    r8 = tm // 8
    gb = cb0 // RW_PAIRS
    xb = (cb0 + 3 * RW_PAIRS) // nx

    def cur(c, n):
        return pl.BlockSpec((n, tm, LANES), lambda i: (c, i, 0))

    def prev(c, n):
        return pl.BlockSpec((n, 8, LANES), lambda i: (c, jnp.maximum(i * r8 - 1, 0), 0))

    def full(shape):
        return pl.BlockSpec(shape, lambda i: (0,) * len(shape))

    vec = lambda a_: a_.astype(F32).reshape(1, -1)
    out = jax.ShapeDtypeStruct((RW_PAIRS, t, LANES), F32)
    return pl.pallas_call(
        functools.partial(_rwprep_kernel, rows_per_seq=s),
        grid=(t // tm,),
        in_specs=[cur(gb, RW_PAIRS), cur(gb + 1, RW_PAIRS), cur(gb + 2, RW_PAIRS), cur(xb, nx),
                  prev(gb, RW_PAIRS), prev(gb + 1, RW_PAIRS), prev(gb + 2, RW_PAIRS), prev(xb, nx),
                  full((1, 3 * RW_W + lr)), full((1, RW_W)), full((1, RW_W)), full((1, RW_W)),
                  full((1, RW_W)), full((2, lr, RW_W)), full((2, lr, RW_W)), full((2, lr, RW_W))],
        out_specs=[pl.BlockSpec((RW_PAIRS, tm, LANES), lambda i: (0, i, 0))] * 7,
        out_shape=[out] * 7,
        compiler_params=_params("parallel"),
    )(p, p, p, p, p, p, p, p, vec(mu), vec(w0), vec(a0), vec(k_k), vec(k_a), w2p, a2p, g2p)


def _rwkv_kernel(r_ref, lw_ref, k_ref, v_ref, a_ref, b_ref, y_ref, rh_ref, yh_ref, p_ref, q_ref,
                 st_ref, *, nchunk, unroll):
    L = CHUNK
    W = 2 * L
    npair = r_ref.shape[0]

    @pl.when(pl.program_id(1) == 0)
    def _():
        st_ref[...] = jnp.zeros(st_ref.shape, F32)

    lane = lax.broadcasted_iota(jnp.int32, (L, W), 1)
    rowi = lax.broadcasted_iota(jnp.int32, (L, W), 0)
    strict = (lane % L) < rowi
    incl = (lane % L) <= rowi
    rr = lax.broadcasted_iota(jnp.int32, (W, W), 0)
    cc = lax.broadcasted_iota(jnp.int32, (W, W), 1)
    same = (rr // L) == (cc // L)
    eye = rr == cc
    tl = lax.broadcasted_iota(jnp.int32, (L, L), 0)
    sl = lax.broadcasted_iota(jnp.int32, (L, L), 1)
    tril = (sl <= tl).astype(BF16)

    def bd(x):
        return jnp.where(same, jnp.concatenate([x, x], axis=0), 0.0)

    def group(gi, carry):
        us = [gi * unroll + i for i in range(unroll)]
        prs = [u // nchunk for u in us]
        rws = [pl.ds(pl.multiple_of((u % nchunk) * L, L), L) for u in us]
        G = range(unroll)
        ld = lambda ref: [ref[prs[i], rws[i], :] for i in G]
        r, lw, k, v, a, b = ld(r_ref), ld(lw_ref), ld(k_ref), ld(v_ref), ld(a_ref), ld(b_ref)

        def csum(x):
            hi, lo = _split(x)
            rest = x - hi.astype(F32) - lo.astype(F32)
            return jnp.concatenate([hi, lo, rest.astype(BF16)], axis=1)

        c3 = [_dot(tril, csum(lw[i])) for i in G]
        cin = [c[:, 0:W] + c[:, W:2 * W] + c[:, 2 * W:] for c in c3]
        clast = [c[L - 1:L, :] for c in cin]
        g_inv = [jnp.exp(-c) for c in cin]
        g_tail = [jnp.exp(clast[i] - cin[i]) for i in G]
        at = [a[i] * jnp.exp(cin[i] - lw[i]) for i in G]
        rt = [r[i] * jnp.exp(cin[i]) for i in G]
        abk = [_dot_nt(jnp.concatenate([at[i], rt[i]], axis=0),
                       jnp.concatenate([bd(b[i] * g_inv[i]), bd(k[i] * g_inv[i])], axis=0)) for i in G]
        a_ab = [jnp.where(strict, m[0:L, 0:W], 0.0) for m in abk]
        a_rb = [jnp.where(incl, m[L:W, 0:W], 0.0) for m in abk]
        a_ak = [jnp.where(strict, m[0:L, W:], 0.0) for m in abk]
        a_rk = [jnp.where(incl, m[L:W, W:], 0.0) for m in abk]
        n = [bd(m) for m in a_ab]
        tm = [jnp.where(eye, 1.0, m) for m in n]
        x = [_dot(m, m) for m in n]
        for j in range(5):
            if j < 4:
                xx = [_dot(x[i], jnp.concatenate([x[i], tm[i]], axis=1)) for i in G]
                x = [m[:, 0:W] for m in xx]
                tm = [tm[i] + xx[i][:, W:] for i in G]
            else:
                tm = [tm[i] + _dot(x[i], tm[i]) for i in G]
        v_bd = [bd(m) for m in v]
        kv = [_dot(jnp.concatenate([bd(a_ak[i]), a_rk[i]], axis=0), v_bd[i]) for i in G]
        au = [_dot(tm[i], jnp.concatenate([bd(at[i]), kv[i][0:W]], axis=1)) for i in G]
        ry = [_dot(a_rb[i], au[i]) for i in G]
        zero = jnp.zeros((W, W), F32)
        pq = [_dot_tn(jnp.concatenate([bd(b[i] * g_tail[i]), bd(k[i] * g_tail[i])], axis=0),
                      jnp.concatenate([au[i], jnp.concatenate([zero, v_bd[i]], axis=1)], axis=0))
              for i in G]
        for i in G:
            rh_ref[prs[i], rws[i], :] = rt[i] + ry[i][:, 0:W]
            yh_ref[prs[i], rws[i], :] = ry[i][:, W:] + kv[i][W:]
            p_ref[us[i]] = jnp.where(eye, jnp.exp(clast[i]), 0.0) + pq[i][:, 0:W]
            q_ref[us[i]] = pq[i][:, W:]
        return carry

    lax.fori_loop(0, npair * nchunk // unroll, group, 0)

    def step(c, carry):
        rows = pl.ds(pl.multiple_of(c * L, L), L)
        for pr in range(npair):
            st = st_ref[pr].astype(BF16)
            y_ref[pr, rows, :] = _dot(rh_ref[pr, rows, :], st) + yh_ref[pr, rows, :]
            st_ref[pr] = _dot(p_ref[pr * nchunk + c], st) + q_ref[pr * nchunk + c]
        return carry

    lax.fori_loop(0, nchunk, step, 0)


def _rwkv(r, lw, k, v, na, bb, b, s, sb, unroll):
    npair, t, w = r.shape
    nchunk = sb // CHUNK
    nsb = s // sb
    spec = pl.BlockSpec((npair, sb, w), lambda bi, si: (0, bi * nsb + si, 0))
    return pl.pallas_call(
        functools.partial(_rwkv_kernel, nchunk=nchunk, unroll=unroll),
        grid=(b, nsb),
        in_specs=[spec] * 6,
        out_specs=spec,
        out_shape=jax.ShapeDtypeStruct((npair, t, w), F32),
        scratch_shapes=[pltpu.VMEM((npair, sb, w), F32), pltpu.VMEM((npair, sb, w), F32),
                        pltpu.VMEM((npair * nchunk, w, w), F32), pltpu.VMEM((npair * nchunk, w, w), F32),
                        pltpu.VMEM((npair, w, w), F32)],
        compiler_params=_params("parallel", "arbitrary"),
    )(r, lw, k, v, na, bb)


def _rwpost_kernel(y_ref, r_ref, k_ref, v_ref, g_ref, lng_ref, lnb_ref, rk_ref, o_ref):
    ones = _group_ones(LANES, RW_DIM)
    for c in range(RW_PAIRS):
        cols = slice(c * LANES, (c + 1) * LANES)
        y = y_ref[c]
        mu = _dot_exact_rhs(y, ones) * (1.0 / RW_DIM)
        yc = y - mu
        var = _dot_exact_rhs(yc * yc, ones) * (1.0 / RW_DIM)
        out = yc * lax.rsqrt(var + RW_LN_EPS) * lng_ref[:, cols] + lnb_ref[:, cols]
        bonus = _dot_exact_rhs(r_ref[c] * k_ref[c] * rk_ref[:, cols], ones)
        o_ref[c] = ((out + bonus * v_ref[c]) * g_ref[c]).astype(o_ref.dtype)


def _rwpost(y, r, k, v, g, ln_g, ln_b, r_k, tm):
    t = y.shape[1]
    big = pl.BlockSpec((RW_PAIRS, tm, LANES), lambda i: (0, i, 0))
    small = pl.BlockSpec((1, RW_W), lambda i: (0, 0))
    vec = lambda a_: a_.astype(F32).reshape(1, RW_W)
    return pl.pallas_call(
        _rwpost_kernel,
        grid=(t // tm,),
        in_specs=[big] * 5 + [small] * 3,
        out_specs=big,
        out_shape=jax.ShapeDtypeStruct((RW_PAIRS, t, LANES), BF16),
        compiler_params=_params("parallel"),
    )(y, r, k, v, g, vec(ln_g), vec(ln_b), vec(r_k))


def _outproj_kernel(x_ref, oa_ref, oc_ref, ob_ref, w_ref, o_ref):
    mix = jnp.concatenate([_cat_blocks(oa_ref, oa_ref.shape[0]), _cat_blocks(oc_ref, oc_ref.shape[0]),
                           _cat_blocks(ob_ref, ob_ref.shape[0])], axis=1)
    o_ref[...] = x_ref[...] + jnp.dot(mix, w_ref[...], preferred_element_type=F32)


def _outproj(x, oa, oc, ob, w, li, tm, tn):
    t, d = x.shape
    blocks = lambda a_: pl.BlockSpec((a_.shape[0], tm, LANES), lambda i, j: (0, i, 0))
    return pl.pallas_call(
        _outproj_kernel,
        grid=(t // tm, d // tn),
        in_specs=[pl.BlockSpec((tm, tn), lambda i, j: (i, j)),
                  blocks(oa), blocks(oc), blocks(ob),
                  pl.BlockSpec((None, w.shape[1], tn), lambda i, j: (li, 0, j))],
        out_specs=pl.BlockSpec((tm, tn), lambda i, j: (i, j)),
        out_shape=jax.ShapeDtypeStruct((t, d), F32),
        compiler_params=_params("parallel", "arbitrary"),
    )(x, oa, oc, ob, w)


FFN_SPLIT = 2


def _swiglu_rows(x_ref, wg_ref, wu_ref, wd_ref):
    n = x_ref.shape[0] // FFN_SPLIT
    sl = [slice(i * n, (i + 1) * n) for i in range(FFN_SPLIT)]
    wg, wu, wd = wg_ref[...], wu_ref[...], wd_ref[...]
    gu = [(jnp.dot(x_ref[r, :], wg, preferred_element_type=F32),
           jnp.dot(x_ref[r, :], wu, preferred_element_type=F32)) for r in sl]
    act = [(g * jax.nn.sigmoid(g) * u).astype(BF16) for g, u in gu]
    return [(r, jnp.dot(a, wd, preferred_element_type=F32)) for r, a in zip(sl, act)]


def _ffn_kernel(x_ref, g_ref, wg_ref, wu_ref, wd_ref, o_ref, xn_ref, acc_ref):
    f = pl.program_id(1)

    @pl.when(f == 0)
    def _():
        xn_ref[...] = _rms(x_ref[...], g_ref[...]).astype(BF16)
        acc_ref[...] = jnp.zeros(acc_ref.shape, F32)

    for r, y in _swiglu_rows(xn_ref, wg_ref, wu_ref, wd_ref):
        acc_ref[r, :] += y

    @pl.when(f == pl.num_programs(1) - 1)
    def _():
        o_ref[...] = x_ref[...] + acc_ref[...]


def _ffn(x, g, wg, wu, wd, li, tm, tf):
    t, d = x.shape
    ff = wg.shape[2]
    return pl.pallas_call(
        _ffn_kernel,
        grid=(t // tm, ff // tf),
        in_specs=[pl.BlockSpec((tm, d), lambda i, f: (i, 0)),
                  pl.BlockSpec((1, d), lambda i, f: (0, 0)),
                  pl.BlockSpec((None, d, tf), lambda i, f: (li, 0, f)),
                  pl.BlockSpec((None, d, tf), lambda i, f: (li, 0, f)),
                  pl.BlockSpec((None, tf, d), lambda i, f: (li, f, 0))],
        out_specs=pl.BlockSpec((tm, d), lambda i, f: (i, 0)),
        out_shape=jax.ShapeDtypeStruct((t, d), F32),
        scratch_shapes=[pltpu.VMEM((tm, d), BF16), pltpu.VMEM((tm, d), F32)],
        compiler_params=_params("parallel", "arbitrary"),
    )(x, g.reshape(1, d), wg, wu, wd)


def _router_kernel(x_ref, g_ref, wr_ref, h_ref, comb_ref, combt_ref, rcol_ref, rrow_ref, cnt_ref,
                   *, n_experts):
    h = _rms(x_ref[...], g_ref[...])
    h_ref[...] = h.astype(BF16)
    logits = _dot3(h, wr_ref[...])
    lane = lax.broadcasted_iota(jnp.int32, logits.shape, 1)
    lg = jnp.where(lane < n_experts, logits, NEG)
    m1 = jnp.max(lg, axis=-1, keepdims=True)
    i1 = jnp.min(jnp.where(lg == m1, lane, LANES), axis=-1, keepdims=True)
    lg2 = jnp.where(lane == i1, NEG, lg)
    m2 = jnp.max(lg2, axis=-1, keepdims=True)
    i2 = jnp.min(jnp.where(lg2 == m2, lane, LANES), axis=-1, keepdims=True)
    e2 = jnp.exp(m2 - m1)
    w1 = 1.0 / (1.0 + e2)
    comb = jnp.where(lane == i1, w1, 0.0) + jnp.where(lane == i2, e2 * w1, 0.0)
    combt = comb.T[0:combt_ref.shape[0], :]
    comb_ref[...] = comb
    combt_ref[...] = combt
    ts = comb.shape[0]
    tt = lax.broadcasted_iota(jnp.int32, (ts, ts), 0)
    uu = lax.broadcasted_iota(jnp.int32, (ts, ts), 1)
    live = jnp.where(comb > 0.0, 1.0, 0.0)
    rcol_ref[...] = _dot((uu < tt).astype(BF16), live)
    rrow_ref[...] = _dot(jnp.where(combt > 0.0, 1.0, 0.0), (tt < uu).astype(BF16))
    cnt_ref[0] = jnp.sum(live, axis=0, keepdims=True).astype(jnp.int32)


def _router(x, g, wr, ts):
    t, d = x.shape
    e = wr.shape[1]
    ep = max(8, e)
    wrp = jnp.zeros((d, LANES), F32).at[:, :e].set(wr)
    nt = t // ts
    return pl.pallas_call(
        functools.partial(_router_kernel, n_experts=e),
        grid=(nt,),
        in_specs=[pl.BlockSpec((ts, d), lambda i: (i, 0)),
                  pl.BlockSpec((1, d), lambda i: (0, 0)),
                  pl.BlockSpec((d, LANES), lambda i: (0, 0))],
        out_specs=[pl.BlockSpec((ts, d), lambda i: (i, 0)),
                   pl.BlockSpec((ts, LANES), lambda i: (i, 0)),
                   pl.BlockSpec((ep, ts), lambda i: (0, i)),
                   pl.BlockSpec((ts, LANES), lambda i: (i, 0)),
                   pl.BlockSpec((ep, ts), lambda i: (0, i)),
                   pl.BlockSpec((1, 1, LANES), lambda i: (i, 0, 0))],
        out_shape=[jax.ShapeDtypeStruct((t, d), BF16),
                   jax.ShapeDtypeStruct((t, LANES), F32),
                   jax.ShapeDtypeStruct((ep, t), F32),
                   jax.ShapeDtypeStruct((t, LANES), F32),
                   jax.ShapeDtypeStruct((ep, t), F32),
                   jax.ShapeDtypeStruct((nt, 1, LANES), jnp.int32)],
        compiler_params=_params("parallel"),
    )(x, g.reshape(1, d), wrp)


MOE_BM = 256
MOE_BMF = 512
NO_MATCH = -(1 << 20)


def _moe_plan(counts, nt, ne, t, bm, bmf):
    i32 = jnp.int32
    cnt = counts.reshape(nt, ne).astype(i32)
    tot = jnp.sum(cnt, axis=0)
    ptot = (tot + bmf - 1) // bmf * bmf
    eend = jnp.cumsum(ptot)
    ebase = eend - ptot
    seg0 = ebase[None, :] + jnp.cumsum(cnt, axis=0) - cnt
    seg1 = seg0 + cnt
    nrows = 2 * t + ne * bmf
    nblk = nrows // bmf
    count_le = lambda ends, v: jnp.sum((ends[None, :] <= v[:, None]).astype(i32), axis=1)
    blk_exp = jnp.minimum(count_le(eend, jnp.arange(nblk, dtype=i32) * bmf), ne - 1)
    nvalid = (eend[-1] // bmf).astype(i32).reshape(1)
    npairs = nrows // bm + nt * ne
    g = jnp.arange(npairs, dtype=i32)

    def pairs(c0, c1, seg_tile, seg_exp, dummy_blk):
        n = jnp.where(c1 > c0, (c1 - 1) // bm - c0 // bm + 1, 0)
        pend = jnp.cumsum(n)
        k = jnp.minimum(count_le(pend, g), c0.shape[0] - 1)
        valid = g < pend[-1]
        blk = c0[k] // bm + (g - (pend[k] - n[k]))
        blk = jnp.where(valid, blk, dummy_blk)
        delta = jnp.where(valid, c0[k] - blk * bm, NO_MATCH)
        return (blk.astype(i32), jnp.where(valid, seg_tile[k], nt - 1).astype(i32),
                seg_exp[k].astype(i32), delta.astype(i32))

    tiles = jnp.arange(nt, dtype=i32)
    exps = jnp.arange(ne, dtype=i32)
    c1g = seg1.at[nt - 1].set(eend)
    g_blk, g_tile, g_exp, g_delta = pairs(seg0.T.reshape(-1), c1g.T.reshape(-1),
                                          jnp.tile(tiles, ne), jnp.repeat(exps, nt), nrows // bm)
    g_first = jnp.concatenate([jnp.ones((1,), i32), (g_blk[1:] != g_blk[:-1]).astype(i32)])
    s_blk, s_tile, s_exp, s_delta = pairs(seg0.reshape(-1), seg1.reshape(-1),
                                          jnp.repeat(tiles, ne), jnp.tile(exps, nt), 0)
    s_first = jnp.concatenate([jnp.ones((1,), i32), (s_tile[1:] != s_tile[:-1]).astype(i32)])
    return dict(nrows=nrows, blk_exp=blk_exp, nvalid=nvalid,
                gather=(g_blk, g_tile, g_exp, g_delta, g_first),
                combine=(s_blk, s_tile, s_exp, s_delta, s_first))


def _moe_gather_kernel(blk_ref, tile_ref, exp_ref, delta_ref, first_ref, h_ref, rrow_ref, combt_ref,
                       o_ref):
    g = pl.program_id(0)
    e = exp_ref[g]
    bm, ts = o_ref.shape[0], h_ref.shape[0]
    pos = rrow_ref[pl.ds(e, 1), :] + delta_ref[g].astype(F32)
    live = combt_ref[pl.ds(e, 1), :] > 0.0
    slot = lax.broadcasted_iota(jnp.int32, (bm, ts), 0).astype(F32)
    onehot = jnp.where((pos == slot) & live, 1.0, 0.0).astype(BF16)
    val = jnp.dot(onehot, h_ref[...], preferred_element_type=F32).astype(BF16)

    @pl.when(first_ref[g] == 1)
    def _():
        o_ref[...] = val

    @pl.when(first_ref[g] == 0)
    def _():
        o_ref[...] += val


def _moe_gather(plan, h, rrow, combt, ts, bm):
    t, d = h.shape
    ep = combt.shape[0]
    blk, tile, exp, delta, first = plan["gather"]
    grid_spec = pltpu.PrefetchScalarGridSpec(
        num_scalar_prefetch=5,
        grid=(blk.shape[0],),
        in_specs=[pl.BlockSpec((ts, d), lambda g, b_, t_, e_, d_, f_: (t_[g], 0)),
                  pl.BlockSpec((ep, ts), lambda g, b_, t_, e_, d_, f_: (0, t_[g])),
                  pl.BlockSpec((ep, ts), lambda g, b_, t_, e_, d_, f_: (0, t_[g]))],
        out_specs=pl.BlockSpec((bm, d), lambda g, b_, t_, e_, d_, f_: (b_[g], 0)),
    )
    return pl.pallas_call(
        _moe_gather_kernel,
        grid_spec=grid_spec,
        out_shape=jax.ShapeDtypeStruct((plan["nrows"] + MOE_BMF, d), BF16),
        compiler_params=_params("arbitrary"),
    )(blk, tile, exp, delta, first, h, rrow, combt)


def _moe_ffn_kernel(bexp_ref, nv_ref, x_ref, wg_ref, wu_ref, wd_ref, o_ref, acc_ref):
    i, f = pl.program_id(0), pl.program_id(1)
    valid = i < nv_ref[0]

    @pl.when(valid)
    def _():
        @pl.when(f == 0)
        def _():
            acc_ref[...] = jnp.zeros(acc_ref.shape, F32)

        for r, y in _swiglu_rows(x_ref, wg_ref, wu_ref, wd_ref):
            acc_ref[r, :] += y

    @pl.when(f == pl.num_programs(1) - 1)
    def _():
        @pl.when(valid)
        def _():
            o_ref[...] = acc_ref[...].astype(BF16)

        @pl.when(jnp.logical_not(valid))
        def _():
            o_ref[...] = jnp.zeros(o_ref.shape, BF16)


def _moe_ffn(plan, xs, wg, wu, wd, li, bmf, tf):
    d = xs.shape[1]
    ff = wg.shape[3]
    nblk = plan["blk_exp"].shape[0]
    grid_spec = pltpu.PrefetchScalarGridSpec(
        num_scalar_prefetch=2,
        grid=(nblk, ff // tf),
        in_specs=[pl.BlockSpec((bmf, d), lambda i, f, be, nv: (i, 0)),
                  pl.BlockSpec((None, None, d, tf), lambda i, f, be, nv: (li, be[i], 0, f)),
                  pl.BlockSpec((None, None, d, tf), lambda i, f, be, nv: (li, be[i], 0, f)),
                  pl.BlockSpec((None, None, tf, d), lambda i, f, be, nv: (li, be[i], f, 0))],
        out_specs=pl.BlockSpec((bmf, d), lambda i, f, be, nv: (i, 0)),
        scratch_shapes=[pltpu.VMEM((bmf, d), F32)],
    )
    return pl.pallas_call(
        _moe_ffn_kernel,
        grid_spec=grid_spec,
        out_shape=jax.ShapeDtypeStruct((nblk * bmf, d), BF16),
        compiler_params=_params("parallel", "arbitrary"),
    )(plan["blk_exp"], plan["nvalid"], xs, wg, wu, wd)


def _moe_combine_kernel(blk_ref, tile_ref, exp_ref, delta_ref, first_ref, y_ref, x_ref, comb_ref,
                        rcol_ref, o_ref):
    g = pl.program_id(0)
    e = exp_ref[g]
    ts, bm = x_ref.shape[0], y_ref.shape[0]

    @pl.when(first_ref[g] == 1)
    def _():
        o_ref[...] = x_ref[...]

    lane = lax.broadcasted_iota(jnp.int32, comb_ref.shape, 1)
    sel = lane == e
    cw = jnp.sum(jnp.where(sel, comb_ref[...], 0.0), axis=-1, keepdims=True)
    pos = (jnp.sum(jnp.where(sel, rcol_ref[...], 0.0), axis=-1, keepdims=True)
           + delta_ref[g].astype(F32))
    slot = lax.broadcasted_iota(jnp.int32, (ts, bm), 1).astype(F32)
    onehot = jnp.where((pos == slot) & (cw > 0.0), 1.0, 0.0).astype(BF16)
    o_ref[...] += cw * jnp.dot(onehot, y_ref[...], preferred_element_type=F32)


def _moe_combine(plan, ys, x, comb, rcol, ts, bm):
    t, d = x.shape
    blk, tile, exp, delta, first = plan["combine"]
    grid_spec = pltpu.PrefetchScalarGridSpec(
        num_scalar_prefetch=5,
        grid=(blk.shape[0],),
        in_specs=[pl.BlockSpec((bm, d), lambda g, b_, t_, e_, d_, f_: (b_[g], 0)),
                  pl.BlockSpec((ts, d), lambda g, b_, t_, e_, d_, f_: (t_[g], 0)),
                  pl.BlockSpec((ts, LANES), lambda g, b_, t_, e_, d_, f_: (t_[g], 0)),
                  pl.BlockSpec((ts, LANES), lambda g, b_, t_, e_, d_, f_: (t_[g], 0))],
        out_specs=pl.BlockSpec((ts, d), lambda g, b_, t_, e_, d_, f_: (t_[g], 0)),
    )
    return pl.pallas_call(
        _moe_combine_kernel,
        grid_spec=grid_spec,
        out_shape=jax.ShapeDtypeStruct((t, d), F32),
        compiler_params=_params("arbitrary"),
    )(blk, tile, exp, delta, first, ys, x, comb, rcol)


def _moe(x, h, comb, combt, rcol, rrow, counts, wg, wu, wd, li, ts, tf, bm=MOE_BM, bmf=MOE_BMF):
    t = x.shape[0]
    plan = _moe_plan(counts, t // ts, wg.shape[1], t, bm, bmf)
    xs = _moe_gather(plan, h, rrow, combt, ts, bm)
    ys = _moe_ffn(plan, xs, wg, wu, wd, li, bmf, tf)
    return _moe_combine(plan, ys, x, comb, rcol, ts, bm)


def kernel(x, norm1_g, w_in, da_q_norm, da_k_norm, da_lambda, da_out_norm, dl_q_norm, dl_k_norm, rw_mu, rw_w0, rw_w2, rw_a0, rw_a2, rw_g2, rw_k_k, rw_k_a, rw_r_k, rw_ln_g, rw_ln_b, w_out, norm2_g, ffn_w_gate, ffn_w_up, ffn_w_down, moe_router, moe_w_gate, moe_w_up, moe_w_down):
    b, s, d = x.shape
    depth = w_in.shape[0]
    t = b * s
    n_experts = moe_router.shape[-1]
    xt = x.reshape(t, d)

    qa, ka, va = 0, DA_HEADS, 2 * DA_HEADS
    qb, kb, vb = 3 * DA_HEADS, 3 * DA_HEADS + DL_HEADS, 3 * DA_HEADS + 2 * DL_HEADS
    rw = 3 * DA_HEADS + 3 * DL_HEADS

    tm = min(1024, t)
    tq = min(256, s)
    tf = 512
    assert ffn_w_gate.shape[2] % tf == 0 and moe_w_gate.shape[3] % tf == 0

    w_in, w_out = w_in.astype(BF16), w_out.astype(BF16)
    ffn_w = [w.astype(BF16) for w in (ffn_w_gate, ffn_w_up, ffn_w_down)]
    moe_w = [w.astype(BF16) for w in (moe_w_gate, moe_w_up, moe_w_down)]

    for l in range(depth):
        p = _inproj(xt, norm1_g[l], w_in, l, tm, 1280)

        lam_init = 0.8 - 0.6 * math.exp(-0.3 * l)
        oa = _diffattn(p, qa, ka, va, da_q_norm[l], da_k_norm[l], da_lambda[l], da_out_norm[l],
                       b, s, lam_init, tq)
        oc = _dilattn(p, qb, kb, vb, dl_q_norm[l], dl_k_norm[l], b, s, tq)

        r, lw, k2, v, na, bb, g = _rwprep(p, rw, s, rw_mu[l], rw_w0[l], rw_a0[l], rw_k_k[l],
                                          rw_k_a[l], rw_w2[l], rw_a2[l], rw_g2[l], min(256, s))
        y = _rwkv(r, lw, k2, v, na, bb, b, s, min(512, s), 12)
        ob = _rwpost(y, r, k2, v, g, rw_ln_g[l], rw_ln_b[l], rw_r_k[l].reshape(-1), min(512, s))

        xt = _outproj(xt, oa, oc, ob, w_out, l, tm, 1024)

        i = l // 2
        if l % 2 == 0:
            xt = _ffn(xt, norm2_g[l], *ffn_w, i, min(512, t), tf)
        else:
            ts = min(1024, t)
            h, comb, combt, rcol, rrow, cnt = _router(xt, norm2_g[l], moe_router[i], ts)
            counts = cnt[:, 0, :n_experts].reshape(-1)
            xt = _moe(xt, h, comb, combt, rcol, rrow, counts, *moe_w, i, ts, tf)
    return xt.reshape(b, s, d)
```

```python
import functools
import math

import numpy as np
import jax
import jax.numpy as jnp
from jax import lax
from jax.experimental import pallas as pl
from jax.experimental.pallas import tpu as pltpu

F32 = jnp.float32
BF16 = jnp.bfloat16

LANES = 128
VMEM_LIMIT = 56 * 1024 * 1024

NEG = -1e30
ROPE_THETA = 10000.0
NORM_EPS = 1e-6
RW_LN_EPS = 64e-5
DL_PATTERNS = ((128, 1), (512, 4), (2048, 16))
TOP_K = 2

DA_HEADS, DA_QK = 4, 64
DL_HEADS = 6
RW_HEADS, RW_DIM = 12, 64
DA_W, DL_W, RW_W = 512, 768, 768
RW_PAIRS = RW_W // LANES
CHUNK = 64


def _params(*sem):
    return pltpu.CompilerParams(dimension_semantics=sem, vmem_limit_bytes=VMEM_LIMIT)


def _dot(a, b):
    return jnp.dot(a.astype(BF16), b.astype(BF16), preferred_element_type=F32)


def _dot_nt(a, b):
    return lax.dot_general(a.astype(BF16), b.astype(BF16), (((1,), (1,)), ((), ())),
                           preferred_element_type=F32)


def _dot_tn(a, b):
    return lax.dot_general(a.astype(BF16), b.astype(BF16), (((0,), (0,)), ((), ())),
                           preferred_element_type=F32)


def _split(x):
    hi = x.astype(BF16)
    lo = (x - hi.astype(F32)).astype(BF16)
    return hi, lo


def _dot3(a, b, b_parts=None):
    ah, al = _split(a)
    bh, bl = _split(b) if b_parts is None else b_parts
    return _dot(ah, bh) + _dot(ah, bl) + _dot(al, bh)


def _dot_exact_rhs(a, b_bf16):
    ah, al = _split(a)
    return _dot(ah, b_bf16) + _dot(al, b_bf16)


def _group_ones(width, group):
    i = lax.broadcasted_iota(jnp.int32, (width, width), 0) // group
    j = lax.broadcasted_iota(jnp.int32, (width, width), 1) // group
    return (i == j).astype(BF16)


def _rms(x, g):
    return x * lax.rsqrt(jnp.mean(x * x, axis=-1, keepdims=True) + NORM_EPS) * g


def _cat_blocks(ref, n):
    return jnp.concatenate([ref[c] for c in range(n)], axis=1)


def _put_blocks(ref, val):
    for c in range(ref.shape[0]):
        ref[c] = val[:, c * LANES:(c + 1) * LANES].astype(ref.dtype)


def _inproj_kernel(x_ref, g_ref, w_ref, o_ref, xn_ref):
    @pl.when(pl.program_id(1) == 0)
    def _():
        xn_ref[...] = _rms(x_ref[...], g_ref[...]).astype(BF16)

    _put_blocks(o_ref, jnp.dot(xn_ref[...], w_ref[...], preferred_element_type=F32))


def _col_tiles(w, tn):
    *lead, k, n = w.shape
    nl = len(lead)
    return w.reshape(*lead, k, n // tn, tn).transpose(*range(nl), nl + 1, nl, nl + 2)


def _inproj(x, g, w, li, tm):
    t, d = x.shape
    tn = w.shape[3]
    n = w.shape[1] * tn
    return pl.pallas_call(
        _inproj_kernel,
        grid=(t // tm, n // tn),
        in_specs=[pl.BlockSpec((tm, d), lambda i, j: (i, 0)),
                  pl.BlockSpec((1, d), lambda i, j: (0, 0)),
                  pl.BlockSpec((None, None, d, tn), lambda i, j: (li, j, 0, 0))],
        out_specs=pl.BlockSpec((tn // LANES, tm, LANES), lambda i, j: (j, i, 0)),
        out_shape=jax.ShapeDtypeStruct((n // LANES, t, LANES), F32),
        scratch_shapes=[pltpu.VMEM((tm, d), BF16)],
        compiler_params=_params("parallel", "arbitrary"),
    )(x, g.reshape(1, d), w)


LOG2E = math.log2(math.e)


def _prep_qk(x, gain, cos, sin, group, post=1.0):
    ones = _group_ones(LANES, group)
    ms = _dot_exact_rhs(x * x, ones) * (1.0 / group)
    y = x * lax.rsqrt(ms + NORM_EPS) * gain
    half = group // 2
    if group == LANES:
        partner = pltpu.roll(y, half, axis=1)
    else:
        lane = lax.broadcasted_iota(jnp.int32, y.shape, 1)
        fwd = pltpu.roll(y, LANES - half, axis=1)
        bwd = pltpu.roll(y, half, axis=1)
        partner = jnp.where((lane % group) < half, fwd, bwd)
    return ((y * cos + partner * sin) * post).astype(BF16)


def _rope_tables(s, group):
    half = group // 2
    lane = np.arange(LANES)
    inv = ROPE_THETA ** (-jnp.asarray(lane % half, F32) / half)
    ang = jnp.arange(s, dtype=F32)[:, None] * inv[None, :]
    sign = jnp.asarray(np.where((lane % group) < half, -1.0, 1.0), F32)
    return jnp.cos(ang), jnp.sin(ang) * sign[None, :]


def _tile_gain(gain, group):
    return jnp.tile(gain.astype(F32), LANES // group).reshape(1, LANES)


def _block_pairs(nq):
    return [(j, qi) for j in range(nq) for qi in range(j, nq)]


def _vt_blocks(v_ref, tq, nq):
    return [v_ref[j * tq:(j + 1) * tq, :].T.astype(BF16) for j in range(nq)]


def _diffattn_kernel(q_ref, k_ref, v_ref, cos_ref, sin_ref, gq_ref, gk_ref, lam_ref, go_ref, o_ref,
                     s0_ref, s1_ref, *, tq, scale, lam_init):
    nq = q_ref.shape[0] // tq
    cos, sin = cos_ref[...], sin_ref[...]
    kp = _prep_qk(k_ref[...], gk_ref[...], cos, sin, DA_QK)
    qp = _prep_qk(q_ref[...], gq_ref[...], cos, sin, DA_QK, scale * LOG2E)
    lane = lax.broadcasted_iota(jnp.int32, qp.shape, 1)
    zero = jnp.zeros_like(qp)
    qs = (jnp.where(lane < DA_QK, qp, zero), jnp.where(lane >= DA_QK, qp, zero))
    vt = _vt_blocks(v_ref, tq, nq)
    blk = lambda x, i: x[i * tq:(i + 1) * tq, :]
    causal = (lax.broadcasted_iota(jnp.int32, (tq, tq), 1)
              >= lax.broadcasted_iota(jnp.int32, (tq, tq), 0))
    pairs = _block_pairs(nq)
    s_refs = (s0_ref, s1_ref)

    m = [[jnp.full((1, tq), NEG, F32) for _ in range(nq)] for _ in range(2)]
    for idx, (j, qi) in enumerate(pairs):
        for c in range(2):
            sc = _dot_nt(blk(kp, j), blk(qs[c], qi))
            if j == qi:
                sc = jnp.where(causal, sc, NEG)
            s_refs[c][idx] = sc
            m[c][qi] = jnp.maximum(m[c][qi], jnp.max(sc, axis=0, keepdims=True))

    l = [[jnp.zeros((1, tq), F32) for _ in range(nq)] for _ in range(2)]
    acc = [[None] * nq for _ in range(2)]
    for idx, (j, qi) in enumerate(pairs):
        for c in range(2):
            pr = jnp.exp2(s_refs[c][idx] - m[c][qi])
            l[c][qi] = l[c][qi] + jnp.sum(pr, axis=0, keepdims=True)
            d = _dot(vt[j], pr)
            acc[c][qi] = d if acc[c][qi] is None else acc[c][qi] + d

    lm = lam_ref[...]
    lam = (jnp.exp(jnp.sum(lm[0:1] * lm[1:2], axis=-1, keepdims=True))
           - jnp.exp(jnp.sum(lm[2:3] * lm[3:4], axis=-1, keepdims=True)) + lam_init)
    for qi in range(nq):
        ot = acc[0][qi] / l[0][qi] - lam * (acc[1][qi] / l[1][qi])
        ot = ot * lax.rsqrt(jnp.mean(ot * ot, axis=0, keepdims=True) + NORM_EPS)
        o_ref[qi * tq:(qi + 1) * tq, :] = (ot.T * go_ref[...] * (1.0 - lam_init)).astype(BF16)


def _diffattn(p, qcb, kcb, vcb, gq, gk, lam4, gout, b, s, lam_init, tq):
    t = p.shape[1]
    nq = s // tq
    npair = nq * (nq + 1) // 2
    cos, sin = _rope_tables(s, DA_QK)
    blk = lambda cb0: pl.BlockSpec((None, s, LANES), lambda bi, h: (cb0 + h, bi, 0))
    full = lambda shape: pl.BlockSpec(shape, lambda bi, h: (0, 0))
    return pl.pallas_call(
        functools.partial(_diffattn_kernel, tq=tq, scale=DA_QK ** -0.5, lam_init=lam_init),
        grid=(b, DA_HEADS),
        in_specs=[blk(qcb), blk(kcb), blk(vcb),
                  full((s, LANES)), full((s, LANES)), full((1, LANES)), full((1, LANES)),
                  full((4, DA_QK)), full((1, LANES))],
        out_specs=blk(0),
        out_shape=jax.ShapeDtypeStruct((DA_HEADS, t, LANES), BF16),
        scratch_shapes=[pltpu.VMEM((npair, tq, tq), F32), pltpu.VMEM((npair, tq, tq), F32)],
        compiler_params=_params("parallel", "parallel"),
    )(p, p, p, cos, sin, _tile_gain(gq, DA_QK), _tile_gain(gk, DA_QK), lam4.astype(F32),
      gout.astype(F32).reshape(1, LANES))


def _dilated_bias(s, tq):
    nd = s // tq
    d = (np.arange(nd)[:, None, None] * tq + np.arange(tq)[None, None, :]
         - np.arange(tq)[None, :, None])
    cnt = np.zeros(d.shape, np.float64)
    for window, dil in DL_PATTERNS:
        cnt += (d >= 0) & (d % dil == 0) & (d <= window)
    bias = np.where(cnt > 0, np.log2(np.maximum(cnt, 1.0)), NEG)
    return jnp.asarray(bias, F32)


def _dilattn_kernel(q_ref, k_ref, v_ref, cos_ref, sin_ref, gq_ref, gk_ref, bias_ref, o_ref, s_ref,
                    *, tq, scale):
    nq = q_ref.shape[0] // tq
    cos, sin = cos_ref[...], sin_ref[...]
    kp = _prep_qk(k_ref[...], gk_ref[...], cos, sin, LANES)
    qp = _prep_qk(q_ref[...], gq_ref[...], cos, sin, LANES, scale * LOG2E)
    vt = _vt_blocks(v_ref, tq, nq)
    blk = lambda x, i: x[i * tq:(i + 1) * tq, :]
    pairs = _block_pairs(nq)

    m = [jnp.full((1, tq), NEG, F32) for _ in range(nq)]
    for idx, (j, qi) in enumerate(pairs):
        sc = _dot_nt(blk(kp, j), blk(qp, qi)) + bias_ref[qi - j]
        s_ref[idx] = sc
        m[qi] = jnp.maximum(m[qi], jnp.max(sc, axis=0, keepdims=True))

    l = [jnp.zeros((1, tq), F32) for _ in range(nq)]
    acc = [None] * nq
    for idx, (j, qi) in enumerate(pairs):
        pr = jnp.exp2(s_ref[idx] - m[qi])
        l[qi] = l[qi] + jnp.sum(pr, axis=0, keepdims=True)
        d = _dot(vt[j], pr)
        acc[qi] = d if acc[qi] is None else acc[qi] + d

    for qi in range(nq):
        o_ref[qi * tq:(qi + 1) * tq, :] = (acc[qi] / l[qi]).T.astype(BF16)


def _dilattn(p, qcb, kcb, vcb, gq, gk, b, s, tq):
    t = p.shape[1]
    nq = s // tq
    cos, sin = _rope_tables(s, LANES)
    bias = _dilated_bias(s, tq)
    blk = lambda cb0: pl.BlockSpec((None, s, LANES), lambda bi, h: (cb0 + h, bi, 0))
    full = lambda shape: pl.BlockSpec(shape, lambda bi, h: (0,) * len(shape))
    return pl.pallas_call(
        functools.partial(_dilattn_kernel, tq=tq, scale=LANES ** -0.5),
        grid=(b, DL_HEADS),
        in_specs=[blk(qcb), blk(kcb), blk(vcb),
                  full((s, LANES)), full((s, LANES)), full((1, LANES)), full((1, LANES)),
                  full((nq, tq, tq))],
        out_specs=blk(0),
        out_shape=jax.ShapeDtypeStruct((DL_HEADS, t, LANES), BF16),
        scratch_shapes=[pltpu.VMEM((nq * (nq + 1) // 2, tq, tq), F32)],
        compiler_params=_params("parallel", "parallel"),
    )(p, p, p, cos, sin, _tile_gain(gq, LANES), _tile_gain(gk, LANES), bias)


def _rwprep_kernel(r_ref, k_ref, v_ref, x_ref, rp_ref, kp_ref, vp_ref, xp_ref,
                   mu_ref, w0_ref, a0_ref, kk_ref, ka_ref, w2_ref, a2_ref, g2_ref,
                   ro, lwo, ko, vo, nao, bo, go, *, rows_per_seq):
    i = pl.program_id(0)
    tm = r_ref.shape[1]
    first = (i * tm) % rows_per_seq == 0
    row = lax.broadcasted_iota(jnp.int32, (tm, 1), 0)

    def shifted(cur_ref, prev_ref, mu):
        n = cur_ref.shape[0]
        cur = _cat_blocks(cur_ref, n)
        last = jnp.concatenate([prev_ref[c, 7:8, :] for c in range(n)], axis=1)
        last = jnp.where(first, 0.0, last)
        prev = jnp.where(row == 0, last, pltpu.roll(cur, 1, axis=0))
        return cur + mu * (prev - cur)

    mu = mu_ref[...]
    rr = shifted(r_ref, rp_ref, mu[:, 0:RW_W])
    kr = shifted(k_ref, kp_ref, mu[:, RW_W:2 * RW_W])
    vr = shifted(v_ref, vp_ref, mu[:, 2 * RW_W:3 * RW_W])
    xs = shifted(x_ref, xp_ref, mu[:, 3 * RW_W:])

    z = w0_ref[...] + _dot3(jnp.tanh(xs), None, (w2_ref[0], w2_ref[1]))
    nz = -z
    softplus = jnp.maximum(nz, 0.0) + jnp.log(1.0 + jnp.exp(-jnp.abs(nz)))
    w_log = -softplus - 0.5
    a = jax.nn.sigmoid(a0_ref[...] + _dot3(xs, None, (a2_ref[0], a2_ref[1])))
    g = _dot3(jax.nn.sigmoid(xs), None, (g2_ref[0], g2_ref[1]))

    ones = _group_ones(LANES, RW_DIM)
    kkr = kr * kk_ref[...]
    sq = kkr * kkr
    ss = jnp.concatenate([_dot_exact_rhs(sq[:, c * LANES:(c + 1) * LANES], ones)
                          for c in range(RW_PAIRS)], axis=1)
    kk = kkr / jnp.maximum(jnp.sqrt(ss), 1e-12)
    _put_blocks(ro, rr)
    _put_blocks(lwo, -jnp.exp(w_log))
    _put_blocks(ko, kr * (1.0 + (a - 1.0) * ka_ref[...]))
    _put_blocks(vo, vr)
    _put_blocks(nao, -kk)
    _put_blocks(bo, kk * a)
    _put_blocks(go, g)


def _rwprep(p, cb0, s, mu, w0, a0, k_k, k_a, w2, a2, g2, tm):
    t = p.shape[1]
    lr = mu.shape[0] - 3 * RW_W
    nx = lr // LANES
    hilo = lambda w_: jnp.stack(_split(w_))
    w2p = hilo(jnp.zeros((lr, RW_W), F32).at[0:w2.shape[0]].set(w2))
    a2p = hilo(jnp.zeros((lr, RW_W), F32).at[w2.shape[0]:w2.shape[0] + a2.shape[0]].set(a2))
    g2p = hilo(jnp.zeros((lr, RW_W), F32).at[lr - g2.shape[0]:].set(g2))
    r8 = tm // 8
    gb = cb0 // RW_PAIRS
    xb = (cb0 + 3 * RW_PAIRS) // nx

    def cur(c, n):
        return pl.BlockSpec((n, tm, LANES), lambda i: (c, i, 0))

    def prev(c, n):
        return pl.BlockSpec((n, 8, LANES), lambda i: (c, jnp.maximum(i * r8 - 1, 0), 0))

    def full(shape):
        return pl.BlockSpec(shape, lambda i: (0,) * len(shape))

    vec = lambda a_: a_.astype(F32).reshape(1, -1)
    out = jax.ShapeDtypeStruct((RW_PAIRS, t, LANES), F32)
    return pl.pallas_call(
        functools.partial(_rwprep_kernel, rows_per_seq=s),
        grid=(t // tm,),
        in_specs=[cur(gb, RW_PAIRS), cur(gb + 1, RW_PAIRS), cur(gb + 2, RW_PAIRS), cur(xb, nx),
                  prev(gb, RW_PAIRS), prev(gb + 1, RW_PAIRS), prev(gb + 2, RW_PAIRS), prev(xb, nx),
                  full((1, 3 * RW_W + lr)), full((1, RW_W)), full((1, RW_W)), full((1, RW_W)),
                  full((1, RW_W)), full((2, lr, RW_W)), full((2, lr, RW_W)), full((2, lr, RW_W))],
        out_specs=[pl.BlockSpec((RW_PAIRS, tm, LANES), lambda i: (0, i, 0))] * 7,
        out_shape=[out] * 7,
        compiler_params=_params("parallel"),
    )(p, p, p, p, p, p, p, p, vec(mu), vec(w0), vec(a0), vec(k_k), vec(k_a), w2p, a2p, g2p)


def _rwkv_kernel(r_ref, lw_ref, k_ref, v_ref, a_ref, b_ref, y_ref, rh_ref, yh_ref, p_ref, q_ref,
                 st_ref, *, nchunk, unroll):
    L = CHUNK
    W = 2 * L
    npair = r_ref.shape[0]

    @pl.when(pl.program_id(1) == 0)
    def _():
        st_ref[...] = jnp.zeros(st_ref.shape, F32)

    lane = lax.broadcasted_iota(jnp.int32, (L, W), 1)
    rowi = lax.broadcasted_iota(jnp.int32, (L, W), 0)
    strict = (lane % L) < rowi
    incl = (lane % L) <= rowi
    rr = lax.broadcasted_iota(jnp.int32, (W, W), 0)
    cc = lax.broadcasted_iota(jnp.int32, (W, W), 1)
    same = (rr // L) == (cc // L)
    eye = rr == cc
    tl = lax.broadcasted_iota(jnp.int32, (L, L), 0)
    sl = lax.broadcasted_iota(jnp.int32, (L, L), 1)
    tril = (sl <= tl).astype(BF16)

    def bd(x):
        return jnp.where(same, jnp.concatenate([x, x], axis=0), 0.0)

    def group(gi, carry):
        us = [gi * unroll + i for i in range(unroll)]
        prs = [u // nchunk for u in us]
        rws = [pl.ds(pl.multiple_of((u % nchunk) * L, L), L) for u in us]
        G = range(unroll)
        ld = lambda ref: [ref[prs[i], rws[i], :] for i in G]
        r, lw, k, v, a, b = ld(r_ref), ld(lw_ref), ld(k_ref), ld(v_ref), ld(a_ref), ld(b_ref)

        def csum(x):
            hi, lo = _split(x)
            rest = x - hi.astype(F32) - lo.astype(F32)
            return jnp.concatenate([hi, lo, rest.astype(BF16)], axis=1)

        c3 = [_dot(tril, csum(lw[i])) for i in G]
        cin = [c[:, 0:W] + c[:, W:2 * W] + c[:, 2 * W:] for c in c3]
        clast = [c[L - 1:L, :] for c in cin]
        g_inv = [jnp.exp(-c) for c in cin]
        g_tail = [jnp.exp(clast[i] - cin[i]) for i in G]
        at = [a[i] * jnp.exp(cin[i] - lw[i]) for i in G]
        rt = [r[i] * jnp.exp(cin[i]) for i in G]
        abk = [_dot_nt(jnp.concatenate([at[i], rt[i]], axis=0),
                       jnp.concatenate([bd(b[i] * g_inv[i]), bd(k[i] * g_inv[i])], axis=0)) for i in G]
        a_ab = [jnp.where(strict, m[0:L, 0:W], 0.0) for m in abk]
        a_rb = [jnp.where(incl, m[L:W, 0:W], 0.0) for m in abk]
        a_ak = [jnp.where(strict, m[0:L, W:], 0.0) for m in abk]
        a_rk = [jnp.where(incl, m[L:W, W:], 0.0) for m in abk]
        n = [bd(m) for m in a_ab]
        tm = [jnp.where(eye, 1.0, m) for m in n]
        x = [_dot(m, m) for m in n]
        for j in range(5):
            if j < 4:
                xx = [_dot(x[i], jnp.concatenate([x[i], tm[i]], axis=1)) for i in G]
                x = [m[:, 0:W] for m in xx]
                tm = [tm[i] + xx[i][:, W:] for i in G]
            else:
                tm = [tm[i] + _dot(x[i], tm[i]) for i in G]
        v_bd = [bd(m) for m in v]
        kv = [_dot(jnp.concatenate([bd(a_ak[i]), a_rk[i]], axis=0), v_bd[i]) for i in G]
        au = [_dot(tm[i], jnp.concatenate([bd(at[i]), kv[i][0:W]], axis=1)) for i in G]
        ry = [_dot(a_rb[i], au[i]) for i in G]
        zero = jnp.zeros((W, W), F32)
        pq = [_dot_tn(jnp.concatenate([bd(b[i] * g_tail[i]), bd(k[i] * g_tail[i])], axis=0),
                      jnp.concatenate([au[i], jnp.concatenate([zero, v_bd[i]], axis=1)], axis=0))
              for i in G]
        for i in G:
            rh_ref[prs[i], rws[i], :] = rt[i] + ry[i][:, 0:W]
            yh_ref[prs[i], rws[i], :] = ry[i][:, W:] + kv[i][W:]
            p_ref[us[i]] = jnp.where(eye, jnp.exp(clast[i]), 0.0) + pq[i][:, 0:W]
            q_ref[us[i]] = pq[i][:, W:]
        return carry

    lax.fori_loop(0, npair * nchunk // unroll, group, 0)

    def step(c, carry):
        rows = pl.ds(pl.multiple_of(c * L, L), L)
        for pr in range(npair):
            st = st_ref[pr].astype(BF16)
            y_ref[pr, rows, :] = _dot(rh_ref[pr, rows, :], st) + yh_ref[pr, rows, :]
            st_ref[pr] = _dot(p_ref[pr * nchunk + c], st) + q_ref[pr * nchunk + c]
        return carry

    lax.fori_loop(0, nchunk, step, 0)


def _rwkv(r, lw, k, v, na, bb, b, s, sb, unroll):
    npair, t, w = r.shape
    nchunk = sb // CHUNK
    nsb = s // sb
    spec = pl.BlockSpec((npair, sb, w), lambda bi, si: (0, bi * nsb + si, 0))
    return pl.pallas_call(
        functools.partial(_rwkv_kernel, nchunk=nchunk, unroll=unroll),
        grid=(b, nsb),
        in_specs=[spec] * 6,
        out_specs=spec,
        out_shape=jax.ShapeDtypeStruct((npair, t, w), F32),
        scratch_shapes=[pltpu.VMEM((npair, sb, w), F32), pltpu.VMEM((npair, sb, w), F32),
                        pltpu.VMEM((npair * nchunk, w, w), F32), pltpu.VMEM((npair * nchunk, w, w), F32),
                        pltpu.VMEM((npair, w, w), F32)],
        compiler_params=_params("parallel", "arbitrary"),
    )(r, lw, k, v, na, bb)


def _rwpost_kernel(y_ref, r_ref, k_ref, v_ref, g_ref, lng_ref, lnb_ref, rk_ref, o_ref):
    ones = _group_ones(LANES, RW_DIM)
    for c in range(RW_PAIRS):
        cols = slice(c * LANES, (c + 1) * LANES)
        y = y_ref[c]
        mu = _dot_exact_rhs(y, ones) * (1.0 / RW_DIM)
        yc = y - mu
        var = _dot_exact_rhs(yc * yc, ones) * (1.0 / RW_DIM)
        out = yc * lax.rsqrt(var + RW_LN_EPS) * lng_ref[:, cols] + lnb_ref[:, cols]
        bonus = _dot_exact_rhs(r_ref[c] * k_ref[c] * rk_ref[:, cols], ones)
        o_ref[c] = ((out + bonus * v_ref[c]) * g_ref[c]).astype(o_ref.dtype)


def _rwpost(y, r, k, v, g, ln_g, ln_b, r_k, tm):
    t = y.shape[1]
    big = pl.BlockSpec((RW_PAIRS, tm, LANES), lambda i: (0, i, 0))
    small = pl.BlockSpec((1, RW_W), lambda i: (0, 0))
    vec = lambda a_: a_.astype(F32).reshape(1, RW_W)
    return pl.pallas_call(
        _rwpost_kernel,
        grid=(t // tm,),
        in_specs=[big] * 5 + [small] * 3,
        out_specs=big,
        out_shape=jax.ShapeDtypeStruct((RW_PAIRS, t, LANES), BF16),
        compiler_params=_params("parallel"),
    )(y, r, k, v, g, vec(ln_g), vec(ln_b), vec(r_k))


def _outproj_kernel(x_ref, oa_ref, oc_ref, ob_ref, w_ref, o_ref):
    mix = jnp.concatenate([_cat_blocks(oa_ref, oa_ref.shape[0]), _cat_blocks(oc_ref, oc_ref.shape[0]),
                           _cat_blocks(ob_ref, ob_ref.shape[0])], axis=1)
    o_ref[...] = x_ref[...] + jnp.dot(mix, w_ref[...], preferred_element_type=F32)


def _outproj(x, oa, oc, ob, w, li, tm):
    t, d = x.shape
    tn = w.shape[3]
    blocks = lambda a_: pl.BlockSpec((a_.shape[0], tm, LANES), lambda i, j: (0, i, 0))
    return pl.pallas_call(
        _outproj_kernel,
        grid=(t // tm, d // tn),
        in_specs=[pl.BlockSpec((tm, tn), lambda i, j: (i, j)),
                  blocks(oa), blocks(oc), blocks(ob),
                  pl.BlockSpec((None, None, w.shape[2], tn), lambda i, j: (li, j, 0, 0))],
        out_specs=pl.BlockSpec((tm, tn), lambda i, j: (i, j)),
        out_shape=jax.ShapeDtypeStruct((t, d), F32),
        compiler_params=_params("parallel", "arbitrary"),
    )(x, oa, oc, ob, w)


FFN_SPLIT = 2


def _swiglu_rows(x_ref, wg_ref, wu_ref, wd_ref):
    n = x_ref.shape[0] // FFN_SPLIT
    sl = [slice(i * n, (i + 1) * n) for i in range(FFN_SPLIT)]
    wg, wu, wd = wg_ref[...], wu_ref[...], wd_ref[...]
    gu = [(jnp.dot(x_ref[r, :], wg, preferred_element_type=F32),
           jnp.dot(x_ref[r, :], wu, preferred_element_type=F32)) for r in sl]
    act = [(g * jax.nn.sigmoid(g) * u).astype(BF16) for g, u in gu]
    return [(r, jnp.dot(a, wd, preferred_element_type=F32)) for r, a in zip(sl, act)]


def _ffn_kernel(x_ref, g_ref, wg_ref, wu_ref, wd_ref, o_ref, xn_ref, acc_ref):
    f = pl.program_id(1)

    @pl.when(f == 0)
    def _():
        xn_ref[...] = _rms(x_ref[...], g_ref[...]).astype(BF16)
        acc_ref[...] = jnp.zeros(acc_ref.shape, F32)

    for r, y in _swiglu_rows(xn_ref, wg_ref, wu_ref, wd_ref):
        acc_ref[r, :] += y

    @pl.when(f == pl.num_programs(1) - 1)
    def _():
        o_ref[...] = x_ref[...] + acc_ref[...]


def _ffn(x, g, wg, wu, wd, li, tm):
    t, d = x.shape
    tf = wg.shape[3]
    ff = wd.shape[1]
    return pl.pallas_call(
        _ffn_kernel,
        grid=(t // tm, ff // tf),
        in_specs=[pl.BlockSpec((tm, d), lambda i, f: (i, 0)),
                  pl.BlockSpec((1, d), lambda i, f: (0, 0)),
                  pl.BlockSpec((None, None, d, tf), lambda i, f: (li, f, 0, 0)),
                  pl.BlockSpec((None, None, d, tf), lambda i, f: (li, f, 0, 0)),
                  pl.BlockSpec((None, tf, d), lambda i, f: (li, f, 0))],
        out_specs=pl.BlockSpec((tm, d), lambda i, f: (i, 0)),
        out_shape=jax.ShapeDtypeStruct((t, d), F32),
        scratch_shapes=[pltpu.VMEM((tm, d), BF16), pltpu.VMEM((tm, d), F32)],
        compiler_params=_params("parallel", "arbitrary"),
    )(x, g.reshape(1, d), wg, wu, wd)


def _router_kernel(x_ref, g_ref, wr_ref, h_ref, comb_ref, combt_ref, rcol_ref, rrow_ref, cnt_ref,
                   *, n_experts):
    h = _rms(x_ref[...], g_ref[...])
    h_ref[...] = h.astype(BF16)
    logits = _dot3(h, wr_ref[...])
    lane = lax.broadcasted_iota(jnp.int32, logits.shape, 1)
    lg = jnp.where(lane < n_experts, logits, NEG)
    m1 = jnp.max(lg, axis=-1, keepdims=True)
    i1 = jnp.min(jnp.where(lg == m1, lane, LANES), axis=-1, keepdims=True)
    lg2 = jnp.where(lane == i1, NEG, lg)
    m2 = jnp.max(lg2, axis=-1, keepdims=True)
    i2 = jnp.min(jnp.where(lg2 == m2, lane, LANES), axis=-1, keepdims=True)
    e2 = jnp.exp(m2 - m1)
    w1 = 1.0 / (1.0 + e2)
    comb = jnp.where(lane == i1, w1, 0.0) + jnp.where(lane == i2, e2 * w1, 0.0)
    combt = comb.T[0:combt_ref.shape[0], :]
    comb_ref[...] = comb
    combt_ref[...] = combt
    ts = comb.shape[0]
    tt = lax.broadcasted_iota(jnp.int32, (ts, ts), 0)
    uu = lax.broadcasted_iota(jnp.int32, (ts, ts), 1)
    live = jnp.where(comb > 0.0, 1.0, 0.0)
    rcol_ref[...] = _dot((uu < tt).astype(BF16), live)
    rrow_ref[...] = _dot(jnp.where(combt > 0.0, 1.0, 0.0), (tt < uu).astype(BF16))
    cnt_ref[0] = jnp.sum(live, axis=0, keepdims=True).astype(jnp.int32)


def _router(x, g, wr, ts):
    t, d = x.shape
    e = wr.shape[1]
    ep = max(8, e)
    wrp = jnp.zeros((d, LANES), F32).at[:, :e].set(wr)
    nt = t // ts
    return pl.pallas_call(
        functools.partial(_router_kernel, n_experts=e),
        grid=(nt,),
        in_specs=[pl.BlockSpec((ts, d), lambda i: (i, 0)),
                  pl.BlockSpec((1, d), lambda i: (0, 0)),
                  pl.BlockSpec((d, LANES), lambda i: (0, 0))],
        out_specs=[pl.BlockSpec((ts, d), lambda i: (i, 0)),
                   pl.BlockSpec((ts, LANES), lambda i: (i, 0)),
                   pl.BlockSpec((ep, ts), lambda i: (0, i)),
                   pl.BlockSpec((ts, LANES), lambda i: (i, 0)),
                   pl.BlockSpec((ep, ts), lambda i: (0, i)),
                   pl.BlockSpec((1, 1, LANES), lambda i: (i, 0, 0))],
        out_shape=[jax.ShapeDtypeStruct((t, d), BF16),
                   jax.ShapeDtypeStruct((t, LANES), F32),
                   jax.ShapeDtypeStruct((ep, t), F32),
                   jax.ShapeDtypeStruct((t, LANES), F32),
                   jax.ShapeDtypeStruct((ep, t), F32),
                   jax.ShapeDtypeStruct((nt, 1, LANES), jnp.int32)],
        compiler_params=_params("parallel"),
    )(x, g.reshape(1, d), wrp)


MOE_BM = 256
MOE_BMF = 512
NO_MATCH = -(1 << 20)


def _moe_plan(counts, nt, ne, t, bm, bmf):
    i32 = jnp.int32
    cnt = counts.reshape(nt, ne).astype(i32)
    tot = jnp.sum(cnt, axis=0)
    ptot = (tot + bmf - 1) // bmf * bmf
    eend = jnp.cumsum(ptot)
    ebase = eend - ptot
    seg0 = ebase[None, :] + jnp.cumsum(cnt, axis=0) - cnt
    seg1 = seg0 + cnt
    nrows = 2 * t + ne * bmf
    nblk = nrows // bmf
    count_le = lambda ends, v: jnp.sum((ends[None, :] <= v[:, None]).astype(i32), axis=1)
    blk_exp = jnp.minimum(count_le(eend, jnp.arange(nblk, dtype=i32) * bmf), ne - 1)
    nvalid = (eend[-1] // bmf).astype(i32).reshape(1)
    npairs = nrows // bm + nt * ne
    g = jnp.arange(npairs, dtype=i32)

    def pairs(c0, c1, seg_tile, seg_exp, dummy_blk):
        n = jnp.where(c1 > c0, (c1 - 1) // bm - c0 // bm + 1, 0)
        pend = jnp.cumsum(n)
        k = jnp.minimum(count_le(pend, g), c0.shape[0] - 1)
        valid = g < pend[-1]
        blk = c0[k] // bm + (g - (pend[k] - n[k]))
        blk = jnp.where(valid, blk, dummy_blk)
        delta = jnp.where(valid, c0[k] - blk * bm, NO_MATCH)
        return (blk.astype(i32), jnp.where(valid, seg_tile[k], nt - 1).astype(i32),
                seg_exp[k].astype(i32), delta.astype(i32))

    tiles = jnp.arange(nt, dtype=i32)
    exps = jnp.arange(ne, dtype=i32)
    c1g = seg1.at[nt - 1].set(eend)
    g_blk, g_tile, g_exp, g_delta = pairs(seg0.T.reshape(-1), c1g.T.reshape(-1),
                                          jnp.tile(tiles, ne), jnp.repeat(exps, nt), nrows // bm)
    g_first = jnp.concatenate([jnp.ones((1,), i32), (g_blk[1:] != g_blk[:-1]).astype(i32)])
    s_blk, s_tile, s_exp, s_delta = pairs(seg0.reshape(-1), seg1.reshape(-1),
                                          jnp.repeat(tiles, ne), jnp.tile(exps, nt), 0)
    s_first = jnp.concatenate([jnp.ones((1,), i32), (s_tile[1:] != s_tile[:-1]).astype(i32)])
    return dict(nrows=nrows, blk_exp=blk_exp, nvalid=nvalid,
                gather=(g_blk, g_tile, g_exp, g_delta, g_first),
                combine=(s_blk, s_tile, s_exp, s_delta, s_first))


def _moe_gather_kernel(blk_ref, tile_ref, exp_ref, delta_ref, first_ref, h_ref, rrow_ref, combt_ref,
                       o_ref):
    g = pl.program_id(0)
    e = exp_ref[g]
    bm, ts = o_ref.shape[0], h_ref.shape[0]
    pos = rrow_ref[pl.ds(e, 1), :] + delta_ref[g].astype(F32)
    live = combt_ref[pl.ds(e, 1), :] > 0.0
    slot = lax.broadcasted_iota(jnp.int32, (bm, ts), 0).astype(F32)
    onehot = jnp.where((pos == slot) & live, 1.0, 0.0).astype(BF16)
    val = jnp.dot(onehot, h_ref[...], preferred_element_type=F32).astype(BF16)

    @pl.when(first_ref[g] == 1)
    def _():
        o_ref[...] = val

    @pl.when(first_ref[g] == 0)
    def _():
        o_ref[...] += val


def _moe_gather(plan, h, rrow, combt, ts, bm):
    t, d = h.shape
    ep = combt.shape[0]
    blk, tile, exp, delta, first = plan["gather"]
    grid_spec = pltpu.PrefetchScalarGridSpec(
        num_scalar_prefetch=5,
        grid=(blk.shape[0],),
        in_specs=[pl.BlockSpec((ts, d), lambda g, b_, t_, e_, d_, f_: (t_[g], 0)),
                  pl.BlockSpec((ep, ts), lambda g, b_, t_, e_, d_, f_: (0, t_[g])),
                  pl.BlockSpec((ep, ts), lambda g, b_, t_, e_, d_, f_: (0, t_[g]))],
        out_specs=pl.BlockSpec((bm, d), lambda g, b_, t_, e_, d_, f_: (b_[g], 0)),
    )
    return pl.pallas_call(
        _moe_gather_kernel,
        grid_spec=grid_spec,
        out_shape=jax.ShapeDtypeStruct((plan["nrows"] + MOE_BMF, d), BF16),
        compiler_params=_params("arbitrary"),
    )(blk, tile, exp, delta, first, h, rrow, combt)


def _moe_ffn_kernel(bexp_ref, nv_ref, x_ref, wg_ref, wu_ref, wd_ref, o_ref, acc_ref):
    i, f = pl.program_id(0), pl.program_id(1)
    valid = i < nv_ref[0]

    @pl.when(valid)
    def _():
        @pl.when(f == 0)
        def _():
            acc_ref[...] = jnp.zeros(acc_ref.shape, F32)

        for r, y in _swiglu_rows(x_ref, wg_ref, wu_ref, wd_ref):
            acc_ref[r, :] += y

    @pl.when(f == pl.num_programs(1) - 1)
    def _():
        @pl.when(valid)
        def _():
            o_ref[...] = acc_ref[...].astype(BF16)

        @pl.when(jnp.logical_not(valid))
        def _():
            o_ref[...] = jnp.zeros(o_ref.shape, BF16)


def _moe_ffn(plan, xs, wg, wu, wd, li, bmf):
    d = xs.shape[1]
    tf = wg.shape[4]
    ff = wd.shape[2]
    nblk = plan["blk_exp"].shape[0]
    grid_spec = pltpu.PrefetchScalarGridSpec(
        num_scalar_prefetch=2,
        grid=(nblk, ff // tf),
        in_specs=[pl.BlockSpec((bmf, d), lambda i, f, be, nv: (i, 0)),
                  pl.BlockSpec((None, None, None, d, tf), lambda i, f, be, nv: (li, be[i], f, 0, 0)),
                  pl.BlockSpec((None, None, None, d, tf), lambda i, f, be, nv: (li, be[i], f, 0, 0)),
                  pl.BlockSpec((None, None, tf, d), lambda i, f, be, nv: (li, be[i], f, 0))],
        out_specs=pl.BlockSpec((bmf, d), lambda i, f, be, nv: (i, 0)),
        scratch_shapes=[pltpu.VMEM((bmf, d), F32)],
    )
    return pl.pallas_call(
        _moe_ffn_kernel,
        grid_spec=grid_spec,
        out_shape=jax.ShapeDtypeStruct((nblk * bmf, d), BF16),
        compiler_params=_params("parallel", "arbitrary"),
    )(plan["blk_exp"], plan["nvalid"], xs, wg, wu, wd)


def _moe_combine_kernel(blk_ref, tile_ref, exp_ref, delta_ref, first_ref, y_ref, x_ref, comb_ref,
                        rcol_ref, o_ref):
    g = pl.program_id(0)
    e = exp_ref[g]
    ts, bm = x_ref.shape[0], y_ref.shape[0]

    @pl.when(first_ref[g] == 1)
    def _():
        o_ref[...] = x_ref[...]

    lane = lax.broadcasted_iota(jnp.int32, comb_ref.shape, 1)
    sel = lane == e
    cw = jnp.sum(jnp.where(sel, comb_ref[...], 0.0), axis=-1, keepdims=True)
    pos = (jnp.sum(jnp.where(sel, rcol_ref[...], 0.0), axis=-1, keepdims=True)
           + delta_ref[g].astype(F32))
    slot = lax.broadcasted_iota(jnp.int32, (ts, bm), 1).astype(F32)
    onehot = jnp.where((pos == slot) & (cw > 0.0), 1.0, 0.0).astype(BF16)
    o_ref[...] += cw * jnp.dot(onehot, y_ref[...], preferred_element_type=F32)


def _moe_combine(plan, ys, x, comb, rcol, ts, bm):
    t, d = x.shape
    blk, tile, exp, delta, first = plan["combine"]
    grid_spec = pltpu.PrefetchScalarGridSpec(
        num_scalar_prefetch=5,
        grid=(blk.shape[0],),
        in_specs=[pl.BlockSpec((bm, d), lambda g, b_, t_, e_, d_, f_: (b_[g], 0)),
                  pl.BlockSpec((ts, d), lambda g, b_, t_, e_, d_, f_: (t_[g], 0)),
                  pl.BlockSpec((ts, LANES), lambda g, b_, t_, e_, d_, f_: (t_[g], 0)),
                  pl.BlockSpec((ts, LANES), lambda g, b_, t_, e_, d_, f_: (t_[g], 0))],
        out_specs=pl.BlockSpec((ts, d), lambda g, b_, t_, e_, d_, f_: (t_[g], 0)),
    )
    return pl.pallas_call(
        _moe_combine_kernel,
        grid_spec=grid_spec,
        out_shape=jax.ShapeDtypeStruct((t, d), F32),
        compiler_params=_params("arbitrary"),
    )(blk, tile, exp, delta, first, ys, x, comb, rcol)


def _moe(x, h, comb, combt, rcol, rrow, counts, wg, wu, wd, li, ts, bm=MOE_BM, bmf=MOE_BMF):
    t = x.shape[0]
    plan = _moe_plan(counts, t // ts, wg.shape[1], t, bm, bmf)
    xs = _moe_gather(plan, h, rrow, combt, ts, bm)
    ys = _moe_ffn(plan, xs, wg, wu, wd, li, bmf)
    return _moe_combine(plan, ys, x, comb, rcol, ts, bm)


def kernel(x, norm1_g, w_in, da_q_norm, da_k_norm, da_lambda, da_out_norm, dl_q_norm, dl_k_norm, rw_mu, rw_w0, rw_w2, rw_a0, rw_a2, rw_g2, rw_k_k, rw_k_a, rw_r_k, rw_ln_g, rw_ln_b, w_out, norm2_g, ffn_w_gate, ffn_w_up, ffn_w_down, moe_router, moe_w_gate, moe_w_up, moe_w_down):
    b, s, d = x.shape
    depth = w_in.shape[0]
    t = b * s
    n_experts = moe_router.shape[-1]
    xt = x.reshape(t, d)

    qa, ka, va = 0, DA_HEADS, 2 * DA_HEADS
    qb, kb, vb = 3 * DA_HEADS, 3 * DA_HEADS + DL_HEADS, 3 * DA_HEADS + 2 * DL_HEADS
    rw = 3 * DA_HEADS + 3 * DL_HEADS

    tm = min(1024, t)
    tq = min(256, s)
    tf, tf_moe = 512, 1408
    assert ffn_w_gate.shape[2] % tf == 0 and moe_w_gate.shape[3] % tf_moe == 0

    w_in = _col_tiles(w_in.astype(BF16), 1280)
    w_out = _col_tiles(w_out.astype(BF16), 1024)
    ffn_w = [_col_tiles(ffn_w_gate.astype(BF16), tf), _col_tiles(ffn_w_up.astype(BF16), tf),
             ffn_w_down.astype(BF16)]
    moe_w = [_col_tiles(moe_w_gate.astype(BF16), tf_moe), _col_tiles(moe_w_up.astype(BF16), tf_moe),
             moe_w_down.astype(BF16)]

    for l in range(depth):
        p = _inproj(xt, norm1_g[l], w_in, l, tm)

        lam_init = 0.8 - 0.6 * math.exp(-0.3 * l)
        oa = _diffattn(p, qa, ka, va, da_q_norm[l], da_k_norm[l], da_lambda[l], da_out_norm[l],
                       b, s, lam_init, tq)
        oc = _dilattn(p, qb, kb, vb, dl_q_norm[l], dl_k_norm[l], b, s, tq)

        r, lw, k2, v, na, bb, g = _rwprep(p, rw, s, rw_mu[l], rw_w0[l], rw_a0[l], rw_k_k[l],
                                          rw_k_a[l], rw_w2[l], rw_a2[l], rw_g2[l], min(256, s))
        y = _rwkv(r, lw, k2, v, na, bb, b, s, min(512, s), 12)
        ob = _rwpost(y, r, k2, v, g, rw_ln_g[l], rw_ln_b[l], rw_r_k[l].reshape(-1), min(512, s))

        xt = _outproj(xt, oa, oc, ob, w_out, l, tm)

        i = l // 2
        if l % 2 == 0:
            xt = _ffn(xt, norm2_g[l], *ffn_w, i, min(512, t))
        else:
            ts = min(1024, t)
            h, comb, combt, rcol, rrow, cnt = _router(xt, norm2_g[l], moe_router[i], ts)
            counts = cnt[:, 0, :n_experts].reshape(-1)
            xt = _moe(xt, h, comb, combt, rcol, rrow, counts, *moe_w, i, ts)
    return xt.reshape(b, s, d)
```

```python
import functools
import math

import numpy as np
import jax
import jax.numpy as jnp
from jax import lax
from jax.experimental import pallas as pl
from jax.experimental.pallas import tpu as pltpu

F32 = jnp.float32
BF16 = jnp.bfloat16

LANES = 128
VMEM_LIMIT = 56 * 1024 * 1024

NEG = -1e30
ROPE_THETA = 10000.0
NORM_EPS = 1e-6
RW_LN_EPS = 64e-5
DL_PATTERNS = ((128, 1), (512, 4), (2048, 16))
TOP_K = 2

DA_HEADS, DA_QK = 4, 64
DL_HEADS = 6
RW_HEADS, RW_DIM = 12, 64
DA_W, DL_W, RW_W = 512, 768, 768
RW_PAIRS = RW_W // LANES
CHUNK = 64


def _params(*sem):
    return pltpu.CompilerParams(dimension_semantics=sem, vmem_limit_bytes=VMEM_LIMIT)


def _dot(a, b):
    return jnp.dot(a.astype(BF16), b.astype(BF16), preferred_element_type=F32)


def _dot_nt(a, b):
    return lax.dot_general(a.astype(BF16), b.astype(BF16), (((1,), (1,)), ((), ())),
                           preferred_element_type=F32)


def _dot_tn(a, b):
    return lax.dot_general(a.astype(BF16), b.astype(BF16), (((0,), (0,)), ((), ())),
                           preferred_element_type=F32)


def _split(x):
    hi = x.astype(BF16)
    lo = (x - hi.astype(F32)).astype(BF16)
    return hi, lo


def _dot3(a, b, b_parts=None):
    ah, al = _split(a)
    bh, bl = _split(b) if b_parts is None else b_parts
    return _dot(ah, bh) + _dot(ah, bl) + _dot(al, bh)


def _dot_exact_rhs(a, b_bf16):
    ah, al = _split(a)
    return _dot(ah, b_bf16) + _dot(al, b_bf16)


def _group_ones(width, group):
    i = lax.broadcasted_iota(jnp.int32, (width, width), 0) // group
    j = lax.broadcasted_iota(jnp.int32, (width, width), 1) // group
    return (i == j).astype(BF16)


def _rms(x, g):
    return x * lax.rsqrt(jnp.mean(x * x, axis=-1, keepdims=True) + NORM_EPS) * g


def _cat_blocks(ref, n):
    return jnp.concatenate([ref[c] for c in range(n)], axis=1)


def _put_blocks(ref, val):
    for c in range(ref.shape[0]):
        ref[c] = val[:, c * LANES:(c + 1) * LANES].astype(ref.dtype)


def _inproj_kernel(x_ref, g_ref, w_ref, o_ref, xn_ref):
    @pl.when(pl.program_id(1) == 0)
    def _():
        xn_ref[...] = _rms(x_ref[...], g_ref[...]).astype(BF16)

    _put_blocks(o_ref, jnp.dot(xn_ref[...], w_ref[...], preferred_element_type=F32))


def _inproj(x, g, w, li, tm, tn):
    t, d = x.shape
    n = w.shape[2]
    return pl.pallas_call(
        _inproj_kernel,
        grid=(t // tm, n // tn),
        in_specs=[pl.BlockSpec((tm, d), lambda i, j: (i, 0)),
                  pl.BlockSpec((1, d), lambda i, j: (0, 0)),
                  pl.BlockSpec((None, d, tn), lambda i, j: (li, 0, j))],
        out_specs=pl.BlockSpec((tn // LANES, tm, LANES), lambda i, j: (j, i, 0)),
        out_shape=jax.ShapeDtypeStruct((n // LANES, t, LANES), F32),
        scratch_shapes=[pltpu.VMEM((tm, d), BF16)],
        compiler_params=_params("parallel", "arbitrary"),
    )(x, g.reshape(1, d), w)


LOG2E = math.log2(math.e)


def _prep_qk(x, gain, cos, sin, group, post=1.0):
    ones = _group_ones(LANES, group)
    ms = _dot_exact_rhs(x * x, ones) * (1.0 / group)
    y = x * lax.rsqrt(ms + NORM_EPS) * gain
    half = group // 2
    if group == LANES:
        partner = pltpu.roll(y, half, axis=1)
    else:
        lane = lax.broadcasted_iota(jnp.int32, y.shape, 1)
        fwd = pltpu.roll(y, LANES - half, axis=1)
        bwd = pltpu.roll(y, half, axis=1)
        partner = jnp.where((lane % group) < half, fwd, bwd)
    return ((y * cos + partner * sin) * post).astype(BF16)


def _rope_tables(s, group):
    half = group // 2
    lane = np.arange(LANES)
    inv = ROPE_THETA ** (-jnp.asarray(lane % half, F32) / half)
    ang = jnp.arange(s, dtype=F32)[:, None] * inv[None, :]
    sign = jnp.asarray(np.where((lane % group) < half, -1.0, 1.0), F32)
    return jnp.cos(ang), jnp.sin(ang) * sign[None, :]


def _tile_gain(gain, group):
    return jnp.tile(gain.astype(F32), LANES // group).reshape(1, LANES)


def _block_pairs(nq):
    return [(j, qi) for j in range(nq) for qi in range(j, nq)]


def _vt_blocks(v_ref, tq, nq):
    return [v_ref[j * tq:(j + 1) * tq, :].T.astype(BF16) for j in range(nq)]


def _diffattn_kernel(q_ref, k_ref, v_ref, cos_ref, sin_ref, gq_ref, gk_ref, lam_ref, go_ref, o_ref,
                     s0_ref, s1_ref, *, tq, scale, lam_init):
    nq = q_ref.shape[0] // tq
    cos, sin = cos_ref[...], sin_ref[...]
    kp = _prep_qk(k_ref[...], gk_ref[...], cos, sin, DA_QK)
    qp = _prep_qk(q_ref[...], gq_ref[...], cos, sin, DA_QK, scale * LOG2E)
    lane = lax.broadcasted_iota(jnp.int32, qp.shape, 1)
    zero = jnp.zeros_like(qp)
    qs = (jnp.where(lane < DA_QK, qp, zero), jnp.where(lane >= DA_QK, qp, zero))
    vt = _vt_blocks(v_ref, tq, nq)
    blk = lambda x, i: x[i * tq:(i + 1) * tq, :]
    causal = (lax.broadcasted_iota(jnp.int32, (tq, tq), 1)
              >= lax.broadcasted_iota(jnp.int32, (tq, tq), 0))
    pairs = _block_pairs(nq)
    s_refs = (s0_ref, s1_ref)

    m = [[jnp.full((1, tq), NEG, F32) for _ in range(nq)] for _ in range(2)]
    for idx, (j, qi) in enumerate(pairs):
        for c in range(2):
            sc = _dot_nt(blk(kp, j), blk(qs[c], qi))
            if j == qi:
                sc = jnp.where(causal, sc, NEG)
            s_refs[c][idx] = sc
            m[c][qi] = jnp.maximum(m[c][qi], jnp.max(sc, axis=0, keepdims=True))

    l = [[jnp.zeros((1, tq), F32) for _ in range(nq)] for _ in range(2)]
    acc = [[None] * nq for _ in range(2)]
    for idx, (j, qi) in enumerate(pairs):
        for c in range(2):
            pr = jnp.exp2(s_refs[c][idx] - m[c][qi])
            l[c][qi] = l[c][qi] + jnp.sum(pr, axis=0, keepdims=True)
            d = _dot(vt[j], pr)
            acc[c][qi] = d if acc[c][qi] is None else acc[c][qi] + d

    lm = lam_ref[...]
    lam = (jnp.exp(jnp.sum(lm[0:1] * lm[1:2], axis=-1, keepdims=True))
           - jnp.exp(jnp.sum(lm[2:3] * lm[3:4], axis=-1, keepdims=True)) + lam_init)
    for qi in range(nq):
        ot = acc[0][qi] / l[0][qi] - lam * (acc[1][qi] / l[1][qi])
        ot = ot * lax.rsqrt(jnp.mean(ot * ot, axis=0, keepdims=True) + NORM_EPS)
        o_ref[qi * tq:(qi + 1) * tq, :] = (ot.T * go_ref[...] * (1.0 - lam_init)).astype(BF16)


def _diffattn(p, qcb, kcb, vcb, gq, gk, lam4, gout, b, s, lam_init, tq):
    t = p.shape[1]
    nq = s // tq
    npair = nq * (nq + 1) // 2
    cos, sin = _rope_tables(s, DA_QK)
    blk = lambda cb0: pl.BlockSpec((None, s, LANES), lambda bi, h: (cb0 + h, bi, 0))
    full = lambda shape: pl.BlockSpec(shape, lambda bi, h: (0, 0))
    return pl.pallas_call(
        functools.partial(_diffattn_kernel, tq=tq, scale=DA_QK ** -0.5, lam_init=lam_init),
        grid=(b, DA_HEADS),
        in_specs=[blk(qcb), blk(kcb), blk(vcb),
                  full((s, LANES)), full((s, LANES)), full((1, LANES)), full((1, LANES)),
                  full((4, DA_QK)), full((1, LANES))],
        out_specs=blk(0),
        out_shape=jax.ShapeDtypeStruct((DA_HEADS, t, LANES), BF16),
        scratch_shapes=[pltpu.VMEM((npair, tq, tq), F32), pltpu.VMEM((npair, tq, tq), F32)],
        compiler_params=_params("parallel", "parallel"),
    )(p, p, p, cos, sin, _tile_gain(gq, DA_QK), _tile_gain(gk, DA_QK), lam4.astype(F32),
      gout.astype(F32).reshape(1, LANES))


def _dilated_bias(s, tq):
    nd = s // tq
    d = (np.arange(nd)[:, None, None] * tq + np.arange(tq)[None, None, :]
         - np.arange(tq)[None, :, None])
    cnt = np.zeros(d.shape, np.float64)
    for window, dil in DL_PATTERNS:
        cnt += (d >= 0) & (d % dil == 0) & (d <= window)
    bias = np.where(cnt > 0, np.log2(np.maximum(cnt, 1.0)), NEG)
    return jnp.asarray(bias, F32)


def _dilattn_kernel(q_ref, k_ref, v_ref, cos_ref, sin_ref, gq_ref, gk_ref, bias_ref, o_ref, s_ref,
                    *, tq, scale):
    nq = q_ref.shape[0] // tq
    cos, sin = cos_ref[...], sin_ref[...]
    kp = _prep_qk(k_ref[...], gk_ref[...], cos, sin, LANES)
    qp = _prep_qk(q_ref[...], gq_ref[...], cos, sin, LANES, scale * LOG2E)
    vt = _vt_blocks(v_ref, tq, nq)
    blk = lambda x, i: x[i * tq:(i + 1) * tq, :]
    pairs = _block_pairs(nq)

    m = [jnp.full((1, tq), NEG, F32) for _ in range(nq)]
    for idx, (j, qi) in enumerate(pairs):
        sc = _dot_nt(blk(kp, j), blk(qp, qi)) + bias_ref[qi - j]
        s_ref[idx] = sc
        m[qi] = jnp.maximum(m[qi], jnp.max(sc, axis=0, keepdims=True))

    l = [jnp.zeros((1, tq), F32) for _ in range(nq)]
    acc = [None] * nq
    for idx, (j, qi) in enumerate(pairs):
        pr = jnp.exp2(s_ref[idx] - m[qi])
        l[qi] = l[qi] + jnp.sum(pr, axis=0, keepdims=True)
        d = _dot(vt[j], pr)
        acc[qi] = d if acc[qi] is None else acc[qi] + d

    for qi in range(nq):
        o_ref[qi * tq:(qi + 1) * tq, :] = (acc[qi] / l[qi]).T.astype(BF16)


def _dilattn(p, qcb, kcb, vcb, gq, gk, b, s, tq):
    t = p.shape[1]
    nq = s // tq
    cos, sin = _rope_tables(s, LANES)
    bias = _dilated_bias(s, tq)
    blk = lambda cb0: pl.BlockSpec((None, s, LANES), lambda bi, h: (cb0 + h, bi, 0))
    full = lambda shape: pl.BlockSpec(shape, lambda bi, h: (0,) * len(shape))
    return pl.pallas_call(
        functools.partial(_dilattn_kernel, tq=tq, scale=LANES ** -0.5),
        grid=(b, DL_HEADS),
        in_specs=[blk(qcb), blk(kcb), blk(vcb),
                  full((s, LANES)), full((s, LANES)), full((1, LANES)), full((1, LANES)),
                  full((nq, tq, tq))],
        out_specs=blk(0),
        out_shape=jax.ShapeDtypeStruct((DL_HEADS, t, LANES), BF16),
        scratch_shapes=[pltpu.VMEM((nq * (nq + 1) // 2, tq, tq), F32)],
        compiler_params=_params("parallel", "parallel"),
    )(p, p, p, cos, sin, _tile_gain(gq, LANES), _tile_gain(gk, LANES), bias)


def _rwprep_kernel(r_ref, k_ref, v_ref, x_ref, rp_ref, kp_ref, vp_ref, xp_ref,
                   mu_ref, w0_ref, a0_ref, kk_ref, ka_ref, w2_ref, a2_ref, g2_ref,
                   ro, lwo, ko, vo, nao, bo, go, *, rows_per_seq):
    i = pl.program_id(0)
    tm = r_ref.shape[1]
    first = (i * tm) % rows_per_seq == 0
    row = lax.broadcasted_iota(jnp.int32, (tm, 1), 0)

    def shifted(cur_ref, prev_ref, mu):
        n = cur_ref.shape[0]
        cur = _cat_blocks(cur_ref, n)
        last = jnp.concatenate([prev_ref[c, 7:8, :] for c in range(n)], axis=1)
        last = jnp.where(first, 0.0, last)
        prev = jnp.where(row == 0, last, pltpu.roll(cur, 1, axis=0))
        return cur + mu * (prev - cur)

    mu = mu_ref[...]
    rr = shifted(r_ref, rp_ref, mu[:, 0:RW_W])
    kr = shifted(k_ref, kp_ref, mu[:, RW_W:2 * RW_W])
    vr = shifted(v_ref, vp_ref, mu[:, 2 * RW_W:3 * RW_W])
    xs = shifted(x_ref, xp_ref, mu[:, 3 * RW_W:])

    z = w0_ref[...] + _dot3(jnp.tanh(xs), None, (w2_ref[0], w2_ref[1]))
    nz = -z
    softplus = jnp.maximum(nz, 0.0) + jnp.log(1.0 + jnp.exp(-jnp.abs(nz)))
    w_log = -softplus - 0.5
    a = jax.nn.sigmoid(a0_ref[...] + _dot3(xs, None, (a2_ref[0], a2_ref[1])))
    g = _dot3(jax.nn.sigmoid(xs), None, (g2_ref[0], g2_ref[1]))

    ones = _group_ones(LANES, RW_DIM)
    kkr = kr * kk_ref[...]
    sq = kkr * kkr
    ss = jnp.concatenate([_dot_exact_rhs(sq[:, c * LANES:(c + 1) * LANES], ones)
                          for c in range(RW_PAIRS)], axis=1)
    kk = kkr / jnp.maximum(jnp.sqrt(ss), 1e-12)
    _put_blocks(ro, rr)
    _put_blocks(lwo, -jnp.exp(w_log))
    _put_blocks(ko, kr * (1.0 + (a - 1.0) * ka_ref[...]))
    _put_blocks(vo, vr)
    _put_blocks(nao, -kk)
    _put_blocks(bo, kk * a)
    _put_blocks(go, g)


def _rwprep(p, cb0, s, mu, w0, a0, k_k, k_a, w2, a2, g2, tm):
    t = p.shape[1]
    lr = mu.shape[0] - 3 * RW_W
    nx = lr // LANES
    hilo = lambda w_: jnp.stack(_split(w_))
    w2p = hilo(jnp.zeros((lr, RW_W), F32).at[0:w2.shape[0]].set(w2))
    a2p = hilo(jnp.zeros((lr, RW_W), F32).at[w2.shape[0]:w2.shape[0] + a2.shape[0]].set(a2))
    g2p = hilo(jnp.zeros((lr, RW_W), F32).at[lr - g2.shape[0]:].set(g2))
    r8 = tm // 8
    gb = cb0 // RW_PAIRS
    xb = (cb0 + 3 * RW_PAIRS) // nx

    def cur(c, n):
        return pl.BlockSpec((n, tm, LANES), lambda i: (c, i, 0))

    def prev(c, n):
        return pl.BlockSpec((n, 8, LANES), lambda i: (c, jnp.maximum(i * r8 - 1, 0), 0))

    def full(shape):
        return pl.BlockSpec(shape, lambda i: (0,) * len(shape))

    vec = lambda a_: a_.astype(F32).reshape(1, -1)
    out = jax.ShapeDtypeStruct((RW_PAIRS, t, LANES), F32)
    return pl.pallas_call(
        functools.partial(_rwprep_kernel, rows_per_seq=s),
        grid=(t // tm,),
        in_specs=[cur(gb, RW_PAIRS), cur(gb + 1, RW_PAIRS), cur(gb + 2, RW_PAIRS), cur(xb, nx),
                  prev(gb, RW_PAIRS), prev(gb + 1, RW_PAIRS), prev(gb + 2, RW_PAIRS), prev(xb, nx),
                  full((1, 3 * RW_W + lr)), full((1, RW_W)), full((1, RW_W)), full((1, RW_W)),
                  full((1, RW_W)), full((2, lr, RW_W)), full((2, lr, RW_W)), full((2, lr, RW_W))],
        out_specs=[pl.BlockSpec((RW_PAIRS, tm, LANES), lambda i: (0, i, 0))] * 7,
        out_shape=[out] * 7,
        compiler_params=_params("parallel"),
    )(p, p, p, p, p, p, p, p, vec(mu), vec(w0), vec(a0), vec(k_k), vec(k_a), w2p, a2p, g2p)


def _rwkv_kernel(r_ref, lw_ref, k_ref, v_ref, a_ref, b_ref, y_ref, rh_ref, yh_ref, p_ref, q_ref,
                 st_ref, *, nchunk, unroll):
    L = CHUNK
    W = 2 * L
    npair = r_ref.shape[0]

    @pl.when(pl.program_id(1) == 0)
    def _():
        st_ref[...] = jnp.zeros(st_ref.shape, F32)

    lane = lax.broadcasted_iota(jnp.int32, (L, W), 1)
    rowi = lax.broadcasted_iota(jnp.int32, (L, W), 0)
    strict = (lane % L) < rowi
    incl = (lane % L) <= rowi
    rr = lax.broadcasted_iota(jnp.int32, (W, W), 0)
    cc = lax.broadcasted_iota(jnp.int32, (W, W), 1)
    same = (rr // L) == (cc // L)
    eye = rr == cc
    tl = lax.broadcasted_iota(jnp.int32, (L, L), 0)
    sl = lax.broadcasted_iota(jnp.int32, (L, L), 1)
    tril = (sl <= tl).astype(BF16)

    def bd(x):
        return jnp.where(same, jnp.concatenate([x, x], axis=0), 0.0)

    def group(gi, carry):
        us = [gi * unroll + i for i in range(unroll)]
        prs = [u // nchunk for u in us]
        rws = [pl.ds(pl.multiple_of((u % nchunk) * L, L), L) for u in us]
        G = range(unroll)
        ld = lambda ref: [ref[prs[i], rws[i], :] for i in G]
        r, lw, k, v, a, b = ld(r_ref), ld(lw_ref), ld(k_ref), ld(v_ref), ld(a_ref), ld(b_ref)

        def csum(x):
            hi, lo = _split(x)
            rest = x - hi.astype(F32) - lo.astype(F32)
            return jnp.concatenate([hi, lo, rest.astype(BF16)], axis=1)

        c3 = [_dot(tril, csum(lw[i])) for i in G]
        cin = [c[:, 0:W] + c[:, W:2 * W] + c[:, 2 * W:] for c in c3]
        clast = [c[L - 1:L, :] for c in cin]
        g_inv = [jnp.exp(-c) for c in cin]
        g_tail = [jnp.exp(clast[i] - cin[i]) for i in G]
        at = [a[i] * jnp.exp(cin[i] - lw[i]) for i in G]
        rt = [r[i] * jnp.exp(cin[i]) for i in G]
        abk = [_dot_nt(jnp.concatenate([at[i], rt[i]], axis=0),
                       jnp.concatenate([bd(b[i] * g_inv[i]), bd(k[i] * g_inv[i])], axis=0)) for i in G]
        a_ab = [jnp.where(strict, m[0:L, 0:W], 0.0) for m in abk]
        a_rb = [jnp.where(incl, m[L:W, 0:W], 0.0) for m in abk]
        a_ak = [jnp.where(strict, m[0:L, W:], 0.0) for m in abk]
        a_rk = [jnp.where(incl, m[L:W, W:], 0.0) for m in abk]
        n = [bd(m) for m in a_ab]
        tm = [jnp.where(eye, 1.0, m) for m in n]
        x = [_dot(m, m) for m in n]
        for j in range(5):
            if j < 4:
                xx = [_dot(x[i], jnp.concatenate([x[i], tm[i]], axis=1)) for i in G]
                x = [m[:, 0:W] for m in xx]
                tm = [tm[i] + xx[i][:, W:] for i in G]
            else:
                tm = [tm[i] + _dot(x[i], tm[i]) for i in G]
        v_bd = [bd(m) for m in v]
        kv = [_dot(jnp.concatenate([bd(a_ak[i]), a_rk[i]], axis=0), v_bd[i]) for i in G]
        au = [_dot(tm[i], jnp.concatenate([bd(at[i]), kv[i][0:W]], axis=1)) for i in G]
        ry = [_dot(a_rb[i], au[i]) for i in G]
        zero = jnp.zeros((W, W), F32)
        pq = [_dot_tn(jnp.concatenate([bd(b[i] * g_tail[i]), bd(k[i] * g_tail[i])], axis=0),
                      jnp.concatenate([au[i], jnp.concatenate([zero, v_bd[i]], axis=1)], axis=0))
              for i in G]
        for i in G:
            rh_ref[prs[i], rws[i], :] = rt[i] + ry[i][:, 0:W]
            yh_ref[prs[i], rws[i], :] = ry[i][:, W:] + kv[i][W:]
            p_ref[us[i]] = jnp.where(eye, jnp.exp(clast[i]), 0.0) + pq[i][:, 0:W]
            q_ref[us[i]] = pq[i][:, W:]
        return carry

    lax.fori_loop(0, npair * nchunk // unroll, group, 0)

    def step(c, carry):
        rows = pl.ds(pl.multiple_of(c * L, L), L)
        for pr in range(npair):
            st = st_ref[pr].astype(BF16)
            y_ref[pr, rows, :] = _dot(rh_ref[pr, rows, :], st) + yh_ref[pr, rows, :]
            st_ref[pr] = _dot(p_ref[pr * nchunk + c], st) + q_ref[pr * nchunk + c]
        return carry

    lax.fori_loop(0, nchunk, step, 0)


def _rwkv(r, lw, k, v, na, bb, b, s, sb, unroll):
    npair, t, w = r.shape
    nchunk = sb // CHUNK
    nsb = s // sb
    spec = pl.BlockSpec((npair, sb, w), lambda bi, si: (0, bi * nsb + si, 0))
    return pl.pallas_call(
        functools.partial(_rwkv_kernel, nchunk=nchunk, unroll=unroll),
        grid=(b, nsb),
        in_specs=[spec] * 6,
        out_specs=spec,
        out_shape=jax.ShapeDtypeStruct((npair, t, w), F32),
        scratch_shapes=[pltpu.VMEM((npair, sb, w), F32), pltpu.VMEM((npair, sb, w), F32),
                        pltpu.VMEM((npair * nchunk, w, w), F32), pltpu.VMEM((npair * nchunk, w, w), F32),
                        pltpu.VMEM((npair, w, w), F32)],
        compiler_params=_params("parallel", "arbitrary"),
    )(r, lw, k, v, na, bb)


def _rwpost_kernel(y_ref, r_ref, k_ref, v_ref, g_ref, lng_ref, lnb_ref, rk_ref, o_ref):
    ones = _group_ones(LANES, RW_DIM)
    for c in range(RW_PAIRS):
        cols = slice(c * LANES, (c + 1) * LANES)
        y = y_ref[c]
        mu = _dot_exact_rhs(y, ones) * (1.0 / RW_DIM)
        yc = y - mu
        var = _dot_exact_rhs(yc * yc, ones) * (1.0 / RW_DIM)
        out = yc * lax.rsqrt(var + RW_LN_EPS) * lng_ref[:, cols] + lnb_ref[:, cols]
        bonus = _dot_exact_rhs(r_ref[c] * k_ref[c] * rk_ref[:, cols], ones)
        o_ref[c] = ((out + bonus * v_ref[c]) * g_ref[c]).astype(o_ref.dtype)


def _rwpost(y, r, k, v, g, ln_g, ln_b, r_k, tm):
    t = y.shape[1]
    big = pl.BlockSpec((RW_PAIRS, tm, LANES), lambda i: (0, i, 0))
    small = pl.BlockSpec((1, RW_W), lambda i: (0, 0))
    vec = lambda a_: a_.astype(F32).reshape(1, RW_W)
    return pl.pallas_call(
        _rwpost_kernel,
        grid=(t // tm,),
        in_specs=[big] * 5 + [small] * 3,
        out_specs=big,
        out_shape=jax.ShapeDtypeStruct((RW_PAIRS, t, LANES), BF16),
        compiler_params=_params("parallel"),
    )(y, r, k, v, g, vec(ln_g), vec(ln_b), vec(r_k))


def _outproj_kernel(x_ref, oa_ref, oc_ref, ob_ref, w_ref, o_ref):
    mix = jnp.concatenate([_cat_blocks(oa_ref, oa_ref.shape[0]), _cat_blocks(oc_ref, oc_ref.shape[0]),
                           _cat_blocks(ob_ref, ob_ref.shape[0])], axis=1)
    o_ref[...] = x_ref[...] + jnp.dot(mix, w_ref[...], preferred_element_type=F32)


def _outproj(x, oa, oc, ob, w, li, tm, tn):
    t, d = x.shape
    blocks = lambda a_: pl.BlockSpec((a_.shape[0], tm, LANES), lambda i, j: (0, i, 0))
    return pl.pallas_call(
        _outproj_kernel,
        grid=(t // tm, d // tn),
        in_specs=[pl.BlockSpec((tm, tn), lambda i, j: (i, j)),
                  blocks(oa), blocks(oc), blocks(ob),
                  pl.BlockSpec((None, w.shape[1], tn), lambda i, j: (li, 0, j))],
        out_specs=pl.BlockSpec((tm, tn), lambda i, j: (i, j)),
        out_shape=jax.ShapeDtypeStruct((t, d), F32),
        compiler_params=_params("parallel", "arbitrary"),
    )(x, oa, oc, ob, w)


FFN_SPLIT = 2


def _swiglu_rows(x_ref, wg_ref, wu_ref, wd_ref):
    n = x_ref.shape[0] // FFN_SPLIT
    sl = [slice(i * n, (i + 1) * n) for i in range(FFN_SPLIT)]
    wg, wu, wd = wg_ref[...], wu_ref[...], wd_ref[...]
    gu = [(jnp.dot(x_ref[r, :], wg, preferred_element_type=F32),
           jnp.dot(x_ref[r, :], wu, preferred_element_type=F32)) for r in sl]
    act = [(g * jax.nn.sigmoid(g) * u).astype(BF16) for g, u in gu]
    return [(r, jnp.dot(a, wd, preferred_element_type=F32)) for r, a in zip(sl, act)]


def _ffn_kernel(x_ref, g_ref, wg_ref, wu_ref, wd_ref, o_ref, xn_ref, acc_ref):
    f = pl.program_id(1)

    @pl.when(f == 0)
    def _():
        xn_ref[...] = _rms(x_ref[...], g_ref[...]).astype(BF16)
        acc_ref[...] = jnp.zeros(acc_ref.shape, F32)

    for r, y in _swiglu_rows(xn_ref, wg_ref, wu_ref, wd_ref):
        acc_ref[r, :] += y

    @pl.when(f == pl.num_programs(1) - 1)
    def _():
        o_ref[...] = x_ref[...] + acc_ref[...]


def _ffn(x, g, wg, wu, wd, li, tm, tf):
    t, d = x.shape
    ff = wg.shape[2]
    return pl.pallas_call(
        _ffn_kernel,
        grid=(t // tm, ff // tf),
        in_specs=[pl.BlockSpec((tm, d), lambda i, f: (i, 0)),
                  pl.BlockSpec((1, d), lambda i, f: (0, 0)),
                  pl.BlockSpec((None, d, tf), lambda i, f: (li, 0, f)),
                  pl.BlockSpec((None, d, tf), lambda i, f: (li, 0, f)),
                  pl.BlockSpec((None, tf, d), lambda i, f: (li, f, 0))],
        out_specs=pl.BlockSpec((tm, d), lambda i, f: (i, 0)),
        out_shape=jax.ShapeDtypeStruct((t, d), F32),
        scratch_shapes=[pltpu.VMEM((tm, d), BF16), pltpu.VMEM((tm, d), F32)],
        compiler_params=_params("parallel", "arbitrary"),
    )(x, g.reshape(1, d), wg, wu, wd)


def _pack_halves(x):
    c = x.shape[1] // 2
    bits = lax.bitcast_convert_type(x.astype(BF16).astype(F32), jnp.uint32)
    return (bits[:, :c] >> 16) | (bits[:, c:] & jnp.uint32(0xFFFF0000))


def _unpack_halves(w):
    lo = lax.bitcast_convert_type(w << 16, F32)
    hi = lax.bitcast_convert_type(w & jnp.uint32(0xFFFF0000), F32)
    return jnp.concatenate([lo, hi], axis=1)


TOK_E1, TOK_E2, TOK_R1, TOK_R2, TOK_W1, TOK_W2 = range(6)


def _router_kernel(x_ref, g_ref, wr_ref, h_ref, tok_ref, cnt_ref, *, n_experts):
    h = _rms(x_ref[...], g_ref[...])
    h_ref[...] = _pack_halves(h)
    logits = _dot3(h, wr_ref[...])
    lane = lax.broadcasted_iota(jnp.int32, logits.shape, 1)
    lg = jnp.where(lane < n_experts, logits, NEG)
    m1 = jnp.max(lg, axis=-1, keepdims=True)
    i1 = jnp.min(jnp.where(lg == m1, lane, LANES), axis=-1, keepdims=True)
    lg2 = jnp.where(lane == i1, NEG, lg)
    m2 = jnp.max(lg2, axis=-1, keepdims=True)
    i2 = jnp.min(jnp.where(lg2 == m2, lane, LANES), axis=-1, keepdims=True)
    e2 = jnp.exp(m2 - m1)
    w1 = 1.0 / (1.0 + e2)
    w2 = e2 * w1
    live = jnp.where((lane == i1) | ((lane == i2) & (w2 > 0.0)), 1.0, 0.0)
    ts = live.shape[0]
    tt = lax.broadcasted_iota(jnp.int32, (ts, ts), 0)
    uu = lax.broadcasted_iota(jnp.int32, (ts, ts), 1)
    rank = _dot((uu < tt).astype(BF16), live)
    pick = lambda sel, val: jnp.sum(jnp.where(sel, val, 0.0), axis=-1, keepdims=True)
    rec = [i1.astype(F32), i2.astype(F32), pick(lane == i1, rank), pick(lane == i2, rank), w1, w2]
    tok = jnp.zeros(logits.shape, F32)
    for k, val in enumerate(rec):
        tok = jnp.where(lane == k, val, tok)
    tok_ref[...] = tok
    cnt_ref[0] = jnp.sum(live, axis=0, keepdims=True).astype(jnp.int32)


def _router(x, g, wr, ts):
    t, d = x.shape
    e = wr.shape[1]
    wrp = jnp.zeros((d, LANES), F32).at[:, :e].set(wr)
    nt = t // ts
    return pl.pallas_call(
        functools.partial(_router_kernel, n_experts=e),
        grid=(nt,),
        in_specs=[pl.BlockSpec((ts, d), lambda i: (i, 0)),
                  pl.BlockSpec((1, d), lambda i: (0, 0)),
                  pl.BlockSpec((d, LANES), lambda i: (0, 0))],
        out_specs=[pl.BlockSpec((ts, d // 2), lambda i: (i, 0)),
                   pl.BlockSpec((ts, LANES), lambda i: (i, 0)),
                   pl.BlockSpec((1, 1, LANES), lambda i: (i, 0, 0))],
        out_shape=[jax.ShapeDtypeStruct((t, d // 2), jnp.uint32),
                   jax.ShapeDtypeStruct((t, LANES), F32),
                   jax.ShapeDtypeStruct((nt, 1, LANES), jnp.int32)],
        compiler_params=_params("parallel"),
    )(x, g.reshape(1, d), wrp)


MOE_BMF = 512
MOE_TB = 2048


def _moe_plan(counts, tok, nt, ne, ts, bmf):
    i32 = jnp.int32
    t = tok.shape[0]
    cnt = counts.reshape(nt, ne).astype(i32)
    tot = jnp.sum(cnt, axis=0)
    ptot = (tot + bmf - 1) // bmf * bmf
    eend = jnp.cumsum(ptot)
    seg0 = (eend - ptot)[None, :] + jnp.cumsum(cnt, axis=0) - cnt
    nrows = 2 * t + ne * bmf
    nblk = nrows // bmf
    blk_row = jnp.arange(nblk, dtype=i32) * bmf
    blk_exp = jnp.minimum(jnp.sum((eend[None, :] <= blk_row[:, None]).astype(i32), axis=1), ne - 1)
    nvalid = (eend[-1] // bmf).astype(i32).reshape(1)
    tile = jnp.arange(t, dtype=i32) // ts
    e1, e2 = tok[:, TOK_E1].astype(i32), tok[:, TOK_E2].astype(i32)
    flat = seg0.reshape(-1)
    pos1 = flat[tile * ne + e1] + tok[:, TOK_R1].astype(i32)
    pos2 = flat[tile * ne + e2] + tok[:, TOK_R2].astype(i32)
    pos2 = jnp.where(tok[:, TOK_W2] > 0.0, pos2, pos1)
    clamp = lambda p: jnp.clip(p, 0, nrows - 1).astype(i32)
    return dict(nrows=nrows, blk_exp=blk_exp.astype(i32), nvalid=nvalid, pos1=clamp(pos1), pos2=clamp(pos2))


def _row(ref, i):
    return ref.at[pl.ds(i, 1)]


def _moe_scatter_kernel(p1_ref, p2_ref, h_ref, zero_ref, o_ref, sem, *, tb):
    del zero_ref
    base = pl.program_id(0) * tb

    def issue(j, carry):
        pltpu.make_async_copy(_row(h_ref, base + j), _row(o_ref, p1_ref[j]), sem).start()
        pltpu.make_async_copy(_row(h_ref, base + j), _row(o_ref, p2_ref[j]), sem).start()
        return carry

    lax.fori_loop(0, tb, issue, 0)
    for _ in range(2):
        pltpu.make_async_copy(h_ref.at[pl.ds(0, tb)], o_ref.at[pl.ds(0, tb)], sem).wait()


def _dma_params():
    return pltpu.CompilerParams(dimension_semantics=("arbitrary",), disable_bounds_checks=True)


def _moe_scatter(plan, h32, tb):
    t, c = h32.shape
    smem = pl.BlockSpec((tb,), lambda i: (i,), memory_space=pltpu.SMEM)
    hbm = pl.BlockSpec(memory_space=pl.ANY)
    zeros = jnp.zeros((plan["nrows"], c), jnp.uint32)
    return pl.pallas_call(
        functools.partial(_moe_scatter_kernel, tb=tb),
        grid=(t // tb,),
        in_specs=[smem, smem, hbm, hbm],
        out_specs=hbm,
        out_shape=jax.ShapeDtypeStruct(zeros.shape, jnp.uint32),
        scratch_shapes=[pltpu.SemaphoreType.DMA(())],
        input_output_aliases={3: 0},
        compiler_params=_dma_params(),
    )(plan["pos1"], plan["pos2"], h32, zeros)


def _moe_ffn_kernel(bexp_ref, nv_ref, x_ref, wg_ref, wu_ref, wd_ref, o_ref, xb_ref, acc_ref):
    i, f = pl.program_id(0), pl.program_id(1)
    valid = i < nv_ref[0]

    @pl.when(valid)
    def _():
        @pl.when(f == 0)
        def _():
            xb_ref[...] = _unpack_halves(x_ref[...]).astype(BF16)
            acc_ref[...] = jnp.zeros(acc_ref.shape, F32)

        for r, y in _swiglu_rows(xb_ref, wg_ref, wu_ref, wd_ref):
            acc_ref[r, :] += y

    @pl.when(f == pl.num_programs(1) - 1)
    def _():
        @pl.when(valid)
        def _():
            o_ref[...] = _pack_halves(acc_ref[...])

        @pl.when(jnp.logical_not(valid))
        def _():
            o_ref[...] = jnp.zeros(o_ref.shape, o_ref.dtype)


def _moe_ffn(plan, xs, wg, wu, wd, li, bmf, tf):
    c = xs.shape[1]
    d = 2 * c
    ff = wg.shape[3]
    nblk = plan["blk_exp"].shape[0]
    grid_spec = pltpu.PrefetchScalarGridSpec(
        num_scalar_prefetch=2,
        grid=(nblk, ff // tf),
        in_specs=[pl.BlockSpec((bmf, c), lambda i, f, be, nv: (i, 0)),
                  pl.BlockSpec((None, None, d, tf), lambda i, f, be, nv: (li, be[i], 0, f)),
                  pl.BlockSpec((None, None, d, tf), lambda i, f, be, nv: (li, be[i], 0, f)),
                  pl.BlockSpec((None, None, tf, d), lambda i, f, be, nv: (li, be[i], f, 0))],
        out_specs=pl.BlockSpec((bmf, c), lambda i, f, be, nv: (i, 0)),
        scratch_shapes=[pltpu.VMEM((bmf, d), BF16), pltpu.VMEM((bmf, d), F32)],
    )
    return pl.pallas_call(
        _moe_ffn_kernel,
        grid_spec=grid_spec,
        out_shape=jax.ShapeDtypeStruct((nblk * bmf, c), jnp.uint32),
        compiler_params=_params("parallel", "arbitrary"),
    )(plan["blk_exp"], plan["nvalid"], xs, wg, wu, wd)


def _moe_fetch_kernel(p1_ref, p2_ref, y_ref, o1_ref, o2_ref, sem, *, tb):
    base = pl.program_id(0) * tb

    def issue(j, carry):
        pltpu.make_async_copy(_row(y_ref, p1_ref[j]), _row(o1_ref, base + j), sem).start()
        pltpu.make_async_copy(_row(y_ref, p2_ref[j]), _row(o2_ref, base + j), sem).start()
        return carry

    lax.fori_loop(0, tb, issue, 0)
    for o_ref in (o1_ref, o2_ref):
        pltpu.make_async_copy(y_ref.at[pl.ds(0, tb)], o_ref.at[pl.ds(0, tb)], sem).wait()


def _moe_fetch(plan, ys, t, tb):
    c = ys.shape[1]
    smem = pl.BlockSpec((tb,), lambda i: (i,), memory_space=pltpu.SMEM)
    hbm = pl.BlockSpec(memory_space=pl.ANY)
    out = jax.ShapeDtypeStruct((t, c), jnp.uint32)
    return pl.pallas_call(
        functools.partial(_moe_fetch_kernel, tb=tb),
        grid=(t // tb,),
        in_specs=[smem, smem, hbm],
        out_specs=[hbm, hbm],
        out_shape=[out, out],
        scratch_shapes=[pltpu.SemaphoreType.DMA(())],
        compiler_params=_dma_params(),
    )(plan["pos1"], plan["pos2"], ys)


def _moe_combine_kernel(x_ref, y1_ref, y2_ref, tok_ref, o_ref):
    lane = lax.broadcasted_iota(jnp.int32, tok_ref.shape, 1)
    tok = tok_ref[...]
    w1 = jnp.sum(jnp.where(lane == TOK_W1, tok, 0.0), axis=-1, keepdims=True)
    w2 = jnp.sum(jnp.where(lane == TOK_W2, tok, 0.0), axis=-1, keepdims=True)
    o_ref[...] = x_ref[...] + w1 * _unpack_halves(y1_ref[...]) + w2 * _unpack_halves(y2_ref[...])


def _moe_combine(x, y1, y2, tok, tm):
    t, d = x.shape
    row = lambda w: pl.BlockSpec((tm, w), lambda i: (i, 0))
    return pl.pallas_call(
        _moe_combine_kernel,
        grid=(t // tm,),
        in_specs=[row(d), row(d // 2), row(d // 2), row(LANES)],
        out_specs=row(d),
        out_shape=jax.ShapeDtypeStruct((t, d), F32),
        compiler_params=_params("parallel"),
    )(x, y1, y2, tok)


def _moe(x, h32, tok, counts, wg, wu, wd, li, ts, tf, bmf=MOE_BMF, tb=MOE_TB):
    t = x.shape[0]
    tb = min(tb, t)
    plan = _moe_plan(counts, tok, t // ts, wg.shape[1], ts, bmf)
    xs = _moe_scatter(plan, h32, tb)
    ys = _moe_ffn(plan, xs, wg, wu, wd, li, bmf, tf)
    y1, y2 = _moe_fetch(plan, ys, t, tb)
    return _moe_combine(x, y1, y2, tok, min(512, t))


def kernel(x, norm1_g, w_in, da_q_norm, da_k_norm, da_lambda, da_out_norm, dl_q_norm, dl_k_norm, rw_mu, rw_w0, rw_w2, rw_a0, rw_a2, rw_g2, rw_k_k, rw_k_a, rw_r_k, rw_ln_g, rw_ln_b, w_out, norm2_g, ffn_w_gate, ffn_w_up, ffn_w_down, moe_router, moe_w_gate, moe_w_up, moe_w_down):
    b, s, d = x.shape
    depth = w_in.shape[0]
    t = b * s
    n_experts = moe_router.shape[-1]
    xt = x.reshape(t, d)

    qa, ka, va = 0, DA_HEADS, 2 * DA_HEADS
    qb, kb, vb = 3 * DA_HEADS, 3 * DA_HEADS + DL_HEADS, 3 * DA_HEADS + 2 * DL_HEADS
    rw = 3 * DA_HEADS + 3 * DL_HEADS

    tm = min(1024, t)
    tq = min(256, s)
    tf = 512
    assert ffn_w_gate.shape[2] % tf == 0 and moe_w_gate.shape[3] % tf == 0

    w_in, w_out = w_in.astype(BF16), w_out.astype(BF16)
    ffn_w = [w.astype(BF16) for w in (ffn_w_gate, ffn_w_up, ffn_w_down)]
    moe_w = [w.astype(BF16) for w in (moe_w_gate, moe_w_up, moe_w_down)]

    for l in range(depth):
        p = _inproj(xt, norm1_g[l], w_in, l, tm, 1280)

        lam_init = 0.8 - 0.6 * math.exp(-0.3 * l)
        oa = _diffattn(p, qa, ka, va, da_q_norm[l], da_k_norm[l], da_lambda[l], da_out_norm[l],
                       b, s, lam_init, tq)
        oc = _dilattn(p, qb, kb, vb, dl_q_norm[l], dl_k_norm[l], b, s, tq)

        r, lw, k2, v, na, bb, g = _rwprep(p, rw, s, rw_mu[l], rw_w0[l], rw_a0[l], rw_k_k[l],
                                          rw_k_a[l], rw_w2[l], rw_a2[l], rw_g2[l], min(256, s))
        y = _rwkv(r, lw, k2, v, na, bb, b, s, min(512, s), 12)
        ob = _rwpost(y, r, k2, v, g, rw_ln_g[l], rw_ln_b[l], rw_r_k[l].reshape(-1), min(512, s))

        xt = _outproj(xt, oa, oc, ob, w_out, l, tm, 1024)

        i = l // 2
        if l % 2 == 0:
            xt = _ffn(xt, norm2_g[l], *ffn_w, i, min(512, t), tf)
        else:
            ts = min(1024, t)
            h32, tok, cnt = _router(xt, norm2_g[l], moe_router[i], ts)
            counts = cnt[:, 0, :n_experts].reshape(-1)
            xt = _moe(xt, h32, tok, counts, *moe_w, i, ts, tf)
    return xt.reshape(b, s, d)
```

```python
import functools
import math

import numpy as np
import jax
import jax.numpy as jnp
from jax import lax
from jax.experimental import pallas as pl
from jax.experimental.pallas import tpu as pltpu

F32 = jnp.float32
BF16 = jnp.bfloat16

LANES = 128
VMEM_LIMIT = 56 * 1024 * 1024

NEG = -1e30
ROPE_THETA = 10000.0
NORM_EPS = 1e-6
RW_LN_EPS = 64e-5
DL_PATTERNS = ((128, 1), (512, 4), (2048, 16))
TOP_K = 2

DA_HEADS, DA_QK = 4, 64
DL_HEADS = 6
RW_HEADS, RW_DIM = 12, 64
DA_W, DL_W, RW_W = 512, 768, 768
RW_PAIRS = RW_W // LANES
CHUNK = 64


def _params(*sem):
    return pltpu.CompilerParams(dimension_semantics=sem, vmem_limit_bytes=VMEM_LIMIT)


def _dot(a, b):
    return jnp.dot(a.astype(BF16), b.astype(BF16), preferred_element_type=F32)


def _dot_nt(a, b):
    return lax.dot_general(a.astype(BF16), b.astype(BF16), (((1,), (1,)), ((), ())),
                           preferred_element_type=F32)


def _dot_tn(a, b):
    return lax.dot_general(a.astype(BF16), b.astype(BF16), (((0,), (0,)), ((), ())),
                           preferred_element_type=F32)


def _split(x):
    hi = x.astype(BF16)
    lo = (x - hi.astype(F32)).astype(BF16)
    return hi, lo


def _dot3(a, b, b_parts=None):
    ah, al = _split(a)
    bh, bl = _split(b) if b_parts is None else b_parts
    return _dot(ah, bh) + _dot(ah, bl) + _dot(al, bh)


def _dot_exact_rhs(a, b_bf16):
    ah, al = _split(a)
    return _dot(ah, b_bf16) + _dot(al, b_bf16)


def _group_ones(width, group):
    i = lax.broadcasted_iota(jnp.int32, (width, width), 0) // group
    j = lax.broadcasted_iota(jnp.int32, (width, width), 1) // group
    return (i == j).astype(BF16)


def _rms(x, g):
    return x * lax.rsqrt(jnp.mean(x * x, axis=-1, keepdims=True) + NORM_EPS) * g


def _cat_blocks(ref, n):
    return jnp.concatenate([ref[c] for c in range(n)], axis=1)


def _put_blocks(ref, val):
    for c in range(ref.shape[0]):
        ref[c] = val[:, c * LANES:(c + 1) * LANES].astype(ref.dtype)


def _inproj_kernel(x_ref, g_ref, w_ref, o_ref, xn_ref):
    @pl.when(pl.program_id(1) == 0)
    def _():
        xn_ref[...] = _rms(x_ref[...], g_ref[...]).astype(BF16)

    _put_blocks(o_ref, jnp.dot(xn_ref[...], w_ref[...], preferred_element_type=F32))


def _inproj(x, g, w, li, tm, tn):
    t, d = x.shape
    n = w.shape[2]
    return pl.pallas_call(
        _inproj_kernel,
        grid=(t // tm, n // tn),
        in_specs=[pl.BlockSpec((tm, d), lambda i, j: (i, 0)),
                  pl.BlockSpec((1, d), lambda i, j: (0, 0)),
                  pl.BlockSpec((None, d, tn), lambda i, j: (li, 0, j))],
        out_specs=pl.BlockSpec((tn // LANES, tm, LANES), lambda i, j: (j, i, 0)),
        out_shape=jax.ShapeDtypeStruct((n // LANES, t, LANES), F32),
        scratch_shapes=[pltpu.VMEM((tm, d), BF16)],
        compiler_params=_params("parallel", "arbitrary"),
    )(x, g.reshape(1, d), w)


LOG2E = math.log2(math.e)


def _prep_qk(x, gain, cos, sin, group, post=1.0):
    ones = _group_ones(LANES, group)
    ms = _dot_exact_rhs(x * x, ones) * (1.0 / group)
    y = x * lax.rsqrt(ms + NORM_EPS) * gain
    half = group // 2
    if group == LANES:
        partner = pltpu.roll(y, half, axis=1)
    else:
        lane = lax.broadcasted_iota(jnp.int32, y.shape, 1)
        fwd = pltpu.roll(y, LANES - half, axis=1)
        bwd = pltpu.roll(y, half, axis=1)
        partner = jnp.where((lane % group) < half, fwd, bwd)
    return ((y * cos + partner * sin) * post).astype(BF16)


def _rope_tables(s, group):
    half = group // 2
    lane = np.arange(LANES)
    inv = ROPE_THETA ** (-jnp.asarray(lane % half, F32) / half)
    ang = jnp.arange(s, dtype=F32)[:, None] * inv[None, :]
    sign = jnp.asarray(np.where((lane % group) < half, -1.0, 1.0), F32)
    return jnp.cos(ang), jnp.sin(ang) * sign[None, :]


def _tile_gain(gain, group):
    return jnp.tile(gain.astype(F32), LANES // group).reshape(1, LANES)


def _block_pairs(nq):
    return [(j, qi) for j in range(nq) for qi in range(j, nq)]


def _vt_blocks(v_ref, tq, nq):
    return [v_ref[j * tq:(j + 1) * tq, :].T.astype(BF16) for j in range(nq)]


def _diffattn_kernel(q_ref, k_ref, v_ref, cos_ref, sin_ref, gq_ref, gk_ref, lam_ref, go_ref, o_ref,
                     s0_ref, s1_ref, *, tq, scale, lam_init):
    nq = q_ref.shape[0] // tq
    cos, sin = cos_ref[...], sin_ref[...]
    kp = _prep_qk(k_ref[...], gk_ref[...], cos, sin, DA_QK)
    qp = _prep_qk(q_ref[...], gq_ref[...], cos, sin, DA_QK, scale * LOG2E)
    lane = lax.broadcasted_iota(jnp.int32, qp.shape, 1)
    zero = jnp.zeros_like(qp)
    qs = (jnp.where(lane < DA_QK, qp, zero), jnp.where(lane >= DA_QK, qp, zero))
    vt = _vt_blocks(v_ref, tq, nq)
    blk = lambda x, i: x[i * tq:(i + 1) * tq, :]
    causal = (lax.broadcasted_iota(jnp.int32, (tq, tq), 1)
              >= lax.broadcasted_iota(jnp.int32, (tq, tq), 0))
    pairs = _block_pairs(nq)
    s_refs = (s0_ref, s1_ref)

    m = [[jnp.full((1, tq), NEG, F32) for _ in range(nq)] for _ in range(2)]
    for idx, (j, qi) in enumerate(pairs):
        for c in range(2):
            sc = _dot_nt(blk(kp, j), blk(qs[c], qi))
            if j == qi:
                sc = jnp.where(causal, sc, NEG)
            s_refs[c][idx] = sc
            m[c][qi] = jnp.maximum(m[c][qi], jnp.max(sc, axis=0, keepdims=True))

    l = [[jnp.zeros((1, tq), F32) for _ in range(nq)] for _ in range(2)]
    acc = [[None] * nq for _ in range(2)]
    for idx, (j, qi) in enumerate(pairs):
        for c in range(2):
            pr = jnp.exp2(s_refs[c][idx] - m[c][qi])
            l[c][qi] = l[c][qi] + jnp.sum(pr, axis=0, keepdims=True)
            d = _dot(vt[j], pr)
            acc[c][qi] = d if acc[c][qi] is None else acc[c][qi] + d

    lm = lam_ref[...]
    lam = (jnp.exp(jnp.sum(lm[0:1] * lm[1:2], axis=-1, keepdims=True))
           - jnp.exp(jnp.sum(lm[2:3] * lm[3:4], axis=-1, keepdims=True)) + lam_init)
    for qi in range(nq):
        ot = acc[0][qi] / l[0][qi] - lam * (acc[1][qi] / l[1][qi])
        ot = ot * lax.rsqrt(jnp.mean(ot * ot, axis=0, keepdims=True) + NORM_EPS)
        o_ref[qi * tq:(qi + 1) * tq, :] = (ot.T * go_ref[...] * (1.0 - lam_init)).astype(BF16)


def _diffattn(p, qcb, kcb, vcb, gq, gk, lam4, gout, b, s, lam_init, tq):
    t = p.shape[1]
    nq = s // tq
    npair = nq * (nq + 1) // 2
    cos, sin = _rope_tables(s, DA_QK)
    blk = lambda cb0: pl.BlockSpec((None, s, LANES), lambda bi, h: (cb0 + h, bi, 0))
    full = lambda shape: pl.BlockSpec(shape, lambda bi, h: (0, 0))
    return pl.pallas_call(
        functools.partial(_diffattn_kernel, tq=tq, scale=DA_QK ** -0.5, lam_init=lam_init),
        grid=(b, DA_HEADS),
        in_specs=[blk(qcb), blk(kcb), blk(vcb),
                  full((s, LANES)), full((s, LANES)), full((1, LANES)), full((1, LANES)),
                  full((4, DA_QK)), full((1, LANES))],
        out_specs=blk(0),
        out_shape=jax.ShapeDtypeStruct((DA_HEADS, t, LANES), BF16),
        scratch_shapes=[pltpu.VMEM((npair, tq, tq), F32), pltpu.VMEM((npair, tq, tq), F32)],
        compiler_params=_params("parallel", "parallel"),
    )(p, p, p, cos, sin, _tile_gain(gq, DA_QK), _tile_gain(gk, DA_QK), lam4.astype(F32),
      gout.astype(F32).reshape(1, LANES))


def _dilated_bias(s, tq):
    nd = s // tq
    d = (np.arange(nd)[:, None, None] * tq + np.arange(tq)[None, None, :]
         - np.arange(tq)[None, :, None])
    cnt = np.zeros(d.shape, np.float64)
    for window, dil in DL_PATTERNS:
        cnt += (d >= 0) & (d % dil == 0) & (d <= window)
    bias = np.where(cnt > 0, np.log2(np.maximum(cnt, 1.0)), NEG)
    return jnp.asarray(bias, F32)


def _dilattn_kernel(q_ref, k_ref, v_ref, cos_ref, sin_ref, gq_ref, gk_ref, bias_ref, o_ref, s_ref,
                    *, tq, scale):
    nq = q_ref.shape[0] // tq
    cos, sin = cos_ref[...], sin_ref[...]
    kp = _prep_qk(k_ref[...], gk_ref[...], cos, sin, LANES)
    qp = _prep_qk(q_ref[...], gq_ref[...], cos, sin, LANES, scale * LOG2E)
    vt = _vt_blocks(v_ref, tq, nq)
    blk = lambda x, i: x[i * tq:(i + 1) * tq, :]
    pairs = _block_pairs(nq)

    m = [jnp.full((1, tq), NEG, F32) for _ in range(nq)]
    for idx, (j, qi) in enumerate(pairs):
        sc = _dot_nt(blk(kp, j), blk(qp, qi)) + bias_ref[qi - j]
        s_ref[idx] = sc
        m[qi] = jnp.maximum(m[qi], jnp.max(sc, axis=0, keepdims=True))

    l = [jnp.zeros((1, tq), F32) for _ in range(nq)]
    acc = [None] * nq
    for idx, (j, qi) in enumerate(pairs):
        pr = jnp.exp2(s_ref[idx] - m[qi])
        l[qi] = l[qi] + jnp.sum(pr, axis=0, keepdims=True)
        d = _dot(vt[j], pr)
        acc[qi] = d if acc[qi] is None else acc[qi] + d

    for qi in range(nq):
        o_ref[qi * tq:(qi + 1) * tq, :] = (acc[qi] / l[qi]).T.astype(BF16)


def _dilattn(p, qcb, kcb, vcb, gq, gk, b, s, tq):
    t = p.shape[1]
    nq = s // tq
    cos, sin = _rope_tables(s, LANES)
    bias = _dilated_bias(s, tq)
    blk = lambda cb0: pl.BlockSpec((None, s, LANES), lambda bi, h: (cb0 + h, bi, 0))
    full = lambda shape: pl.BlockSpec(shape, lambda bi, h: (0,) * len(shape))
    return pl.pallas_call(
        functools.partial(_dilattn_kernel, tq=tq, scale=LANES ** -0.5),
        grid=(b, DL_HEADS),
        in_specs=[blk(qcb), blk(kcb), blk(vcb),
                  full((s, LANES)), full((s, LANES)), full((1, LANES)), full((1, LANES)),
                  full((nq, tq, tq))],
        out_specs=blk(0),
        out_shape=jax.ShapeDtypeStruct((DL_HEADS, t, LANES), BF16),
        scratch_shapes=[pltpu.VMEM((nq * (nq + 1) // 2, tq, tq), F32)],
        compiler_params=_params("parallel", "parallel"),
    )(p, p, p, cos, sin, _tile_gain(gq, LANES), _tile_gain(gk, LANES), bias)


def _rwprep_kernel(r_ref, k_ref, v_ref, x_ref, rp_ref, kp_ref, vp_ref, xp_ref,
                   mu_ref, w0_ref, a0_ref, kk_ref, ka_ref, w2_ref, a2_ref, g2_ref,
                   ro, lwo, ko, vo, nao, bo, go, *, rows_per_seq):
    i = pl.program_id(0)
    tm = r_ref.shape[1]
    first = (i * tm) % rows_per_seq == 0
    row = lax.broadcasted_iota(jnp.int32, (tm, 1), 0)

    def shifted(cur_ref, prev_ref, mu):
        n = cur_ref.shape[0]
        cur = _cat_blocks(cur_ref, n)
        last = jnp.concatenate([prev_ref[c, 7:8, :] for c in range(n)], axis=1)
        last = jnp.where(first, 0.0, last)
        prev = jnp.where(row == 0, last, pltpu.roll(cur, 1, axis=0))
        return cur + mu * (prev - cur)

    mu = mu_ref[...]
    rr = shifted(r_ref, rp_ref, mu[:, 0:RW_W])
    kr = shifted(k_ref, kp_ref, mu[:, RW_W:2 * RW_W])
    vr = shifted(v_ref, vp_ref, mu[:, 2 * RW_W:3 * RW_W])
    xs = shifted(x_ref, xp_ref, mu[:, 3 * RW_W:])

    z = w0_ref[...] + _dot3(jnp.tanh(xs), None, (w2_ref[0], w2_ref[1]))
    nz = -z
    softplus = jnp.maximum(nz, 0.0) + jnp.log(1.0 + jnp.exp(-jnp.abs(nz)))
    w_log = -softplus - 0.5
    a = jax.nn.sigmoid(a0_ref[...] + _dot3(xs, None, (a2_ref[0], a2_ref[1])))
    g = _dot3(jax.nn.sigmoid(xs), None, (g2_ref[0], g2_ref[1]))

    ones = _group_ones(LANES, RW_DIM)
    kkr = kr * kk_ref[...]
    sq = kkr * kkr
    ss = jnp.concatenate([_dot_exact_rhs(sq[:, c * LANES:(c + 1) * LANES], ones)
                          for c in range(RW_PAIRS)], axis=1)
    kk = kkr / jnp.maximum(jnp.sqrt(ss), 1e-12)
    _put_blocks(ro, rr)
    _put_blocks(lwo, -jnp.exp(w_log))
    _put_blocks(ko, kr * (1.0 + (a - 1.0) * ka_ref[...]))
    _put_blocks(vo, vr)
    _put_blocks(nao, -kk)
    _put_blocks(bo, kk * a)
    _put_blocks(go, g)


def _rwprep(p, cb0, s, mu, w0, a0, k_k, k_a, w2, a2, g2, tm):
    t = p.shape[1]
    lr = mu.shape[0] - 3 * RW_W
    nx = lr // LANES
    hilo = lambda w_: jnp.stack(_split(w_))
    w2p = hilo(jnp.zeros((lr, RW_W), F32).at[0:w2.shape[0]].set(w2))
    a2p = hilo(jnp.zeros((lr, RW_W), F32).at[w2.shape[0]:w2.shape[0] + a2.shape[0]].set(a2))
    g2p = hilo(jnp.zeros((lr, RW_W), F32).at[lr - g2.shape[0]:].set(g2))
    r8 = tm // 8
    gb = cb0 // RW_PAIRS
    xb = (cb0 + 3 * RW_PAIRS) // nx

    def cur(c, n):
        return pl.BlockSpec((n, tm, LANES), lambda i: (c, i, 0))

    def prev(c, n):
        return pl.BlockSpec((n, 8, LANES), lambda i: (c, jnp.maximum(i * r8 - 1, 0), 0))

    def full(shape):
        return pl.BlockSpec(shape, lambda i: (0,) * len(shape))

    vec = lambda a_: a_.astype(F32).reshape(1, -1)
    out = jax.ShapeDtypeStruct((RW_PAIRS, t, LANES), F32)
    return pl.pallas_call(
        functools.partial(_rwprep_kernel, rows_per_seq=s),
        grid=(t // tm,),
        in_specs=[cur(gb, RW_PAIRS), cur(gb + 1, RW_PAIRS), cur(gb + 2, RW_PAIRS), cur(xb, nx),
                  prev(gb, RW_PAIRS), prev(gb + 1, RW_PAIRS), prev(gb + 2, RW_PAIRS), prev(xb, nx),
                  full((1, 3 * RW_W + lr)), full((1, RW_W)), full((1, RW_W)), full((1, RW_W)),
                  full((1, RW_W)), full((2, lr, RW_W)), full((2, lr, RW_W)), full((2, lr, RW_W))],
        out_specs=[pl.BlockSpec((RW_PAIRS, tm, LANES), lambda i: (0, i, 0))] * 7,
        out_shape=[out] * 7,
        compiler_params=_params("parallel"),
    )(p, p, p, p, p, p, p, p, vec(mu), vec(w0), vec(a0), vec(k_k), vec(k_a), w2p, a2p, g2p)


def _rw_out_block(y, r, k, v, g, lng, lnb, rk, ones):
    mu = _dot_exact_rhs(y, ones) * (1.0 / RW_DIM)
    yc = y - mu
    var = _dot_exact_rhs(yc * yc, ones) * (1.0 / RW_DIM)
    out = yc * lax.rsqrt(var + RW_LN_EPS) * lng + lnb
    bonus = _dot_exact_rhs(r * k * rk, ones)
    return (out + bonus * v) * g


def _rwkv_kernel(r_ref, lw_ref, k_ref, v_ref, a_ref, b_ref, g_ref, lng_ref, lnb_ref, rk_ref, o_ref,
                 y_ref, rh_ref, yh_ref, p_ref, q_ref, st_ref, *, nchunk, unroll):
    L = CHUNK
    W = 2 * L
    npair = r_ref.shape[0]

    @pl.when(pl.program_id(1) == 0)
    def _():
        st_ref[...] = jnp.zeros(st_ref.shape, F32)

    lane = lax.broadcasted_iota(jnp.int32, (L, W), 1)
    rowi = lax.broadcasted_iota(jnp.int32, (L, W), 0)
    strict = (lane % L) < rowi
    incl = (lane % L) <= rowi
    rr = lax.broadcasted_iota(jnp.int32, (W, W), 0)
    cc = lax.broadcasted_iota(jnp.int32, (W, W), 1)
    same = (rr // L) == (cc // L)
    eye = rr == cc
    tl = lax.broadcasted_iota(jnp.int32, (L, L), 0)
    sl = lax.broadcasted_iota(jnp.int32, (L, L), 1)
    tril = (sl <= tl).astype(BF16)

    def bd(x):
        return jnp.where(same, jnp.concatenate([x, x], axis=0), 0.0)

    def group(gi, carry):
        us = [gi * unroll + i for i in range(unroll)]
        prs = [u // nchunk for u in us]
        rws = [pl.ds(pl.multiple_of((u % nchunk) * L, L), L) for u in us]
        G = range(unroll)
        ld = lambda ref: [ref[prs[i], rws[i], :] for i in G]
        r, lw, k, v, a, b = ld(r_ref), ld(lw_ref), ld(k_ref), ld(v_ref), ld(a_ref), ld(b_ref)

        def csum(x):
            hi, lo = _split(x)
            rest = x - hi.astype(F32) - lo.astype(F32)
            return jnp.concatenate([hi, lo, rest.astype(BF16)], axis=1)

        c3 = [_dot(tril, csum(lw[i])) for i in G]
        cin = [c[:, 0:W] + c[:, W:2 * W] + c[:, 2 * W:] for c in c3]
        clast = [c[L - 1:L, :] for c in cin]
        g_inv = [jnp.exp(-c) for c in cin]
        g_tail = [jnp.exp(clast[i] - cin[i]) for i in G]
        at = [a[i] * jnp.exp(cin[i] - lw[i]) for i in G]
        rt = [r[i] * jnp.exp(cin[i]) for i in G]
        abk = [_dot_nt(jnp.concatenate([at[i], rt[i]], axis=0),
                       jnp.concatenate([bd(b[i] * g_inv[i]), bd(k[i] * g_inv[i])], axis=0)) for i in G]
        a_ab = [jnp.where(strict, m[0:L, 0:W], 0.0) for m in abk]
        a_rb = [jnp.where(incl, m[L:W, 0:W], 0.0) for m in abk]
        a_ak = [jnp.where(strict, m[0:L, W:], 0.0) for m in abk]
        a_rk = [jnp.where(incl, m[L:W, W:], 0.0) for m in abk]
        n = [bd(m) for m in a_ab]
        tm = [jnp.where(eye, 1.0, m) for m in n]
        x = [_dot(m, m) for m in n]
        for j in range(5):
            if j < 4:
                xx = [_dot(x[i], jnp.concatenate([x[i], tm[i]], axis=1)) for i in G]
                x = [m[:, 0:W] for m in xx]
                tm = [tm[i] + xx[i][:, W:] for i in G]
            else:
                tm = [tm[i] + _dot(x[i], tm[i]) for i in G]
        v_bd = [bd(m) for m in v]
        kv = [_dot(jnp.concatenate([bd(a_ak[i]), a_rk[i]], axis=0), v_bd[i]) for i in G]
        au = [_dot(tm[i], jnp.concatenate([bd(at[i]), kv[i][0:W]], axis=1)) for i in G]
        ry = [_dot(a_rb[i], au[i]) for i in G]
        zero = jnp.zeros((W, W), F32)
        pq = [_dot_tn(jnp.concatenate([bd(b[i] * g_tail[i]), bd(k[i] * g_tail[i])], axis=0),
                      jnp.concatenate([au[i], jnp.concatenate([zero, v_bd[i]], axis=1)], axis=0))
              for i in G]
        for i in G:
            rh_ref[prs[i], rws[i], :] = rt[i] + ry[i][:, 0:W]
            yh_ref[prs[i], rws[i], :] = ry[i][:, W:] + kv[i][W:]
            p_ref[us[i]] = jnp.where(eye, jnp.exp(clast[i]), 0.0) + pq[i][:, 0:W]
            q_ref[us[i]] = pq[i][:, W:]
        return carry

    lax.fori_loop(0, npair * nchunk // unroll, group, 0)

    def step(c, carry):
        rows = pl.ds(pl.multiple_of(c * L, L), L)
        for pr in range(npair):
            st = st_ref[pr].astype(BF16)
            y_ref[pr, rows, :] = _dot(rh_ref[pr, rows, :], st) + yh_ref[pr, rows, :]
            st_ref[pr] = _dot(p_ref[pr * nchunk + c], st) + q_ref[pr * nchunk + c]
        return carry

    lax.fori_loop(0, nchunk, step, 0)

    ones = _group_ones(LANES, RW_DIM)
    for pr in range(npair):
        cols = slice(pr * LANES, (pr + 1) * LANES)
        o_ref[pr] = _rw_out_block(y_ref[pr], r_ref[pr], k_ref[pr], v_ref[pr], g_ref[pr], lng_ref[:, cols],
                                  lnb_ref[:, cols], rk_ref[:, cols], ones).astype(o_ref.dtype)


def _rwkv(r, lw, k, v, na, bb, g, ln_g, ln_b, r_k, b, s, sb, unroll):
    npair, t, w = r.shape
    nchunk = sb // CHUNK
    nsb = s // sb
    spec = pl.BlockSpec((npair, sb, w), lambda bi, si: (0, bi * nsb + si, 0))
    small = pl.BlockSpec((1, RW_W), lambda bi, si: (0, 0))
    vec = lambda a_: a_.astype(F32).reshape(1, RW_W)
    return pl.pallas_call(
        functools.partial(_rwkv_kernel, nchunk=nchunk, unroll=unroll),
        grid=(b, nsb),
        in_specs=[spec] * 7 + [small] * 3,
        out_specs=spec,
        out_shape=jax.ShapeDtypeStruct((npair, t, w), BF16),
        scratch_shapes=[pltpu.VMEM((npair, sb, w), F32),
                        pltpu.VMEM((npair, sb, w), F32), pltpu.VMEM((npair, sb, w), F32),
                        pltpu.VMEM((npair * nchunk, w, w), F32), pltpu.VMEM((npair * nchunk, w, w), F32),
                        pltpu.VMEM((npair, w, w), F32)],
        compiler_params=_params("parallel", "arbitrary"),
    )(r, lw, k, v, na, bb, g, vec(ln_g), vec(ln_b), vec(r_k))


def _outproj_kernel(x_ref, oa_ref, oc_ref, ob_ref, w_ref, o_ref):
    mix = jnp.concatenate([_cat_blocks(oa_ref, oa_ref.shape[0]), _cat_blocks(oc_ref, oc_ref.shape[0]),
                           _cat_blocks(ob_ref, ob_ref.shape[0])], axis=1)
    o_ref[...] = x_ref[...] + jnp.dot(mix, w_ref[...], preferred_element_type=F32)


def _outproj(x, oa, oc, ob, w, li, tm, tn):
    t, d = x.shape
    blocks = lambda a_: pl.BlockSpec((a_.shape[0], tm, LANES), lambda i, j: (0, i, 0))
    return pl.pallas_call(
        _outproj_kernel,
        grid=(t // tm, d // tn),
        in_specs=[pl.BlockSpec((tm, tn), lambda i, j: (i, j)),
                  blocks(oa), blocks(oc), blocks(ob),
                  pl.BlockSpec((None, w.shape[1], tn), lambda i, j: (li, 0, j))],
        out_specs=pl.BlockSpec((tm, tn), lambda i, j: (i, j)),
        out_shape=jax.ShapeDtypeStruct((t, d), F32),
        compiler_params=_params("parallel", "arbitrary"),
    )(x, oa, oc, ob, w)


FFN_SPLIT = 2


def _swiglu_rows(x_ref, wg_ref, wu_ref, wd_ref):
    n = x_ref.shape[0] // FFN_SPLIT
    sl = [slice(i * n, (i + 1) * n) for i in range(FFN_SPLIT)]
    wg, wu, wd = wg_ref[...], wu_ref[...], wd_ref[...]
    gu = [(jnp.dot(x_ref[r, :], wg, preferred_element_type=F32),
           jnp.dot(x_ref[r, :], wu, preferred_element_type=F32)) for r in sl]
    act = [(g * jax.nn.sigmoid(g) * u).astype(BF16) for g, u in gu]
    return [(r, jnp.dot(a, wd, preferred_element_type=F32)) for r, a in zip(sl, act)]


def _ffn_kernel(x_ref, g_ref, wg_ref, wu_ref, wd_ref, o_ref, xn_ref, acc_ref):
    f = pl.program_id(1)

    @pl.when(f == 0)
    def _():
        xn_ref[...] = _rms(x_ref[...], g_ref[...]).astype(BF16)
        acc_ref[...] = jnp.zeros(acc_ref.shape, F32)

    for r, y in _swiglu_rows(xn_ref, wg_ref, wu_ref, wd_ref):
        acc_ref[r, :] += y

    @pl.when(f == pl.num_programs(1) - 1)
    def _():
        o_ref[...] = x_ref[...] + acc_ref[...]


def _ffn(x, g, wg, wu, wd, li, tm, tf):
    t, d = x.shape
    ff = wg.shape[2]
    return pl.pallas_call(
        _ffn_kernel,
        grid=(t // tm, ff // tf),
        in_specs=[pl.BlockSpec((tm, d), lambda i, f: (i, 0)),
                  pl.BlockSpec((1, d), lambda i, f: (0, 0)),
                  pl.BlockSpec((None, d, tf), lambda i, f: (li, 0, f)),
                  pl.BlockSpec((None, d, tf), lambda i, f: (li, 0, f)),
                  pl.BlockSpec((None, tf, d), lambda i, f: (li, f, 0))],
        out_specs=pl.BlockSpec((tm, d), lambda i, f: (i, 0)),
        out_shape=jax.ShapeDtypeStruct((t, d), F32),
        scratch_shapes=[pltpu.VMEM((tm, d), BF16), pltpu.VMEM((tm, d), F32)],
        compiler_params=_params("parallel", "arbitrary"),
    )(x, g.reshape(1, d), wg, wu, wd)


def _router_kernel(x_ref, g_ref, wr_ref, h_ref, comb_ref, combt_ref, rcol_ref, rrow_ref, cnt_ref,
                   *, n_experts):
    h = _rms(x_ref[...], g_ref[...])
    h_ref[...] = h.astype(BF16)
    logits = _dot3(h, wr_ref[...])
    lane = lax.broadcasted_iota(jnp.int32, logits.shape, 1)
    lg = jnp.where(lane < n_experts, logits, NEG)
    m1 = jnp.max(lg, axis=-1, keepdims=True)
    i1 = jnp.min(jnp.where(lg == m1, lane, LANES), axis=-1, keepdims=True)
    lg2 = jnp.where(lane == i1, NEG, lg)
    m2 = jnp.max(lg2, axis=-1, keepdims=True)
    i2 = jnp.min(jnp.where(lg2 == m2, lane, LANES), axis=-1, keepdims=True)
    e2 = jnp.exp(m2 - m1)
    w1 = 1.0 / (1.0 + e2)
    comb = jnp.where(lane == i1, w1, 0.0) + jnp.where(lane == i2, e2 * w1, 0.0)
    combt = comb.T[0:combt_ref.shape[0], :]
    comb_ref[...] = comb
    combt_ref[...] = combt
    ts = comb.shape[0]
    tt = lax.broadcasted_iota(jnp.int32, (ts, ts), 0)
    uu = lax.broadcasted_iota(jnp.int32, (ts, ts), 1)
    live = jnp.where(comb > 0.0, 1.0, 0.0)
    rcol_ref[...] = _dot((uu < tt).astype(BF16), live)
    rrow_ref[...] = _dot(jnp.where(combt > 0.0, 1.0, 0.0), (tt < uu).astype(BF16))
    cnt_ref[0] = jnp.sum(live, axis=0, keepdims=True).astype(jnp.int32)


def _router(x, g, wr, ts):
    t, d = x.shape
    e = wr.shape[1]
    ep = max(8, e)
    wrp = jnp.zeros((d, LANES), F32).at[:, :e].set(wr)
    nt = t // ts
    return pl.pallas_call(
        functools.partial(_router_kernel, n_experts=e),
        grid=(nt,),
        in_specs=[pl.BlockSpec((ts, d), lambda i: (i, 0)),
                  pl.BlockSpec((1, d), lambda i: (0, 0)),
                  pl.BlockSpec((d, LANES), lambda i: (0, 0))],
        out_specs=[pl.BlockSpec((ts, d), lambda i: (i, 0)),
                   pl.BlockSpec((ts, LANES), lambda i: (i, 0)),
                   pl.BlockSpec((ep, ts), lambda i: (0, i)),
                   pl.BlockSpec((ts, LANES), lambda i: (i, 0)),
                   pl.BlockSpec((ep, ts), lambda i: (0, i)),
                   pl.BlockSpec((1, 1, LANES), lambda i: (i, 0, 0))],
        out_shape=[jax.ShapeDtypeStruct((t, d), BF16),
                   jax.ShapeDtypeStruct((t, LANES), F32),
                   jax.ShapeDtypeStruct((ep, t), F32),
                   jax.ShapeDtypeStruct((t, LANES), F32),
                   jax.ShapeDtypeStruct((ep, t), F32),
                   jax.ShapeDtypeStruct((nt, 1, LANES), jnp.int32)],
        compiler_params=_params("parallel"),
    )(x, g.reshape(1, d), wrp)


MOE_BM = 256
MOE_BMF = 512
NO_MATCH = -(1 << 20)


def _moe_plan(counts, nt, ne, t, bm, bmf):
    i32 = jnp.int32
    cnt = counts.reshape(nt, ne).astype(i32)
    tot = jnp.sum(cnt, axis=0)
    ptot = (tot + bmf - 1) // bmf * bmf
    eend = jnp.cumsum(ptot)
    ebase = eend - ptot
    seg0 = ebase[None, :] + jnp.cumsum(cnt, axis=0) - cnt
    seg1 = seg0 + cnt
    nrows = 2 * t + ne * bmf
    nblk = nrows // bmf
    count_le = lambda ends, v: jnp.sum((ends[None, :] <= v[:, None]).astype(i32), axis=1)
    blk_exp = jnp.minimum(count_le(eend, jnp.arange(nblk, dtype=i32) * bmf), ne - 1)
    nvalid = (eend[-1] // bmf).astype(i32).reshape(1)
    npairs = nrows // bm + nt * ne
    g = jnp.arange(npairs, dtype=i32)

    def pairs(c0, c1, seg_tile, seg_exp, dummy_blk):
        n = jnp.where(c1 > c0, (c1 - 1) // bm - c0 // bm + 1, 0)
        pend = jnp.cumsum(n)
        k = jnp.minimum(count_le(pend, g), c0.shape[0] - 1)
        valid = g < pend[-1]
        blk = c0[k] // bm + (g - (pend[k] - n[k]))
        blk = jnp.where(valid, blk, dummy_blk)
        delta = jnp.where(valid, c0[k] - blk * bm, NO_MATCH)
        return (blk.astype(i32), jnp.where(valid, seg_tile[k], nt - 1).astype(i32),
                seg_exp[k].astype(i32), delta.astype(i32))

    tiles = jnp.arange(nt, dtype=i32)
    exps = jnp.arange(ne, dtype=i32)
    c1g = seg1.at[nt - 1].set(eend)
    g_blk, g_tile, g_exp, g_delta = pairs(seg0.T.reshape(-1), c1g.T.reshape(-1),
                                          jnp.tile(tiles, ne), jnp.repeat(exps, nt), nrows // bm)
    g_first = jnp.concatenate([jnp.ones((1,), i32), (g_blk[1:] != g_blk[:-1]).astype(i32)])
    s_blk, s_tile, s_exp, s_delta = pairs(seg0.reshape(-1), seg1.reshape(-1),
                                          jnp.repeat(tiles, ne), jnp.tile(exps, nt), 0)
    s_first = jnp.concatenate([jnp.ones((1,), i32), (s_tile[1:] != s_tile[:-1]).astype(i32)])
    return dict(nrows=nrows, blk_exp=blk_exp, nvalid=nvalid,
                gather=(g_blk, g_tile, g_exp, g_delta, g_first),
                combine=(s_blk, s_tile, s_exp, s_delta, s_first))


def _moe_gather_kernel(blk_ref, tile_ref, exp_ref, delta_ref, first_ref, h_ref, rrow_ref, combt_ref,
                       o_ref):
    g = pl.program_id(0)
    e = exp_ref[g]
    bm, ts = o_ref.shape[0], h_ref.shape[0]
    pos = rrow_ref[pl.ds(e, 1), :] + delta_ref[g].astype(F32)
    live = combt_ref[pl.ds(e, 1), :] > 0.0
    slot = lax.broadcasted_iota(jnp.int32, (bm, ts), 0).astype(F32)
    onehot = jnp.where((pos == slot) & live, 1.0, 0.0).astype(BF16)
    val = jnp.dot(onehot, h_ref[...], preferred_element_type=F32).astype(BF16)

    @pl.when(first_ref[g] == 1)
    def _():
        o_ref[...] = val

    @pl.when(first_ref[g] == 0)
    def _():
        o_ref[...] += val


def _moe_gather(plan, h, rrow, combt, ts, bm):
    t, d = h.shape
    ep = combt.shape[0]
    blk, tile, exp, delta, first = plan["gather"]
    grid_spec = pltpu.PrefetchScalarGridSpec(
        num_scalar_prefetch=5,
        grid=(blk.shape[0],),
        in_specs=[pl.BlockSpec((ts, d), lambda g, b_, t_, e_, d_, f_: (t_[g], 0)),
                  pl.BlockSpec((ep, ts), lambda g, b_, t_, e_, d_, f_: (0, t_[g])),
                  pl.BlockSpec((ep, ts), lambda g, b_, t_, e_, d_, f_: (0, t_[g]))],
        out_specs=pl.BlockSpec((bm, d), lambda g, b_, t_, e_, d_, f_: (b_[g], 0)),
    )
    return pl.pallas_call(
        _moe_gather_kernel,
        grid_spec=grid_spec,
        out_shape=jax.ShapeDtypeStruct((plan["nrows"] + MOE_BMF, d), BF16),
        compiler_params=_params("arbitrary"),
    )(blk, tile, exp, delta, first, h, rrow, combt)


def _moe_ffn_kernel(bexp_ref, nv_ref, x_ref, wg_ref, wu_ref, wd_ref, o_ref, acc_ref):
    i, f = pl.program_id(0), pl.program_id(1)
    valid = i < nv_ref[0]

    @pl.when(valid)
    def _():
        @pl.when(f == 0)
        def _():
            acc_ref[...] = jnp.zeros(acc_ref.shape, F32)

        for r, y in _swiglu_rows(x_ref, wg_ref, wu_ref, wd_ref):
            acc_ref[r, :] += y

    @pl.when(f == pl.num_programs(1) - 1)
    def _():
        @pl.when(valid)
        def _():
            o_ref[...] = acc_ref[...].astype(BF16)

        @pl.when(jnp.logical_not(valid))
        def _():
            o_ref[...] = jnp.zeros(o_ref.shape, BF16)


def _moe_ffn(plan, xs, wg, wu, wd, li, bmf, tf):
    d = xs.shape[1]
    ff = wg.shape[3]
    nblk = plan["blk_exp"].shape[0]
    grid_spec = pltpu.PrefetchScalarGridSpec(
        num_scalar_prefetch=2,
        grid=(nblk, ff // tf),
        in_specs=[pl.BlockSpec((bmf, d), lambda i, f, be, nv: (i, 0)),
                  pl.BlockSpec((None, None, d, tf), lambda i, f, be, nv: (li, be[i], 0, f)),
                  pl.BlockSpec((None, None, d, tf), lambda i, f, be, nv: (li, be[i], 0, f)),
                  pl.BlockSpec((None, None, tf, d), lambda i, f, be, nv: (li, be[i], f, 0))],
        out_specs=pl.BlockSpec((bmf, d), lambda i, f, be, nv: (i, 0)),
        scratch_shapes=[pltpu.VMEM((bmf, d), F32)],
    )
    return pl.pallas_call(
        _moe_ffn_kernel,
        grid_spec=grid_spec,
        out_shape=jax.ShapeDtypeStruct((nblk * bmf, d), BF16),
        compiler_params=_params("parallel", "arbitrary"),
    )(plan["blk_exp"], plan["nvalid"], xs, wg, wu, wd)


def _moe_combine_kernel(blk_ref, tile_ref, exp_ref, delta_ref, first_ref, y_ref, x_ref, comb_ref,
                        rcol_ref, o_ref):
    g = pl.program_id(0)
    e = exp_ref[g]
    ts, bm = x_ref.shape[0], y_ref.shape[0]

    @pl.when(first_ref[g] == 1)
    def _():
        o_ref[...] = x_ref[...]

    lane = lax.broadcasted_iota(jnp.int32, comb_ref.shape, 1)
    sel = lane == e
    cw = jnp.sum(jnp.where(sel, comb_ref[...], 0.0), axis=-1, keepdims=True)
    pos = (jnp.sum(jnp.where(sel, rcol_ref[...], 0.0), axis=-1, keepdims=True)
           + delta_ref[g].astype(F32))
    slot = lax.broadcasted_iota(jnp.int32, (ts, bm), 1).astype(F32)
    onehot = jnp.where((pos == slot) & (cw > 0.0), 1.0, 0.0).astype(BF16)
    o_ref[...] += cw * jnp.dot(onehot, y_ref[...], preferred_element_type=F32)


def _moe_combine(plan, ys, x, comb, rcol, ts, bm):
    t, d = x.shape
    blk, tile, exp, delta, first = plan["combine"]
    grid_spec = pltpu.PrefetchScalarGridSpec(
        num_scalar_prefetch=5,
        grid=(blk.shape[0],),
        in_specs=[pl.BlockSpec((bm, d), lambda g, b_, t_, e_, d_, f_: (b_[g], 0)),
                  pl.BlockSpec((ts, d), lambda g, b_, t_, e_, d_, f_: (t_[g], 0)),
                  pl.BlockSpec((ts, LANES), lambda g, b_, t_, e_, d_, f_: (t_[g], 0)),
                  pl.BlockSpec((ts, LANES), lambda g, b_, t_, e_, d_, f_: (t_[g], 0))],
        out_specs=pl.BlockSpec((ts, d), lambda g, b_, t_, e_, d_, f_: (t_[g], 0)),
    )
    return pl.pallas_call(
        _moe_combine_kernel,
        grid_spec=grid_spec,
        out_shape=jax.ShapeDtypeStruct((t, d), F32),
        compiler_params=_params("arbitrary"),
    )(blk, tile, exp, delta, first, ys, x, comb, rcol)


def _moe(x, h, comb, combt, rcol, rrow, counts, wg, wu, wd, li, ts, tf, bm=MOE_BM, bmf=MOE_BMF):
    t = x.shape[0]
    plan = _moe_plan(counts, t // ts, wg.shape[1], t, bm, bmf)
    xs = _moe_gather(plan, h, rrow, combt, ts, bm)
    ys = _moe_ffn(plan, xs, wg, wu, wd, li, bmf, tf)
    return _moe_combine(plan, ys, x, comb, rcol, ts, bm)


def kernel(x, norm1_g, w_in, da_q_norm, da_k_norm, da_lambda, da_out_norm, dl_q_norm, dl_k_norm, rw_mu, rw_w0, rw_w2, rw_a0, rw_a2, rw_g2, rw_k_k, rw_k_a, rw_r_k, rw_ln_g, rw_ln_b, w_out, norm2_g, ffn_w_gate, ffn_w_up, ffn_w_down, moe_router, moe_w_gate, moe_w_up, moe_w_down):
    b, s, d = x.shape
    depth = w_in.shape[0]
    t = b * s
    n_experts = moe_router.shape[-1]
    xt = x.reshape(t, d)

    qa, ka, va = 0, DA_HEADS, 2 * DA_HEADS
    qb, kb, vb = 3 * DA_HEADS, 3 * DA_HEADS + DL_HEADS, 3 * DA_HEADS + 2 * DL_HEADS
    rw = 3 * DA_HEADS + 3 * DL_HEADS

    tm = min(1024, t)
    tq = min(256, s)
    tf = 512
    assert ffn_w_gate.shape[2] % tf == 0 and moe_w_gate.shape[3] % tf == 0

    w_in, w_out = w_in.astype(BF16), w_out.astype(BF16)
    ffn_w = [w.astype(BF16) for w in (ffn_w_gate, ffn_w_up, ffn_w_down)]
    moe_w = [w.astype(BF16) for w in (moe_w_gate, moe_w_up, moe_w_down)]

    for l in range(depth):
        p = _inproj(xt, norm1_g[l], w_in, l, tm, 1280)

        lam_init = 0.8 - 0.6 * math.exp(-0.3 * l)
        oa = _diffattn(p, qa, ka, va, da_q_norm[l], da_k_norm[l], da_lambda[l], da_out_norm[l],
                       b, s, lam_init, tq)
        oc = _dilattn(p, qb, kb, vb, dl_q_norm[l], dl_k_norm[l], b, s, tq)

        r, lw, k2, v, na, bb, g = _rwprep(p, rw, s, rw_mu[l], rw_w0[l], rw_a0[l], rw_k_k[l],
                                          rw_k_a[l], rw_w2[l], rw_a2[l], rw_g2[l], min(256, s))
        ob = _rwkv(r, lw, k2, v, na, bb, g, rw_ln_g[l], rw_ln_b[l], rw_r_k[l].reshape(-1),
                   b, s, min(512, s), 16)

        xt = _outproj(xt, oa, oc, ob, w_out, l, tm, 1024)

        i = l // 2
        if l % 2 == 0:
            xt = _ffn(xt, norm2_g[l], *ffn_w, i, min(512, t), tf)
        else:
            ts = min(1024, t)
            h, comb, combt, rcol, rrow, cnt = _router(xt, norm2_g[l], moe_router[i], ts)
            counts = cnt[:, 0, :n_experts].reshape(-1)
            xt = _moe(xt, h, comb, combt, rcol, rrow, counts, *moe_w, i, ts, tf)
    return xt.reshape(b, s, d)
```

```python
import functools
import math

import numpy as np
import jax
import jax.numpy as jnp
from jax import lax
from jax.experimental import pallas as pl
from jax.experimental.pallas import tpu as pltpu

F32 = jnp.float32
BF16 = jnp.bfloat16

LANES = 128
VMEM_LIMIT = 56 * 1024 * 1024

NEG = -1e30
ROPE_THETA = 10000.0
NORM_EPS = 1e-6
RW_LN_EPS = 64e-5
DL_PATTERNS = ((128, 1), (512, 4), (2048, 16))
TOP_K = 2

DA_HEADS, DA_QK = 4, 64
DL_HEADS = 6
RW_HEADS, RW_DIM = 12, 64
DA_W, DL_W, RW_W = 512, 768, 768
RW_PAIRS = RW_W // LANES
CHUNK = 64


def _params(*sem):
    return pltpu.CompilerParams(dimension_semantics=sem, vmem_limit_bytes=VMEM_LIMIT)


def _dot(a, b):
    return jnp.dot(a.astype(BF16), b.astype(BF16), preferred_element_type=F32)


def _dot_nt(a, b):
    return lax.dot_general(a.astype(BF16), b.astype(BF16), (((1,), (1,)), ((), ())),
                           preferred_element_type=F32)


def _dot_tn(a, b):
    return lax.dot_general(a.astype(BF16), b.astype(BF16), (((0,), (0,)), ((), ())),
                           preferred_element_type=F32)


def _split(x):
    hi = x.astype(BF16)
    lo = (x - hi.astype(F32)).astype(BF16)
    return hi, lo


def _dot3(a, b, b_parts=None):
    ah, al = _split(a)
    bh, bl = _split(b) if b_parts is None else b_parts
    return _dot(ah, bh) + _dot(ah, bl) + _dot(al, bh)


def _dot_exact_rhs(a, b_bf16):
    ah, al = _split(a)
    return _dot(ah, b_bf16) + _dot(al, b_bf16)


def _group_ones(width, group):
    i = lax.broadcasted_iota(jnp.int32, (width, width), 0) // group
    j = lax.broadcasted_iota(jnp.int32, (width, width), 1) // group
    return (i == j).astype(BF16)


def _rms(x, g):
    return x * lax.rsqrt(jnp.mean(x * x, axis=-1, keepdims=True) + NORM_EPS) * g


def _cat_blocks(ref, n):
    return jnp.concatenate([ref[c] for c in range(n)], axis=1)


def _put_blocks(ref, val):
    for c in range(ref.shape[0]):
        ref[c] = val[:, c * LANES:(c + 1) * LANES].astype(ref.dtype)


def _inproj_kernel(x_ref, g_ref, w_ref, o_ref, xn_ref):
    @pl.when(pl.program_id(1) == 0)
    def _():
        xn_ref[...] = _rms(x_ref[...], g_ref[...]).astype(BF16)

    _put_blocks(o_ref, jnp.dot(xn_ref[...], w_ref[...], preferred_element_type=F32))


def _inproj(x, g, w, li, tm, tn):
    t, d = x.shape
    n = w.shape[2]
    return pl.pallas_call(
        _inproj_kernel,
        grid=(t // tm, n // tn),
        in_specs=[pl.BlockSpec((tm, d), lambda i, j: (i, 0)),
                  pl.BlockSpec((1, d), lambda i, j: (0, 0)),
                  pl.BlockSpec((None, d, tn), lambda i, j: (li, 0, j))],
        out_specs=pl.BlockSpec((tn // LANES, tm, LANES), lambda i, j: (j, i, 0)),
        out_shape=jax.ShapeDtypeStruct((n // LANES, t, LANES), F32),
        scratch_shapes=[pltpu.VMEM((tm, d), BF16)],
        compiler_params=_params("parallel", "arbitrary"),
    )(x, g.reshape(1, d), w)


LOG2E = math.log2(math.e)


def _prep_qk(x, gain, cos, sin, group, post=1.0):
    ones = _group_ones(LANES, group)
    ms = _dot_exact_rhs(x * x, ones) * (1.0 / group)
    y = x * lax.rsqrt(ms + NORM_EPS) * gain
    half = group // 2
    if group == LANES:
        partner = pltpu.roll(y, half, axis=1)
    else:
        lane = lax.broadcasted_iota(jnp.int32, y.shape, 1)
        fwd = pltpu.roll(y, LANES - half, axis=1)
        bwd = pltpu.roll(y, half, axis=1)
        partner = jnp.where((lane % group) < half, fwd, bwd)
    return ((y * cos + partner * sin) * post).astype(BF16)


def _rope_tables(s, group):
    half = group // 2
    lane = np.arange(LANES)
    inv = ROPE_THETA ** (-jnp.asarray(lane % half, F32) / half)
    ang = jnp.arange(s, dtype=F32)[:, None] * inv[None, :]
    sign = jnp.asarray(np.where((lane % group) < half, -1.0, 1.0), F32)
    return jnp.cos(ang), jnp.sin(ang) * sign[None, :]


def _tile_gain(gain, group):
    return jnp.tile(gain.astype(F32), LANES // group).reshape(1, LANES)


def _block_pairs(nq):
    return [(j, qi) for j in range(nq) for qi in range(j, nq)]


def _vt_blocks(v_ref, tq, nq):
    return [v_ref[j * tq:(j + 1) * tq, :].T.astype(BF16) for j in range(nq)]


def _diffattn_kernel(q_ref, k_ref, v_ref, cos_ref, sin_ref, gq_ref, gk_ref, lam_ref, go_ref, o_ref,
                     s0_ref, s1_ref, *, tq, scale, lam_init):
    nq = q_ref.shape[0] // tq
    cos, sin = cos_ref[...], sin_ref[...]
    kp = _prep_qk(k_ref[...], gk_ref[...], cos, sin, DA_QK)
    qp = _prep_qk(q_ref[...], gq_ref[...], cos, sin, DA_QK, scale * LOG2E)
    lane = lax.broadcasted_iota(jnp.int32, qp.shape, 1)
    zero = jnp.zeros_like(qp)
    qs = (jnp.where(lane < DA_QK, qp, zero), jnp.where(lane >= DA_QK, qp, zero))
    vt = _vt_blocks(v_ref, tq, nq)
    blk = lambda x, i: x[i * tq:(i + 1) * tq, :]
    causal = (lax.broadcasted_iota(jnp.int32, (tq, tq), 1)
              >= lax.broadcasted_iota(jnp.int32, (tq, tq), 0))
    pairs = _block_pairs(nq)
    s_refs = (s0_ref, s1_ref)

    m = [[jnp.full((1, tq), NEG, F32) for _ in range(nq)] for _ in range(2)]
    for idx, (j, qi) in enumerate(pairs):
        for c in range(2):
            sc = _dot_nt(blk(kp, j), blk(qs[c], qi))
            if j == qi:
                sc = jnp.where(causal, sc, NEG)
            s_refs[c][idx] = sc
            m[c][qi] = jnp.maximum(m[c][qi], jnp.max(sc, axis=0, keepdims=True))

    l = [[jnp.zeros((1, tq), F32) for _ in range(nq)] for _ in range(2)]
    acc = [[None] * nq for _ in range(2)]
    for idx, (j, qi) in enumerate(pairs):
        for c in range(2):
            pr = jnp.exp2(s_refs[c][idx] - m[c][qi])
            l[c][qi] = l[c][qi] + jnp.sum(pr, axis=0, keepdims=True)
            d = _dot(vt[j], pr)
            acc[c][qi] = d if acc[c][qi] is None else acc[c][qi] + d

    lm = lam_ref[...]
    lam = (jnp.exp(jnp.sum(lm[0:1] * lm[1:2], axis=-1, keepdims=True))
           - jnp.exp(jnp.sum(lm[2:3] * lm[3:4], axis=-1, keepdims=True)) + lam_init)
    for qi in range(nq):
        ot = acc[0][qi] / l[0][qi] - lam * (acc[1][qi] / l[1][qi])
        ot = ot * lax.rsqrt(jnp.mean(ot * ot, axis=0, keepdims=True) + NORM_EPS)
        o_ref[qi * tq:(qi + 1) * tq, :] = (ot.T * go_ref[...] * (1.0 - lam_init)).astype(BF16)


def _diffattn(p, qcb, kcb, vcb, gq, gk, lam4, gout, b, s, lam_init, tq):
    t = p.shape[1]
    nq = s // tq
    npair = nq * (nq + 1) // 2
    cos, sin = _rope_tables(s, DA_QK)
    blk = lambda cb0: pl.BlockSpec((None, s, LANES), lambda bi, h: (cb0 + h, bi, 0))
    full = lambda shape: pl.BlockSpec(shape, lambda bi, h: (0, 0))
    return pl.pallas_call(
        functools.partial(_diffattn_kernel, tq=tq, scale=DA_QK ** -0.5, lam_init=lam_init),
        grid=(b, DA_HEADS),
        in_specs=[blk(qcb), blk(kcb), blk(vcb),
                  full((s, LANES)), full((s, LANES)), full((1, LANES)), full((1, LANES)),
                  full((4, DA_QK)), full((1, LANES))],
        out_specs=blk(0),
        out_shape=jax.ShapeDtypeStruct((DA_HEADS, t, LANES), BF16),
        scratch_shapes=[pltpu.VMEM((npair, tq, tq), F32), pltpu.VMEM((npair, tq, tq), F32)],
        compiler_params=_params("parallel", "parallel"),
    )(p, p, p, cos, sin, _tile_gain(gq, DA_QK), _tile_gain(gk, DA_QK), lam4.astype(F32),
      gout.astype(F32).reshape(1, LANES))


def _dilated_bias(s, tq):
    nd = s // tq
    d = (np.arange(nd)[:, None, None] * tq + np.arange(tq)[None, None, :]
         - np.arange(tq)[None, :, None])
    cnt = np.zeros(d.shape, np.float64)
    for window, dil in DL_PATTERNS:
        cnt += (d >= 0) & (d % dil == 0) & (d <= window)
    bias = np.where(cnt > 0, np.log2(np.maximum(cnt, 1.0)), NEG)
    return jnp.asarray(bias, F32)


def _dilattn_kernel(q_ref, k_ref, v_ref, cos_ref, sin_ref, gq_ref, gk_ref, bias_ref, o_ref, s_ref,
                    *, tq, scale):
    nq = q_ref.shape[0] // tq
    cos, sin = cos_ref[...], sin_ref[...]
    kp = _prep_qk(k_ref[...], gk_ref[...], cos, sin, LANES)
    qp = _prep_qk(q_ref[...], gq_ref[...], cos, sin, LANES, scale * LOG2E)
    vt = _vt_blocks(v_ref, tq, nq)
    blk = lambda x, i: x[i * tq:(i + 1) * tq, :]
    pairs = _block_pairs(nq)

    m = [jnp.full((1, tq), NEG, F32) for _ in range(nq)]
    for idx, (j, qi) in enumerate(pairs):
        sc = _dot_nt(blk(kp, j), blk(qp, qi)) + bias_ref[qi - j]
        s_ref[idx] = sc
        m[qi] = jnp.maximum(m[qi], jnp.max(sc, axis=0, keepdims=True))

    l = [jnp.zeros((1, tq), F32) for _ in range(nq)]
    acc = [None] * nq
    for idx, (j, qi) in enumerate(pairs):
        pr = jnp.exp2(s_ref[idx] - m[qi])
        l[qi] = l[qi] + jnp.sum(pr, axis=0, keepdims=True)
        d = _dot(vt[j], pr)
        acc[qi] = d if acc[qi] is None else acc[qi] + d

    for qi in range(nq):
        o_ref[qi * tq:(qi + 1) * tq, :] = (acc[qi] / l[qi]).T.astype(BF16)


def _dilattn(p, qcb, kcb, vcb, gq, gk, b, s, tq):
    t = p.shape[1]
    nq = s // tq
    cos, sin = _rope_tables(s, LANES)
    bias = _dilated_bias(s, tq)
    blk = lambda cb0: pl.BlockSpec((None, s, LANES), lambda bi, h: (cb0 + h, bi, 0))
    full = lambda shape: pl.BlockSpec(shape, lambda bi, h: (0,) * len(shape))
    return pl.pallas_call(
        functools.partial(_dilattn_kernel, tq=tq, scale=LANES ** -0.5),
        grid=(b, DL_HEADS),
        in_specs=[blk(qcb), blk(kcb), blk(vcb),
                  full((s, LANES)), full((s, LANES)), full((1, LANES)), full((1, LANES)),
                  full((nq, tq, tq))],
        out_specs=blk(0),
        out_shape=jax.ShapeDtypeStruct((DL_HEADS, t, LANES), BF16),
        scratch_shapes=[pltpu.VMEM((nq * (nq + 1) // 2, tq, tq), F32)],
        compiler_params=_params("parallel", "parallel"),
    )(p, p, p, cos, sin, _tile_gain(gq, LANES), _tile_gain(gk, LANES), bias)


def _rwprep_kernel(r_ref, k_ref, v_ref, x_ref, rp_ref, kp_ref, vp_ref, xp_ref,
                   mu_ref, w0_ref, a0_ref, kk_ref, ka_ref, w2_ref, a2_ref, g2_ref,
                   ro, lwo, ko, vo, nao, bo, go, *, rows_per_seq):
    i = pl.program_id(0)
    tm = r_ref.shape[1]
    first = (i * tm) % rows_per_seq == 0
    row = lax.broadcasted_iota(jnp.int32, (tm, 1), 0)

    def shifted(cur_ref, prev_ref, mu):
        n = cur_ref.shape[0]
        cur = _cat_blocks(cur_ref, n)
        last = jnp.concatenate([prev_ref[c, 7:8, :] for c in range(n)], axis=1)
        last = jnp.where(first, 0.0, last)
        prev = jnp.where(row == 0, last, pltpu.roll(cur, 1, axis=0))
        return cur + mu * (prev - cur)

    mu = mu_ref[...]
    rr = shifted(r_ref, rp_ref, mu[:, 0:RW_W])
    kr = shifted(k_ref, kp_ref, mu[:, RW_W:2 * RW_W])
    vr = shifted(v_ref, vp_ref, mu[:, 2 * RW_W:3 * RW_W])
    xs = shifted(x_ref, xp_ref, mu[:, 3 * RW_W:])

    z = w0_ref[...] + _dot3(jnp.tanh(xs), None, (w2_ref[0], w2_ref[1]))
    nz = -z
    softplus = jnp.maximum(nz, 0.0) + jnp.log(1.0 + jnp.exp(-jnp.abs(nz)))
    w_log = -softplus - 0.5
    a = jax.nn.sigmoid(a0_ref[...] + _dot3(xs, None, (a2_ref[0], a2_ref[1])))
    g = _dot3(jax.nn.sigmoid(xs), None, (g2_ref[0], g2_ref[1]))

    ones = _group_ones(LANES, RW_DIM)
    kkr = kr * kk_ref[...]
    sq = kkr * kkr
    ss = jnp.concatenate([_dot_exact_rhs(sq[:, c * LANES:(c + 1) * LANES], ones)
                          for c in range(RW_PAIRS)], axis=1)
    kk = kkr / jnp.maximum(jnp.sqrt(ss), 1e-12)
    _put_blocks(ro, rr)
    _put_blocks(lwo, -jnp.exp(w_log))
    _put_blocks(ko, kr * (1.0 + (a - 1.0) * ka_ref[...]))
    _put_blocks(vo, vr)
    _put_blocks(nao, -kk)
    _put_blocks(bo, kk * a)
    _put_blocks(go, g)


def _rwprep(p, cb0, s, mu, w0, a0, k_k, k_a, w2, a2, g2, tm):
    t = p.shape[1]
    lr = mu.shape[0] - 3 * RW_W
    nx = lr // LANES
    hilo = lambda w_: jnp.stack(_split(w_))
    w2p = hilo(jnp.zeros((lr, RW_W), F32).at[0:w2.shape[0]].set(w2))
    a2p = hilo(jnp.zeros((lr, RW_W), F32).at[w2.shape[0]:w2.shape[0] + a2.shape[0]].set(a2))
    g2p = hilo(jnp.zeros((lr, RW_W), F32).at[lr - g2.shape[0]:].set(g2))
    r8 = tm // 8
    gb = cb0 // RW_PAIRS
    xb = (cb0 + 3 * RW_PAIRS) // nx

    def cur(c, n):
        return pl.BlockSpec((n, tm, LANES), lambda i: (c, i, 0))

    def prev(c, n):
        return pl.BlockSpec((n, 8, LANES), lambda i: (c, jnp.maximum(i * r8 - 1, 0), 0))

    def full(shape):
        return pl.BlockSpec(shape, lambda i: (0,) * len(shape))

    vec = lambda a_: a_.astype(F32).reshape(1, -1)
    out = jax.ShapeDtypeStruct((RW_PAIRS, t, LANES), F32)
    return pl.pallas_call(
        functools.partial(_rwprep_kernel, rows_per_seq=s),
        grid=(t // tm,),
        in_specs=[cur(gb, RW_PAIRS), cur(gb + 1, RW_PAIRS), cur(gb + 2, RW_PAIRS), cur(xb, nx),
                  prev(gb, RW_PAIRS), prev(gb + 1, RW_PAIRS), prev(gb + 2, RW_PAIRS), prev(xb, nx),
                  full((1, 3 * RW_W + lr)), full((1, RW_W)), full((1, RW_W)), full((1, RW_W)),
                  full((1, RW_W)), full((2, lr, RW_W)), full((2, lr, RW_W)), full((2, lr, RW_W))],
        out_specs=[pl.BlockSpec((RW_PAIRS, tm, LANES), lambda i: (0, i, 0))] * 7,
        out_shape=[out] * 7,
        compiler_params=_params("parallel"),
    )(p, p, p, p, p, p, p, p, vec(mu), vec(w0), vec(a0), vec(k_k), vec(k_a), w2p, a2p, g2p)


def _rw_out_block(y, r, k, v, g, lng, lnb, rk, ones):
    mu = _dot_exact_rhs(y, ones) * (1.0 / RW_DIM)
    yc = y - mu
    var = _dot_exact_rhs(yc * yc, ones) * (1.0 / RW_DIM)
    out = yc * lax.rsqrt(var + RW_LN_EPS) * lng + lnb
    bonus = _dot_exact_rhs(r * k * rk, ones)
    return (out + bonus * v) * g


def _rwkv_kernel(r_ref, lw_ref, k_ref, v_ref, a_ref, b_ref, g_ref, lng_ref, lnb_ref, rk_ref, o_ref,
                 y_ref, rh_ref, yh_ref, p_ref, q_ref, st_ref, *, nchunk, unroll):
    L = CHUNK
    W = 2 * L
    npair = r_ref.shape[0]

    @pl.when(pl.program_id(1) == 0)
    def _():
        st_ref[...] = jnp.zeros(st_ref.shape, F32)

    lane = lax.broadcasted_iota(jnp.int32, (L, W), 1)
    rowi = lax.broadcasted_iota(jnp.int32, (L, W), 0)
    strict = (lane % L) < rowi
    incl = (lane % L) <= rowi
    rr = lax.broadcasted_iota(jnp.int32, (W, W), 0)
    cc = lax.broadcasted_iota(jnp.int32, (W, W), 1)
    same = (rr // L) == (cc // L)
    eye = rr == cc
    tl = lax.broadcasted_iota(jnp.int32, (L, L), 0)
    sl = lax.broadcasted_iota(jnp.int32, (L, L), 1)
    tril = (sl <= tl).astype(BF16)

    def bd(x):
        return jnp.where(same, jnp.concatenate([x, x], axis=0), 0.0)

    def group(gi, carry):
        us = [gi * unroll + i for i in range(unroll)]
        prs = [u // nchunk for u in us]
        rws = [pl.ds(pl.multiple_of((u % nchunk) * L, L), L) for u in us]
        G = range(unroll)
        ld = lambda ref: [ref[prs[i], rws[i], :] for i in G]
        r, lw, k, v, a, b = ld(r_ref), ld(lw_ref), ld(k_ref), ld(v_ref), ld(a_ref), ld(b_ref)

        def csum(x):
            hi, lo = _split(x)
            rest = x - hi.astype(F32) - lo.astype(F32)
            return jnp.concatenate([hi, lo, rest.astype(BF16)], axis=1)

        c3 = [_dot(tril, csum(lw[i])) for i in G]
        cin = [c[:, 0:W] + c[:, W:2 * W] + c[:, 2 * W:] for c in c3]
        clast = [c[L - 1:L, :] for c in cin]
        g_inv = [jnp.exp(-c) for c in cin]
        g_tail = [jnp.exp(clast[i] - cin[i]) for i in G]
        at = [a[i] * jnp.exp(cin[i] - lw[i]) for i in G]
        rt = [r[i] * jnp.exp(cin[i]) for i in G]
        abk = [_dot_nt(jnp.concatenate([at[i], rt[i]], axis=0),
                       jnp.concatenate([bd(b[i] * g_inv[i]), bd(k[i] * g_inv[i])], axis=0)) for i in G]
        a_ab = [jnp.where(strict, m[0:L, 0:W], 0.0) for m in abk]
        a_rb = [jnp.where(incl, m[L:W, 0:W], 0.0) for m in abk]
        a_ak = [jnp.where(strict, m[0:L, W:], 0.0) for m in abk]
        a_rk = [jnp.where(incl, m[L:W, W:], 0.0) for m in abk]
        n = [bd(m) for m in a_ab]
        tm = [jnp.where(eye, 1.0, m) for m in n]
        x = [_dot(m, m) for m in n]
        for j in range(5):
            if j < 4:
                xx = [_dot(x[i], jnp.concatenate([x[i], tm[i]], axis=1)) for i in G]
                x = [m[:, 0:W] for m in xx]
                tm = [tm[i] + xx[i][:, W:] for i in G]
            else:
                tm = [tm[i] + _dot(x[i], tm[i]) for i in G]
        v_bd = [bd(m) for m in v]
        kv = [_dot(jnp.concatenate([bd(a_ak[i]), a_rk[i]], axis=0), v_bd[i]) for i in G]
        au = [_dot(tm[i], jnp.concatenate([bd(at[i]), kv[i][0:W]], axis=1)) for i in G]
        ry = [_dot(a_rb[i], au[i]) for i in G]
        zero = jnp.zeros((W, W), F32)
        pq = [_dot_tn(jnp.concatenate([bd(b[i] * g_tail[i]), bd(k[i] * g_tail[i])], axis=0),
                      jnp.concatenate([au[i], jnp.concatenate([zero, v_bd[i]], axis=1)], axis=0))
              for i in G]
        for i in G:
            rh_ref[prs[i], rws[i], :] = rt[i] + ry[i][:, 0:W]
            yh_ref[prs[i], rws[i], :] = ry[i][:, W:] + kv[i][W:]
            p_ref[us[i]] = jnp.where(eye, jnp.exp(clast[i]), 0.0) + pq[i][:, 0:W]
            q_ref[us[i]] = pq[i][:, W:]
        return carry

    lax.fori_loop(0, npair * nchunk // unroll, group, 0)

    def step(c, carry):
        rows = pl.ds(pl.multiple_of(c * L, L), L)
        for pr in range(npair):
            st = st_ref[pr].astype(BF16)
            y_ref[pr, rows, :] = _dot(rh_ref[pr, rows, :], st) + yh_ref[pr, rows, :]
            st_ref[pr] = _dot(p_ref[pr * nchunk + c], st) + q_ref[pr * nchunk + c]
        return carry

    lax.fori_loop(0, nchunk, step, 0)

    ones = _group_ones(LANES, RW_DIM)
    for pr in range(npair):
        cols = slice(pr * LANES, (pr + 1) * LANES)
        o_ref[pr] = _rw_out_block(y_ref[pr], r_ref[pr], k_ref[pr], v_ref[pr], g_ref[pr], lng_ref[:, cols],
                                  lnb_ref[:, cols], rk_ref[:, cols], ones).astype(o_ref.dtype)


def _rwkv(r, lw, k, v, na, bb, g, ln_g, ln_b, r_k, b, s, sb, unroll):
    npair, t, w = r.shape
    nchunk = sb // CHUNK
    nsb = s // sb
    spec = pl.BlockSpec((npair, sb, w), lambda bi, si: (0, bi * nsb + si, 0))
    small = pl.BlockSpec((1, RW_W), lambda bi, si: (0, 0))
    vec = lambda a_: a_.astype(F32).reshape(1, RW_W)
    return pl.pallas_call(
        functools.partial(_rwkv_kernel, nchunk=nchunk, unroll=unroll),
        grid=(b, nsb),
        in_specs=[spec] * 7 + [small] * 3,
        out_specs=spec,
        out_shape=jax.ShapeDtypeStruct((npair, t, w), BF16),
        scratch_shapes=[pltpu.VMEM((npair, sb, w), F32),
                        pltpu.VMEM((npair, sb, w), F32), pltpu.VMEM((npair, sb, w), F32),
                        pltpu.VMEM((npair * nchunk, w, w), F32), pltpu.VMEM((npair * nchunk, w, w), F32),
                        pltpu.VMEM((npair, w, w), F32)],
        compiler_params=_params("parallel", "arbitrary"),
    )(r, lw, k, v, na, bb, g, vec(ln_g), vec(ln_b), vec(r_k))


def _outproj_kernel(x_ref, oa_ref, oc_ref, ob_ref, w_ref, o_ref):
    mix = jnp.concatenate([_cat_blocks(oa_ref, oa_ref.shape[0]), _cat_blocks(oc_ref, oc_ref.shape[0]),
                           _cat_blocks(ob_ref, ob_ref.shape[0])], axis=1)
    o_ref[...] = x_ref[...] + jnp.dot(mix, w_ref[...], preferred_element_type=F32)


def _outproj(x, oa, oc, ob, w, li, tm, tn):
    t, d = x.shape
    blocks = lambda a_: pl.BlockSpec((a_.shape[0], tm, LANES), lambda i, j: (0, i, 0))
    return pl.pallas_call(
        _outproj_kernel,
        grid=(t // tm, d // tn),
        in_specs=[pl.BlockSpec((tm, tn), lambda i, j: (i, j)),
                  blocks(oa), blocks(oc), blocks(ob),
                  pl.BlockSpec((None, w.shape[1], tn), lambda i, j: (li, 0, j))],
        out_specs=pl.BlockSpec((tm, tn), lambda i, j: (i, j)),
        out_shape=jax.ShapeDtypeStruct((t, d), F32),
        compiler_params=_params("parallel", "arbitrary"),
    )(x, oa, oc, ob, w)


FFN_SPLIT = 2


def _swiglu_rows(x_ref, wg_ref, wu_ref, wd_ref):
    n = x_ref.shape[0] // FFN_SPLIT
    sl = [slice(i * n, (i + 1) * n) for i in range(FFN_SPLIT)]
    wg, wu, wd = wg_ref[...], wu_ref[...], wd_ref[...]
    gu = [(jnp.dot(x_ref[r, :], wg, preferred_element_type=F32),
           jnp.dot(x_ref[r, :], wu, preferred_element_type=F32)) for r in sl]
    act = [(g * jax.nn.sigmoid(g) * u).astype(BF16) for g, u in gu]
    return [(r, jnp.dot(a, wd, preferred_element_type=F32)) for r, a in zip(sl, act)]


def _ffn_kernel(x_ref, g_ref, wg_ref, wu_ref, wd_ref, o_ref, xn_ref, acc_ref):
    f = pl.program_id(1)

    @pl.when(f == 0)
    def _():
        xn_ref[...] = _rms(x_ref[...], g_ref[...]).astype(BF16)
        acc_ref[...] = jnp.zeros(acc_ref.shape, F32)

    for r, y in _swiglu_rows(xn_ref, wg_ref, wu_ref, wd_ref):
        acc_ref[r, :] += y

    @pl.when(f == pl.num_programs(1) - 1)
    def _():
        o_ref[...] = x_ref[...] + acc_ref[...]


def _ffn(x, g, wg, wu, wd, li, tm, tf):
    t, d = x.shape
    ff = wg.shape[2]
    return pl.pallas_call(
        _ffn_kernel,
        grid=(t // tm, ff // tf),
        in_specs=[pl.BlockSpec((tm, d), lambda i, f: (i, 0)),
                  pl.BlockSpec((1, d), lambda i, f: (0, 0)),
                  pl.BlockSpec((None, d, tf), lambda i, f: (li, 0, f)),
                  pl.BlockSpec((None, d, tf), lambda i, f: (li, 0, f)),
                  pl.BlockSpec((None, tf, d), lambda i, f: (li, f, 0))],
        out_specs=pl.BlockSpec((tm, d), lambda i, f: (i, 0)),
        out_shape=jax.ShapeDtypeStruct((t, d), F32),
        scratch_shapes=[pltpu.VMEM((tm, d), BF16), pltpu.VMEM((tm, d), F32)],
        compiler_params=_params("parallel", "arbitrary"),
    )(x, g.reshape(1, d), wg, wu, wd)


def _router_kernel(x_ref, g_ref, wr_ref, h_ref, comb_ref, combt_ref, rcol_ref, rrow_ref, cnt_ref,
                   *, n_experts):
    h = _rms(x_ref[...], g_ref[...])
    h_ref[...] = h.astype(BF16)
    logits = _dot3(h, wr_ref[...])
    lane = lax.broadcasted_iota(jnp.int32, logits.shape, 1)
    lg = jnp.where(lane < n_experts, logits, NEG)
    m1 = jnp.max(lg, axis=-1, keepdims=True)
    i1 = jnp.min(jnp.where(lg == m1, lane, LANES), axis=-1, keepdims=True)
    lg2 = jnp.where(lane == i1, NEG, lg)
    m2 = jnp.max(lg2, axis=-1, keepdims=True)
    i2 = jnp.min(jnp.where(lg2 == m2, lane, LANES), axis=-1, keepdims=True)
    e2 = jnp.exp(m2 - m1)
    w1 = 1.0 / (1.0 + e2)
    comb = jnp.where(lane == i1, w1, 0.0) + jnp.where(lane == i2, e2 * w1, 0.0)
    combt = comb.T[0:combt_ref.shape[0], :]
    comb_ref[...] = comb
    combt_ref[...] = combt
    ts = comb.shape[0]
    tt = lax.broadcasted_iota(jnp.int32, (ts, ts), 0)
    uu = lax.broadcasted_iota(jnp.int32, (ts, ts), 1)
    live = jnp.where(comb > 0.0, 1.0, 0.0)
    rcol_ref[...] = _dot((uu < tt).astype(BF16), live)
    rrow_ref[...] = _dot(jnp.where(combt > 0.0, 1.0, 0.0), (tt < uu).astype(BF16))
    cnt_ref[0] = jnp.sum(live, axis=0, keepdims=True).astype(jnp.int32)


def _router(x, g, wr, ts):
    t, d = x.shape
    e = wr.shape[1]
    ep = max(8, e)
    wrp = jnp.zeros((d, LANES), F32).at[:, :e].set(wr)
    nt = t // ts
    return pl.pallas_call(
        functools.partial(_router_kernel, n_experts=e),
        grid=(nt,),
        in_specs=[pl.BlockSpec((ts, d), lambda i: (i, 0)),
                  pl.BlockSpec((1, d), lambda i: (0, 0)),
                  pl.BlockSpec((d, LANES), lambda i: (0, 0))],
        out_specs=[pl.BlockSpec((ts, d), lambda i: (i, 0)),
                   pl.BlockSpec((ts, LANES), lambda i: (i, 0)),
                   pl.BlockSpec((ep, ts), lambda i: (0, i)),
                   pl.BlockSpec((ts, LANES), lambda i: (i, 0)),
                   pl.BlockSpec((ep, ts), lambda i: (0, i)),
                   pl.BlockSpec((1, 1, LANES), lambda i: (i, 0, 0))],
        out_shape=[jax.ShapeDtypeStruct((t, d), BF16),
                   jax.ShapeDtypeStruct((t, LANES), F32),
                   jax.ShapeDtypeStruct((ep, t), F32),
                   jax.ShapeDtypeStruct((t, LANES), F32),
                   jax.ShapeDtypeStruct((ep, t), F32),
                   jax.ShapeDtypeStruct((nt, 1, LANES), jnp.int32)],
        compiler_params=_params("parallel"),
    )(x, g.reshape(1, d), wrp)


MOE_BM = 256
MOE_BMF = 512
NO_MATCH = -(1 << 20)


def _moe_plan(counts, nt, ne, t, bm, bmf):
    i32 = jnp.int32
    cnt = counts.reshape(nt, ne).astype(i32)
    tot = jnp.sum(cnt, axis=0)
    ptot = (tot + bmf - 1) // bmf * bmf
    eend = jnp.cumsum(ptot)
    ebase = eend - ptot
    seg0 = ebase[None, :] + jnp.cumsum(cnt, axis=0) - cnt
    seg1 = seg0 + cnt
    nrows = 2 * t + ne * bmf
    nblk = nrows // bmf
    count_le = lambda ends, v: jnp.sum((ends[None, :] <= v[:, None]).astype(i32), axis=1)
    blk_exp = jnp.minimum(count_le(eend, jnp.arange(nblk, dtype=i32) * bmf), ne - 1)
    nvalid = (eend[-1] // bmf).astype(i32).reshape(1)
    npairs = nrows // bm + nt * ne
    g = jnp.arange(npairs, dtype=i32)

    def pairs(c0, c1, seg_tile, seg_exp, dummy_blk):
        n = jnp.where(c1 > c0, (c1 - 1) // bm - c0 // bm + 1, 0)
        pend = jnp.cumsum(n)
        k = jnp.minimum(count_le(pend, g), c0.shape[0] - 1)
        valid = g < pend[-1]
        blk = c0[k] // bm + (g - (pend[k] - n[k]))
        blk = jnp.where(valid, blk, dummy_blk)
        delta = jnp.where(valid, c0[k] - blk * bm, NO_MATCH)
        return (blk.astype(i32), jnp.where(valid, seg_tile[k], nt - 1).astype(i32),
                seg_exp[k].astype(i32), delta.astype(i32))

    tiles = jnp.arange(nt, dtype=i32)
    exps = jnp.arange(ne, dtype=i32)
    c1g = seg1.at[nt - 1].set(eend)
    g_blk, g_tile, g_exp, g_delta = pairs(seg0.T.reshape(-1), c1g.T.reshape(-1),
                                          jnp.tile(tiles, ne), jnp.repeat(exps, nt), nrows // bm)
    g_first = jnp.concatenate([jnp.ones((1,), i32), (g_blk[1:] != g_blk[:-1]).astype(i32)])
    s_blk, s_tile, s_exp, s_delta = pairs(seg0.reshape(-1), seg1.reshape(-1),
                                          jnp.repeat(tiles, ne), jnp.tile(exps, nt), 0)
    s_first = jnp.concatenate([jnp.ones((1,), i32), (s_tile[1:] != s_tile[:-1]).astype(i32)])
    return dict(nrows=nrows, blk_exp=blk_exp, nvalid=nvalid,
                gather=(g_blk, g_tile, g_exp, g_delta, g_first),
                combine=(s_blk, s_tile, s_exp, s_delta, s_first))


def _moe_gather_kernel(blk_ref, tile_ref, exp_ref, delta_ref, first_ref, h_ref, rrow_ref, combt_ref,
                       o_ref):
    g = pl.program_id(0)
    e = exp_ref[g]
    bm, ts = o_ref.shape[0], h_ref.shape[0]
    pos = rrow_ref[pl.ds(e, 1), :] + delta_ref[g].astype(F32)
    live = combt_ref[pl.ds(e, 1), :] > 0.0
    slot = lax.broadcasted_iota(jnp.int32, (bm, ts), 0).astype(F32)
    onehot = jnp.where((pos == slot) & live, 1.0, 0.0).astype(BF16)
    val = jnp.dot(onehot, h_ref[...], preferred_element_type=F32).astype(BF16)

    @pl.when(first_ref[g] == 1)
    def _():
        o_ref[...] = val

    @pl.when(first_ref[g] == 0)
    def _():
        o_ref[...] += val


def _moe_gather(plan, h, rrow, combt, ts, bm):
    t, d = h.shape
    ep = combt.shape[0]
    blk, tile, exp, delta, first = plan["gather"]
    grid_spec = pltpu.PrefetchScalarGridSpec(
        num_scalar_prefetch=5,
        grid=(blk.shape[0],),
        in_specs=[pl.BlockSpec((ts, d), lambda g, b_, t_, e_, d_, f_: (t_[g], 0)),
                  pl.BlockSpec((ep, ts), lambda g, b_, t_, e_, d_, f_: (0, t_[g])),
                  pl.BlockSpec((ep, ts), lambda g, b_, t_, e_, d_, f_: (0, t_[g]))],
        out_specs=pl.BlockSpec((bm, d), lambda g, b_, t_, e_, d_, f_: (b_[g], 0)),
    )
    return pl.pallas_call(
        _moe_gather_kernel,
        grid_spec=grid_spec,
        out_shape=jax.ShapeDtypeStruct((plan["nrows"] + MOE_BMF, d), BF16),
        compiler_params=_params("arbitrary"),
    )(blk, tile, exp, delta, first, h, rrow, combt)


def _moe_ffn_kernel(bexp_ref, nv_ref, x_ref, wg_ref, wu_ref, wd_ref, o_ref, acc_ref):
    i, f = pl.program_id(0), pl.program_id(1)
    valid = i < nv_ref[0]

    @pl.when(valid)
    def _():
        @pl.when(f == 0)
        def _():
            acc_ref[...] = jnp.zeros(acc_ref.shape, F32)

        for r, y in _swiglu_rows(x_ref, wg_ref, wu_ref, wd_ref):
            acc_ref[r, :] += y

    @pl.when(f == pl.num_programs(1) - 1)
    def _():
        @pl.when(valid)
        def _():
            o_ref[...] = acc_ref[...].astype(BF16)

        @pl.when(jnp.logical_not(valid))
        def _():
            o_ref[...] = jnp.zeros(o_ref.shape, BF16)


def _moe_ffn(plan, xs, wg, wu, wd, li, bmf, tf):
    d = xs.shape[1]
    ff = wg.shape[3]
    nblk = plan["blk_exp"].shape[0]
    grid_spec = pltpu.PrefetchScalarGridSpec(
        num_scalar_prefetch=2,
        grid=(nblk, ff // tf),
        in_specs=[pl.BlockSpec((bmf, d), lambda i, f, be, nv: (i, 0)),
                  pl.BlockSpec((None, None, d, tf), lambda i, f, be, nv: (li, be[i], 0, f)),
                  pl.BlockSpec((None, None, d, tf), lambda i, f, be, nv: (li, be[i], 0, f)),
                  pl.BlockSpec((None, None, tf, d), lambda i, f, be, nv: (li, be[i], f, 0))],
        out_specs=pl.BlockSpec((bmf, d), lambda i, f, be, nv: (i, 0)),
        scratch_shapes=[pltpu.VMEM((bmf, d), F32)],
    )
    return pl.pallas_call(
        _moe_ffn_kernel,
        grid_spec=grid_spec,
        out_shape=jax.ShapeDtypeStruct((nblk * bmf, d), BF16),
        compiler_params=_params("parallel", "arbitrary"),
    )(plan["blk_exp"], plan["nvalid"], xs, wg, wu, wd)


def _moe_combine_kernel(blk_ref, tile_ref, exp_ref, delta_ref, first_ref, y_ref, x_ref, comb_ref,
                        rcol_ref, o_ref):
    g = pl.program_id(0)
    e = exp_ref[g]
    ts, bm = x_ref.shape[0], y_ref.shape[0]

    @pl.when(first_ref[g] == 1)
    def _():
        o_ref[...] = x_ref[...]

    lane = lax.broadcasted_iota(jnp.int32, comb_ref.shape, 1)
    sel = lane == e
    cw = jnp.sum(jnp.where(sel, comb_ref[...], 0.0), axis=-1, keepdims=True)
    pos = (jnp.sum(jnp.where(sel, rcol_ref[...], 0.0), axis=-1, keepdims=True)
           + delta_ref[g].astype(F32))
    slot = lax.broadcasted_iota(jnp.int32, (ts, bm), 1).astype(F32)
    onehot = jnp.where((pos == slot) & (cw > 0.0), 1.0, 0.0).astype(BF16)
    o_ref[...] += cw * jnp.dot(onehot, y_ref[...], preferred_element_type=F32)


def _moe_combine(plan, ys, x, comb, rcol, ts, bm):
    t, d = x.shape
    blk, tile, exp, delta, first = plan["combine"]
    grid_spec = pltpu.PrefetchScalarGridSpec(
        num_scalar_prefetch=5,
        grid=(blk.shape[0],),
        in_specs=[pl.BlockSpec((bm, d), lambda g, b_, t_, e_, d_, f_: (b_[g], 0)),
                  pl.BlockSpec((ts, d), lambda g, b_, t_, e_, d_, f_: (t_[g], 0)),
                  pl.BlockSpec((ts, LANES), lambda g, b_, t_, e_, d_, f_: (t_[g], 0)),
                  pl.BlockSpec((ts, LANES), lambda g, b_, t_, e_, d_, f_: (t_[g], 0))],
        out_specs=pl.BlockSpec((ts, d), lambda g, b_, t_, e_, d_, f_: (t_[g], 0)),
    )
    return pl.pallas_call(
        _moe_combine_kernel,
        grid_spec=grid_spec,
        out_shape=jax.ShapeDtypeStruct((t, d), F32),
        compiler_params=_params("arbitrary"),
    )(blk, tile, exp, delta, first, ys, x, comb, rcol)


def _moe(x, h, comb, combt, rcol, rrow, counts, wg, wu, wd, li, ts, tf, bm=MOE_BM, bmf=MOE_BMF):
    t = x.shape[0]
    plan = _moe_plan(counts, t // ts, wg.shape[1], t, bm, bmf)
    xs = _moe_gather(plan, h, rrow, combt, ts, bm)
    ys = _moe_ffn(plan, xs, wg, wu, wd, li, bmf, tf)
    return _moe_combine(plan, ys, x, comb, rcol, ts, bm)


def kernel(x, norm1_g, w_in, da_q_norm, da_k_norm, da_lambda, da_out_norm, dl_q_norm, dl_k_norm, rw_mu, rw_w0, rw_w2, rw_a0, rw_a2, rw_g2, rw_k_k, rw_k_a, rw_r_k, rw_ln_g, rw_ln_b, w_out, norm2_g, ffn_w_gate, ffn_w_up, ffn_w_down, moe_router, moe_w_gate, moe_w_up, moe_w_down):
    b, s, d = x.shape
    depth = w_in.shape[0]
    t = b * s
    n_experts = moe_router.shape[-1]
    xt = x.reshape(t, d)

    qa, ka, va = 0, DA_HEADS, 2 * DA_HEADS
    qb, kb, vb = 3 * DA_HEADS, 3 * DA_HEADS + DL_HEADS, 3 * DA_HEADS + 2 * DL_HEADS
    rw = 3 * DA_HEADS + 3 * DL_HEADS

    tm = min(1024, t)
    tq = min(256, s)
    tf = 512
    assert ffn_w_gate.shape[2] % tf == 0 and moe_w_gate.shape[3] % tf == 0

    w_in, w_out = w_in.astype(BF16), w_out.astype(BF16)
    ffn_w = [w.astype(BF16) for w in (ffn_w_gate, ffn_w_up, ffn_w_down)]
    moe_w = [w.astype(BF16) for w in (moe_w_gate, moe_w_up, moe_w_down)]

    for l in range(depth):
        p = _inproj(xt, norm1_g[l], w_in, l, tm, 1280)

        lam_init = 0.8 - 0.6 * math.exp(-0.3 * l)
        oa = _diffattn(p, qa, ka, va, da_q_norm[l], da_k_norm[l], da_lambda[l], da_out_norm[l],
                       b, s, lam_init, tq)
        oc = _dilattn(p, qb, kb, vb, dl_q_norm[l], dl_k_norm[l], b, s, tq)

        r, lw, k2, v, na, bb, g = _rwprep(p, rw, s, rw_mu[l], rw_w0[l], rw_a0[l], rw_k_k[l],
                                          rw_k_a[l], rw_w2[l], rw_a2[l], rw_g2[l], min(256, s))
        ob = _rwkv(r, lw, k2, v, na, bb, g, rw_ln_g[l], rw_ln_b[l], rw_r_k[l].reshape(-1),
                   b, s, min(512, s), 16)

        xt = _outproj(xt, oa, oc, ob, w_out, l, min(512, t), d)

        i = l // 2
        if l % 2 == 0:
            xt = _ffn(xt, norm2_g[l], *ffn_w, i, min(512, t), tf)
        else:
            ts = min(1024, t)
            h, comb, combt, rcol, rrow, cnt = _router(xt, norm2_g[l], moe_router[i], ts)
            counts = cnt[:, 0, :n_experts].reshape(-1)
            xt = _moe(xt, h, comb, combt, rcol, rrow, counts, *moe_w, i, ts, tf)
    return xt.reshape(b, s, d)
```

```python
import functools
import math

import numpy as np
import jax
import jax.numpy as jnp
from jax import lax
from jax.experimental import pallas as pl
from jax.experimental.pallas import tpu as pltpu

F32 = jnp.float32
BF16 = jnp.bfloat16

LANES = 128
VMEM_LIMIT = 56 * 1024 * 1024

NEG = -1e30
ROPE_THETA = 10000.0
NORM_EPS = 1e-6
RW_LN_EPS = 64e-5
DL_PATTERNS = ((128, 1), (512, 4), (2048, 16))
TOP_K = 2

DA_HEADS, DA_QK = 4, 64
DL_HEADS = 6
RW_HEADS, RW_DIM = 12, 64
DA_W, DL_W, RW_W = 512, 768, 768
RW_PAIRS = RW_W // LANES
CHUNK = 64


def _params(*sem):
    return pltpu.CompilerParams(dimension_semantics=sem, vmem_limit_bytes=VMEM_LIMIT)


def _dot(a, b):
    return jnp.dot(a.astype(BF16), b.astype(BF16), preferred_element_type=F32)


def _dot_nt(a, b):
    return lax.dot_general(a.astype(BF16), b.astype(BF16), (((1,), (1,)), ((), ())),
                           preferred_element_type=F32)


def _dot_tn(a, b):
    return lax.dot_general(a.astype(BF16), b.astype(BF16), (((0,), (0,)), ((), ())),
                           preferred_element_type=F32)


def _split(x):
    hi = x.astype(BF16)
    lo = (x - hi.astype(F32)).astype(BF16)
    return hi, lo


def _dot3(a, b, b_parts=None):
    ah, al = _split(a)
    bh, bl = _split(b) if b_parts is None else b_parts
    return _dot(ah, bh) + _dot(ah, bl) + _dot(al, bh)


def _dot_exact_rhs(a, b_bf16):
    ah, al = _split(a)
    return _dot(ah, b_bf16) + _dot(al, b_bf16)


def _group_ones(width, group):
    i = lax.broadcasted_iota(jnp.int32, (width, width), 0) // group
    j = lax.broadcasted_iota(jnp.int32, (width, width), 1) // group
    return (i == j).astype(BF16)


def _rms(x, g):
    return x * lax.rsqrt(jnp.mean(x * x, axis=-1, keepdims=True) + NORM_EPS) * g


def _cat_blocks(ref, n):
    return jnp.concatenate([ref[c] for c in range(n)], axis=1)


def _put_blocks(ref, val):
    for c in range(ref.shape[0]):
        ref[c] = val[:, c * LANES:(c + 1) * LANES].astype(ref.dtype)


def _inproj_kernel(x_ref, g_ref, w_ref, o_ref, xn_ref):
    @pl.when(pl.program_id(1) == 0)
    def _():
        xn_ref[...] = _rms(x_ref[...], g_ref[...]).astype(BF16)

    _put_blocks(o_ref, jnp.dot(xn_ref[...], w_ref[...], preferred_element_type=F32))


def _inproj(x, g, w, li, tm, tn):
    t, d = x.shape
    n = w.shape[2]
    return pl.pallas_call(
        _inproj_kernel,
        grid=(t // tm, n // tn),
        in_specs=[pl.BlockSpec((tm, d), lambda i, j: (i, 0)),
                  pl.BlockSpec((1, d), lambda i, j: (0, 0)),
                  pl.BlockSpec((None, d, tn), lambda i, j: (li, 0, j))],
        out_specs=pl.BlockSpec((tn // LANES, tm, LANES), lambda i, j: (j, i, 0)),
        out_shape=jax.ShapeDtypeStruct((n // LANES, t, LANES), F32),
        scratch_shapes=[pltpu.VMEM((tm, d), BF16)],
        compiler_params=_params("parallel", "arbitrary"),
    )(x, g.reshape(1, d), w)


LOG2E = math.log2(math.e)


def _prep_qk(x, gain, cos, sin, group, post=1.0):
    ones = _group_ones(LANES, group)
    ms = _dot_exact_rhs(x * x, ones) * (1.0 / group)
    y = x * lax.rsqrt(ms + NORM_EPS) * gain
    half = group // 2
    if group == LANES:
        partner = pltpu.roll(y, half, axis=1)
    else:
        lane = lax.broadcasted_iota(jnp.int32, y.shape, 1)
        fwd = pltpu.roll(y, LANES - half, axis=1)
        bwd = pltpu.roll(y, half, axis=1)
        partner = jnp.where((lane % group) < half, fwd, bwd)
    return ((y * cos + partner * sin) * post).astype(BF16)


def _rope_tables(s, group):
    half = group // 2
    lane = np.arange(LANES)
    inv = ROPE_THETA ** (-jnp.asarray(lane % half, F32) / half)
    ang = jnp.arange(s, dtype=F32)[:, None] * inv[None, :]
    sign = jnp.asarray(np.where((lane % group) < half, -1.0, 1.0), F32)
    return jnp.cos(ang), jnp.sin(ang) * sign[None, :]


def _tile_gain(gain, group):
    return jnp.tile(gain.astype(F32), LANES // group).reshape(1, LANES)


def _block_pairs(nq):
    return [(j, qi) for j in range(nq) for qi in range(j, nq)]


def _vt_blocks(v_ref, tq, nq):
    return [v_ref[j * tq:(j + 1) * tq, :].T.astype(BF16) for j in range(nq)]


def _diffattn_kernel(q_ref, k_ref, v_ref, cos_ref, sin_ref, gq_ref, gk_ref, lam_ref, go_ref, o_ref,
                     s0_ref, s1_ref, *, tq, scale, lam_init):
    nq = q_ref.shape[0] // tq
    cos, sin = cos_ref[...], sin_ref[...]
    kp = _prep_qk(k_ref[...], gk_ref[...], cos, sin, DA_QK)
    qp = _prep_qk(q_ref[...], gq_ref[...], cos, sin, DA_QK, scale * LOG2E)
    lane = lax.broadcasted_iota(jnp.int32, qp.shape, 1)
    zero = jnp.zeros_like(qp)
    qs = (jnp.where(lane < DA_QK, qp, zero), jnp.where(lane >= DA_QK, qp, zero))
    vt = _vt_blocks(v_ref, tq, nq)
    blk = lambda x, i: x[i * tq:(i + 1) * tq, :]
    causal = (lax.broadcasted_iota(jnp.int32, (tq, tq), 1)
              >= lax.broadcasted_iota(jnp.int32, (tq, tq), 0))
    pairs = _block_pairs(nq)
    s_refs = (s0_ref, s1_ref)

    m = [[jnp.full((1, tq), NEG, F32) for _ in range(nq)] for _ in range(2)]
    for idx, (j, qi) in enumerate(pairs):
        for c in range(2):
            sc = _dot_nt(blk(kp, j), blk(qs[c], qi))
            if j == qi:
                sc = jnp.where(causal, sc, NEG)
            s_refs[c][idx] = sc
            m[c][qi] = jnp.maximum(m[c][qi], jnp.max(sc, axis=0, keepdims=True))

    l = [[jnp.zeros((1, tq), F32) for _ in range(nq)] for _ in range(2)]
    acc = [[None] * nq for _ in range(2)]
    for idx, (j, qi) in enumerate(pairs):
        for c in range(2):
            pr = jnp.exp2(s_refs[c][idx] - m[c][qi])
            l[c][qi] = l[c][qi] + jnp.sum(pr, axis=0, keepdims=True)
            d = _dot(vt[j], pr)
            acc[c][qi] = d if acc[c][qi] is None else acc[c][qi] + d

    lm = lam_ref[...]
    lam = (jnp.exp(jnp.sum(lm[0:1] * lm[1:2], axis=-1, keepdims=True))
           - jnp.exp(jnp.sum(lm[2:3] * lm[3:4], axis=-1, keepdims=True)) + lam_init)
    for qi in range(nq):
        ot = acc[0][qi] / l[0][qi] - lam * (acc[1][qi] / l[1][qi])
        ot = ot * lax.rsqrt(jnp.mean(ot * ot, axis=0, keepdims=True) + NORM_EPS)
        o_ref[qi * tq:(qi + 1) * tq, :] = (ot.T * go_ref[...] * (1.0 - lam_init)).astype(BF16)


def _diffattn(p, qcb, kcb, vcb, gq, gk, lam4, gout, b, s, lam_init, tq):
    t = p.shape[1]
    nq = s // tq
    npair = nq * (nq + 1) // 2
    cos, sin = _rope_tables(s, DA_QK)
    blk = lambda cb0: pl.BlockSpec((None, s, LANES), lambda bi, h: (cb0 + h, bi, 0))
    full = lambda shape: pl.BlockSpec(shape, lambda bi, h: (0, 0))
    return pl.pallas_call(
        functools.partial(_diffattn_kernel, tq=tq, scale=DA_QK ** -0.5, lam_init=lam_init),
        grid=(b, DA_HEADS),
        in_specs=[blk(qcb), blk(kcb), blk(vcb),
                  full((s, LANES)), full((s, LANES)), full((1, LANES)), full((1, LANES)),
                  full((4, DA_QK)), full((1, LANES))],
        out_specs=blk(0),
        out_shape=jax.ShapeDtypeStruct((DA_HEADS, t, LANES), BF16),
        scratch_shapes=[pltpu.VMEM((npair, tq, tq), F32), pltpu.VMEM((npair, tq, tq), F32)],
        compiler_params=_params("parallel", "parallel"),
    )(p, p, p, cos, sin, _tile_gain(gq, DA_QK), _tile_gain(gk, DA_QK), lam4.astype(F32),
      gout.astype(F32).reshape(1, LANES))


def _dilated_bias(s, tq):
    nd = s // tq
    d = (np.arange(nd)[:, None, None] * tq + np.arange(tq)[None, None, :]
         - np.arange(tq)[None, :, None])
    cnt = np.zeros(d.shape, np.float64)
    for window, dil in DL_PATTERNS:
        cnt += (d >= 0) & (d % dil == 0) & (d <= window)
    bias = np.where(cnt > 0, np.log2(np.maximum(cnt, 1.0)), NEG)
    return jnp.asarray(bias, F32)


def _dilattn_kernel(q_ref, k_ref, v_ref, cos_ref, sin_ref, gq_ref, gk_ref, bias_ref, o_ref, s_ref,
                    *, tq, scale):
    nq = q_ref.shape[0] // tq
    cos, sin = cos_ref[...], sin_ref[...]
    kp = _prep_qk(k_ref[...], gk_ref[...], cos, sin, LANES)
    qp = _prep_qk(q_ref[...], gq_ref[...], cos, sin, LANES, scale * LOG2E)
    vt = _vt_blocks(v_ref, tq, nq)
    blk = lambda x, i: x[i * tq:(i + 1) * tq, :]
    pairs = _block_pairs(nq)

    m = [jnp.full((1, tq), NEG, F32) for _ in range(nq)]
    for idx, (j, qi) in enumerate(pairs):
        sc = _dot_nt(blk(kp, j), blk(qp, qi)) + bias_ref[qi - j]
        s_ref[idx] = sc
        m[qi] = jnp.maximum(m[qi], jnp.max(sc, axis=0, keepdims=True))

    l = [jnp.zeros((1, tq), F32) for _ in range(nq)]
    acc = [None] * nq
    for idx, (j, qi) in enumerate(pairs):
        pr = jnp.exp2(s_ref[idx] - m[qi])
        l[qi] = l[qi] + jnp.sum(pr, axis=0, keepdims=True)
        d = _dot(vt[j], pr)
        acc[qi] = d if acc[qi] is None else acc[qi] + d

    for qi in range(nq):
        o_ref[qi * tq:(qi + 1) * tq, :] = (acc[qi] / l[qi]).T.astype(BF16)


def _dilattn(p, qcb, kcb, vcb, gq, gk, b, s, tq):
    t = p.shape[1]
    nq = s // tq
    cos, sin = _rope_tables(s, LANES)
    bias = _dilated_bias(s, tq)
    blk = lambda cb0: pl.BlockSpec((None, s, LANES), lambda bi, h: (cb0 + h, bi, 0))
    full = lambda shape: pl.BlockSpec(shape, lambda bi, h: (0,) * len(shape))
    return pl.pallas_call(
        functools.partial(_dilattn_kernel, tq=tq, scale=LANES ** -0.5),
        grid=(b, DL_HEADS),
        in_specs=[blk(qcb), blk(kcb), blk(vcb),
                  full((s, LANES)), full((s, LANES)), full((1, LANES)), full((1, LANES)),
                  full((nq, tq, tq))],
        out_specs=blk(0),
        out_shape=jax.ShapeDtypeStruct((DL_HEADS, t, LANES), BF16),
        scratch_shapes=[pltpu.VMEM((nq * (nq + 1) // 2, tq, tq), F32)],
        compiler_params=_params("parallel", "parallel"),
    )(p, p, p, cos, sin, _tile_gain(gq, LANES), _tile_gain(gk, LANES), bias)


def _rwprep_kernel(r_ref, k_ref, v_ref, x_ref, rp_ref, kp_ref, vp_ref, xp_ref,
                   mu_ref, w0_ref, a0_ref, kk_ref, ka_ref, w2_ref, a2_ref, g2_ref,
                   ro, lwo, ko, vo, nao, bo, go, *, rows_per_seq):
    i = pl.program_id(0)
    tm = r_ref.shape[1]
    first = (i * tm) % rows_per_seq == 0
    row = lax.broadcasted_iota(jnp.int32, (tm, 1), 0)

    def shifted(cur_ref, prev_ref, mu):
        n = cur_ref.shape[0]
        cur = _cat_blocks(cur_ref, n)
        last = jnp.concatenate([prev_ref[c, 7:8, :] for c in range(n)], axis=1)
        last = jnp.where(first, 0.0, last)
        prev = jnp.where(row == 0, last, pltpu.roll(cur, 1, axis=0))
        return cur + mu * (prev - cur)

    mu = mu_ref[...]
    rr = shifted(r_ref, rp_ref, mu[:, 0:RW_W])
    kr = shifted(k_ref, kp_ref, mu[:, RW_W:2 * RW_W])
    vr = shifted(v_ref, vp_ref, mu[:, 2 * RW_W:3 * RW_W])
    xs = shifted(x_ref, xp_ref, mu[:, 3 * RW_W:])

    z = w0_ref[...] + _dot3(jnp.tanh(xs), None, (w2_ref[0], w2_ref[1]))
    nz = -z
    softplus = jnp.maximum(nz, 0.0) + jnp.log(1.0 + jnp.exp(-jnp.abs(nz)))
    w_log = -softplus - 0.5
    a = jax.nn.sigmoid(a0_ref[...] + _dot3(xs, None, (a2_ref[0], a2_ref[1])))
    g = _dot3(jax.nn.sigmoid(xs), None, (g2_ref[0], g2_ref[1]))

    ones = _group_ones(LANES, RW_DIM)
    kkr = kr * kk_ref[...]
    sq = kkr * kkr
    ss = jnp.concatenate([_dot_exact_rhs(sq[:, c * LANES:(c + 1) * LANES], ones)
                          for c in range(RW_PAIRS)], axis=1)
    kk = kkr / jnp.maximum(jnp.sqrt(ss), 1e-12)
    _put_blocks(ro, rr)
    _put_blocks(lwo, -jnp.exp(w_log))
    _put_blocks(ko, kr * (1.0 + (a - 1.0) * ka_ref[...]))
    _put_blocks(vo, vr)
    _put_blocks(nao, -kk)
    _put_blocks(bo, kk * a)
    _put_blocks(go, g)


def _rwprep(p, cb0, s, mu, w0, a0, k_k, k_a, w2, a2, g2, tm):
    t = p.shape[1]
    lr = mu.shape[0] - 3 * RW_W
    nx = lr // LANES
    hilo = lambda w_: jnp.stack(_split(w_))
    w2p = hilo(jnp.zeros((lr, RW_W), F32).at[0:w2.shape[0]].set(w2))
    a2p = hilo(jnp.zeros((lr, RW_W), F32).at[w2.shape[0]:w2.shape[0] + a2.shape[0]].set(a2))
    g2p = hilo(jnp.zeros((lr, RW_W), F32).at[lr - g2.shape[0]:].set(g2))
    r8 = tm // 8
    gb = cb0 // RW_PAIRS
    xb = (cb0 + 3 * RW_PAIRS) // nx

    def cur(c, n):
        return pl.BlockSpec((n, tm, LANES), lambda i: (c, i, 0))

    def prev(c, n):
        return pl.BlockSpec((n, 8, LANES), lambda i: (c, jnp.maximum(i * r8 - 1, 0), 0))

    def full(shape):
        return pl.BlockSpec(shape, lambda i: (0,) * len(shape))

    vec = lambda a_: a_.astype(F32).reshape(1, -1)
    out = jax.ShapeDtypeStruct((RW_PAIRS, t, LANES), F32)
    return pl.pallas_call(
        functools.partial(_rwprep_kernel, rows_per_seq=s),
        grid=(t // tm,),
        in_specs=[cur(gb, RW_PAIRS), cur(gb + 1, RW_PAIRS), cur(gb + 2, RW_PAIRS), cur(xb, nx),
                  prev(gb, RW_PAIRS), prev(gb + 1, RW_PAIRS), prev(gb + 2, RW_PAIRS), prev(xb, nx),
                  full((1, 3 * RW_W + lr)), full((1, RW_W)), full((1, RW_W)), full((1, RW_W)),
                  full((1, RW_W)), full((2, lr, RW_W)), full((2, lr, RW_W)), full((2, lr, RW_W))],
        out_specs=[pl.BlockSpec((RW_PAIRS, tm, LANES), lambda i: (0, i, 0))] * 7,
        out_shape=[out] * 7,
        compiler_params=_params("parallel"),
    )(p, p, p, p, p, p, p, p, vec(mu), vec(w0), vec(a0), vec(k_k), vec(k_a), w2p, a2p, g2p)


def _rw_out_block(y, r, k, v, g, lng, lnb, rk, ones):
    mu = _dot_exact_rhs(y, ones) * (1.0 / RW_DIM)
    yc = y - mu
    var = _dot_exact_rhs(yc * yc, ones) * (1.0 / RW_DIM)
    out = yc * lax.rsqrt(var + RW_LN_EPS) * lng + lnb
    bonus = _dot_exact_rhs(r * k * rk, ones)
    return (out + bonus * v) * g


def _rwkv_kernel(r_ref, lw_ref, k_ref, v_ref, a_ref, b_ref, g_ref, lng_ref, lnb_ref, rk_ref, o_ref,
                 y_ref, rh_ref, yh_ref, p_ref, q_ref, st_ref, *, nchunk, unroll):
    L = CHUNK
    W = 2 * L
    npair = r_ref.shape[0]

    @pl.when(pl.program_id(1) == 0)
    def _():
        st_ref[...] = jnp.zeros(st_ref.shape, F32)

    lane = lax.broadcasted_iota(jnp.int32, (L, W), 1)
    rowi = lax.broadcasted_iota(jnp.int32, (L, W), 0)
    strict = (lane % L) < rowi
    incl = (lane % L) <= rowi
    rr = lax.broadcasted_iota(jnp.int32, (W, W), 0)
    cc = lax.broadcasted_iota(jnp.int32, (W, W), 1)
    same = (rr // L) == (cc // L)
    eye = rr == cc
    tl = lax.broadcasted_iota(jnp.int32, (L, L), 0)
    sl = lax.broadcasted_iota(jnp.int32, (L, L), 1)
    tril = (sl <= tl).astype(BF16)

    def bd(x):
        return jnp.where(same, jnp.concatenate([x, x], axis=0), 0.0)

    def group(gi, carry):
        us = [gi * unroll + i for i in range(unroll)]
        prs = [u // nchunk for u in us]
        rws = [pl.ds(pl.multiple_of((u % nchunk) * L, L), L) for u in us]
        G = range(unroll)
        ld = lambda ref: [ref[prs[i], rws[i], :] for i in G]
        r, lw, k, v, a, b = ld(r_ref), ld(lw_ref), ld(k_ref), ld(v_ref), ld(a_ref), ld(b_ref)

        def csum(x):
            hi, lo = _split(x)
            rest = x - hi.astype(F32) - lo.astype(F32)
            return jnp.concatenate([hi, lo, rest.astype(BF16)], axis=1)

        c3 = [_dot(tril, csum(lw[i])) for i in G]
        cin = [c[:, 0:W] + c[:, W:2 * W] + c[:, 2 * W:] for c in c3]
        clast = [c[L - 1:L, :] for c in cin]
        g_inv = [jnp.exp(-c) for c in cin]
        g_tail = [jnp.exp(clast[i] - cin[i]) for i in G]
        at = [a[i] * jnp.exp(cin[i] - lw[i]) for i in G]
        rt = [r[i] * jnp.exp(cin[i]) for i in G]
        abk = [_dot_nt(jnp.concatenate([at[i], rt[i]], axis=0),
                       jnp.concatenate([bd(b[i] * g_inv[i]), bd(k[i] * g_inv[i])], axis=0)) for i in G]
        a_ab = [jnp.where(strict, m[0:L, 0:W], 0.0) for m in abk]
        a_rb = [jnp.where(incl, m[L:W, 0:W], 0.0) for m in abk]
        a_ak = [jnp.where(strict, m[0:L, W:], 0.0) for m in abk]
        a_rk = [jnp.where(incl, m[L:W, W:], 0.0) for m in abk]
        n = [bd(m) for m in a_ab]
        tm = [jnp.where(eye, 1.0, m) for m in n]
        x = [_dot(m, m) for m in n]
        for j in range(5):
            if j < 4:
                xx = [_dot(x[i], jnp.concatenate([x[i], tm[i]], axis=1)) for i in G]
                x = [m[:, 0:W] for m in xx]
                tm = [tm[i] + xx[i][:, W:] for i in G]
            else:
                tm = [tm[i] + _dot(x[i], tm[i]) for i in G]
        v_bd = [bd(m) for m in v]
        kv = [_dot(jnp.concatenate([bd(a_ak[i]), a_rk[i]], axis=0), v_bd[i]) for i in G]
        au = [_dot(tm[i], jnp.concatenate([bd(at[i]), kv[i][0:W]], axis=1)) for i in G]
        ry = [_dot(a_rb[i], au[i]) for i in G]
        zero = jnp.zeros((W, W), F32)
        pq = [_dot_tn(jnp.concatenate([bd(b[i] * g_tail[i]), bd(k[i] * g_tail[i])], axis=0),
                      jnp.concatenate([au[i], jnp.concatenate([zero, v_bd[i]], axis=1)], axis=0))
              for i in G]
        for i in G:
            rh_ref[prs[i], rws[i], :] = rt[i] + ry[i][:, 0:W]
            yh_ref[prs[i], rws[i], :] = ry[i][:, W:] + kv[i][W:]
            p_ref[us[i]] = jnp.where(eye, jnp.exp(clast[i]), 0.0) + pq[i][:, 0:W]
            q_ref[us[i]] = pq[i][:, W:]
        return carry

    lax.fori_loop(0, npair * nchunk // unroll, group, 0)

    def step(c, carry):
        rows = pl.ds(pl.multiple_of(c * L, L), L)
        for pr in range(npair):
            st = st_ref[pr].astype(BF16)
            y_ref[pr, rows, :] = _dot(rh_ref[pr, rows, :], st) + yh_ref[pr, rows, :]
            st_ref[pr] = _dot(p_ref[pr * nchunk + c], st) + q_ref[pr * nchunk + c]
        return carry

    lax.fori_loop(0, nchunk, step, 0)

    ones = _group_ones(LANES, RW_DIM)
    for pr in range(npair):
        cols = slice(pr * LANES, (pr + 1) * LANES)
        o_ref[pr] = _rw_out_block(y_ref[pr], r_ref[pr], k_ref[pr], v_ref[pr], g_ref[pr], lng_ref[:, cols],
                                  lnb_ref[:, cols], rk_ref[:, cols], ones).astype(o_ref.dtype)


def _rwkv(r, lw, k, v, na, bb, g, ln_g, ln_b, r_k, b, s, sb, unroll):
    npair, t, w = r.shape
    nchunk = sb // CHUNK
    nsb = s // sb
    spec = pl.BlockSpec((npair, sb, w), lambda bi, si: (0, bi * nsb + si, 0))
    small = pl.BlockSpec((1, RW_W), lambda bi, si: (0, 0))
    vec = lambda a_: a_.astype(F32).reshape(1, RW_W)
    return pl.pallas_call(
        functools.partial(_rwkv_kernel, nchunk=nchunk, unroll=unroll),
        grid=(b, nsb),
        in_specs=[spec] * 7 + [small] * 3,
        out_specs=spec,
        out_shape=jax.ShapeDtypeStruct((npair, t, w), BF16),
        scratch_shapes=[pltpu.VMEM((npair, sb, w), F32),
                        pltpu.VMEM((npair, sb, w), F32), pltpu.VMEM((npair, sb, w), F32),
                        pltpu.VMEM((npair * nchunk, w, w), F32), pltpu.VMEM((npair * nchunk, w, w), F32),
                        pltpu.VMEM((npair, w, w), F32)],
        compiler_params=_params("parallel", "arbitrary"),
    )(r, lw, k, v, na, bb, g, vec(ln_g), vec(ln_b), vec(r_k))


def _outproj_kernel(x_ref, oa_ref, oc_ref, ob_ref, w_ref, o_ref):
    mix = jnp.concatenate([_cat_blocks(oa_ref, oa_ref.shape[0]), _cat_blocks(oc_ref, oc_ref.shape[0]),
                           _cat_blocks(ob_ref, ob_ref.shape[0])], axis=1)
    o_ref[...] = x_ref[...] + jnp.dot(mix, w_ref[...], preferred_element_type=F32)


def _outproj(x, oa, oc, ob, w, li, tm, tn):
    t, d = x.shape
    blocks = lambda a_: pl.BlockSpec((a_.shape[0], tm, LANES), lambda i, j: (0, i, 0))
    return pl.pallas_call(
        _outproj_kernel,
        grid=(t // tm, d // tn),
        in_specs=[pl.BlockSpec((tm, tn), lambda i, j: (i, j)),
                  blocks(oa), blocks(oc), blocks(ob),
                  pl.BlockSpec((None, w.shape[1], tn), lambda i, j: (li, 0, j))],
        out_specs=pl.BlockSpec((tm, tn), lambda i, j: (i, j)),
        out_shape=jax.ShapeDtypeStruct((t, d), F32),
        compiler_params=_params("parallel", "arbitrary"),
    )(x, oa, oc, ob, w)


FFN_SPLIT = 2


def _swiglu_rows(x_ref, wg_ref, wu_ref, wd_ref):
    n = x_ref.shape[0] // FFN_SPLIT
    sl = [slice(i * n, (i + 1) * n) for i in range(FFN_SPLIT)]
    wg, wu, wd = wg_ref[...], wu_ref[...], wd_ref[...]
    gu = [(jnp.dot(x_ref[r, :], wg, preferred_element_type=F32),
           jnp.dot(x_ref[r, :], wu, preferred_element_type=F32)) for r in sl]
    act = [(g * jax.nn.sigmoid(g) * u).astype(BF16) for g, u in gu]
    return [(r, jnp.dot(a, wd, preferred_element_type=F32)) for r, a in zip(sl, act)]


def _ffn_kernel(x_ref, g_ref, wg_ref, wu_ref, wd_ref, o_ref, xn_ref, acc_ref):
    f = pl.program_id(1)

    @pl.when(f == 0)
    def _():
        xn_ref[...] = _rms(x_ref[...], g_ref[...]).astype(BF16)
        acc_ref[...] = jnp.zeros(acc_ref.shape, F32)

    for r, y in _swiglu_rows(xn_ref, wg_ref, wu_ref, wd_ref):
        acc_ref[r, :] += y

    @pl.when(f == pl.num_programs(1) - 1)
    def _():
        o_ref[...] = x_ref[...] + acc_ref[...]


def _ffn(x, g, wg, wu, wd, li, tm, tf):
    t, d = x.shape
    ff = wg.shape[2]
    return pl.pallas_call(
        _ffn_kernel,
        grid=(t // tm, ff // tf),
        in_specs=[pl.BlockSpec((tm, d), lambda i, f: (i, 0)),
                  pl.BlockSpec((1, d), lambda i, f: (0, 0)),
                  pl.BlockSpec((None, d, tf), lambda i, f: (li, 0, f)),
                  pl.BlockSpec((None, d, tf), lambda i, f: (li, 0, f)),
                  pl.BlockSpec((None, tf, d), lambda i, f: (li, f, 0))],
        out_specs=pl.BlockSpec((tm, d), lambda i, f: (i, 0)),
        out_shape=jax.ShapeDtypeStruct((t, d), F32),
        scratch_shapes=[pltpu.VMEM((tm, d), BF16), pltpu.VMEM((tm, d), F32)],
        compiler_params=_params("parallel", "arbitrary"),
    )(x, g.reshape(1, d), wg, wu, wd)


def _router_kernel(x_ref, g_ref, wr_ref, h_ref, comb_ref, combt_ref, rcol_ref, rrow_ref, cnt_ref,
                   *, n_experts):
    h = _rms(x_ref[...], g_ref[...])
    h_ref[...] = h.astype(BF16)
    logits = _dot3(h, wr_ref[...])
    lane = lax.broadcasted_iota(jnp.int32, logits.shape, 1)
    lg = jnp.where(lane < n_experts, logits, NEG)
    m1 = jnp.max(lg, axis=-1, keepdims=True)
    i1 = jnp.min(jnp.where(lg == m1, lane, LANES), axis=-1, keepdims=True)
    lg2 = jnp.where(lane == i1, NEG, lg)
    m2 = jnp.max(lg2, axis=-1, keepdims=True)
    i2 = jnp.min(jnp.where(lg2 == m2, lane, LANES), axis=-1, keepdims=True)
    e2 = jnp.exp(m2 - m1)
    w1 = 1.0 / (1.0 + e2)
    comb = jnp.where(lane == i1, w1, 0.0) + jnp.where(lane == i2, e2 * w1, 0.0)
    combt = comb.T[0:combt_ref.shape[0], :]
    comb_ref[...] = comb
    combt_ref[...] = combt
    ts = comb.shape[0]
    tt = lax.broadcasted_iota(jnp.int32, (ts, ts), 0)
    uu = lax.broadcasted_iota(jnp.int32, (ts, ts), 1)
    live = jnp.where(comb > 0.0, 1.0, 0.0)
    rcol_ref[...] = _dot((uu < tt).astype(BF16), live)
    rrow_ref[...] = _dot(jnp.where(combt > 0.0, 1.0, 0.0), (tt < uu).astype(BF16))
    cnt_ref[0] = jnp.sum(live, axis=0, keepdims=True).astype(jnp.int32)


def _router(x, g, wr, ts):
    t, d = x.shape
    e = wr.shape[1]
    ep = max(8, e)
    wrp = jnp.zeros((d, LANES), F32).at[:, :e].set(wr)
    nt = t // ts
    return pl.pallas_call(
        functools.partial(_router_kernel, n_experts=e),
        grid=(nt,),
        in_specs=[pl.BlockSpec((ts, d), lambda i: (i, 0)),
                  pl.BlockSpec((1, d), lambda i: (0, 0)),
                  pl.BlockSpec((d, LANES), lambda i: (0, 0))],
        out_specs=[pl.BlockSpec((ts, d), lambda i: (i, 0)),
                   pl.BlockSpec((ts, LANES), lambda i: (i, 0)),
                   pl.BlockSpec((ep, ts), lambda i: (0, i)),
                   pl.BlockSpec((ts, LANES), lambda i: (i, 0)),
                   pl.BlockSpec((ep, ts), lambda i: (0, i)),
                   pl.BlockSpec((1, 1, LANES), lambda i: (i, 0, 0))],
        out_shape=[jax.ShapeDtypeStruct((t, d), BF16),
                   jax.ShapeDtypeStruct((t, LANES), F32),
                   jax.ShapeDtypeStruct((ep, t), F32),
                   jax.ShapeDtypeStruct((t, LANES), F32),
                   jax.ShapeDtypeStruct((ep, t), F32),
                   jax.ShapeDtypeStruct((nt, 1, LANES), jnp.int32)],
        compiler_params=_params("parallel"),
    )(x, g.reshape(1, d), wrp)


MOE_BM = 128
MOE_BMC = 256
MOE_BMF = 512
NO_MATCH = -(1 << 20)


def _moe_plan(counts, nt, ne, t, bm, bmc, bmf):
    i32 = jnp.int32
    cnt = counts.reshape(nt, ne).astype(i32)
    tot = jnp.sum(cnt, axis=0)
    ptot = (tot + bmf - 1) // bmf * bmf
    eend = jnp.cumsum(ptot)
    ebase = eend - ptot
    seg0 = ebase[None, :] + jnp.cumsum(cnt, axis=0) - cnt
    seg1 = seg0 + cnt
    nrows = 2 * t + ne * bmf
    nblk = nrows // bmf
    count_le = lambda ends, v: jnp.sum((ends[None, :] <= v[:, None]).astype(i32), axis=1)
    blk_exp = jnp.minimum(count_le(eend, jnp.arange(nblk, dtype=i32) * bmf), ne - 1)
    nvalid = (eend[-1] // bmf).astype(i32).reshape(1)

    def pairs(c0, c1, seg_tile, seg_exp, bm, dummy_blk):
        g = jnp.arange(nrows // bm + nt * ne, dtype=i32)
        n = jnp.where(c1 > c0, (c1 - 1) // bm - c0 // bm + 1, 0)
        pend = jnp.cumsum(n)
        k = jnp.minimum(count_le(pend, g), c0.shape[0] - 1)
        valid = g < pend[-1]
        blk = c0[k] // bm + (g - (pend[k] - n[k]))
        blk = jnp.where(valid, blk, dummy_blk)
        delta = jnp.where(valid, c0[k] - blk * bm, NO_MATCH)
        return (blk.astype(i32), jnp.where(valid, seg_tile[k], nt - 1).astype(i32),
                seg_exp[k].astype(i32), delta.astype(i32))

    tiles = jnp.arange(nt, dtype=i32)
    exps = jnp.arange(ne, dtype=i32)
    c1g = seg1.at[nt - 1].set(eend)
    g_blk, g_tile, g_exp, g_delta = pairs(seg0.T.reshape(-1), c1g.T.reshape(-1),
                                          jnp.tile(tiles, ne), jnp.repeat(exps, nt), bm, nrows // bm)
    g_first = jnp.concatenate([jnp.ones((1,), i32), (g_blk[1:] != g_blk[:-1]).astype(i32)])
    s_blk, s_tile, s_exp, s_delta = pairs(seg0.reshape(-1), seg1.reshape(-1),
                                          jnp.repeat(tiles, ne), jnp.tile(exps, nt), bmc, 0)
    s_first = jnp.concatenate([jnp.ones((1,), i32), (s_tile[1:] != s_tile[:-1]).astype(i32)])
    return dict(nrows=nrows, blk_exp=blk_exp, nvalid=nvalid,
                gather=(g_blk, g_tile, g_exp, g_delta, g_first),
                combine=(s_blk, s_tile, s_exp, s_delta, s_first))


def _moe_gather_kernel(blk_ref, tile_ref, exp_ref, delta_ref, first_ref, h_ref, rrow_ref, combt_ref,
                       o_ref):
    g = pl.program_id(0)
    e = exp_ref[g]
    bm, ts = o_ref.shape[0], h_ref.shape[0]
    pos = rrow_ref[pl.ds(e, 1), :] + delta_ref[g].astype(F32)
    live = combt_ref[pl.ds(e, 1), :] > 0.0
    slot = lax.broadcasted_iota(jnp.int32, (bm, ts), 0).astype(F32)
    onehot = jnp.where((pos == slot) & live, 1.0, 0.0).astype(BF16)
    val = jnp.dot(onehot, h_ref[...], preferred_element_type=F32).astype(BF16)

    @pl.when(first_ref[g] == 1)
    def _():
        o_ref[...] = val

    @pl.when(first_ref[g] == 0)
    def _():
        o_ref[...] += val


def _moe_gather(plan, h, rrow, combt, ts, bm):
    t, d = h.shape
    ep = combt.shape[0]
    blk, tile, exp, delta, first = plan["gather"]
    grid_spec = pltpu.PrefetchScalarGridSpec(
        num_scalar_prefetch=5,
        grid=(blk.shape[0],),
        in_specs=[pl.BlockSpec((ts, d), lambda g, b_, t_, e_, d_, f_: (t_[g], 0)),
                  pl.BlockSpec((ep, ts), lambda g, b_, t_, e_, d_, f_: (0, t_[g])),
                  pl.BlockSpec((ep, ts), lambda g, b_, t_, e_, d_, f_: (0, t_[g]))],
        out_specs=pl.BlockSpec((bm, d), lambda g, b_, t_, e_, d_, f_: (b_[g], 0)),
    )
    return pl.pallas_call(
        _moe_gather_kernel,
        grid_spec=grid_spec,
        out_shape=jax.ShapeDtypeStruct((plan["nrows"] + MOE_BMF, d), BF16),
        compiler_params=_params("arbitrary"),
    )(blk, tile, exp, delta, first, h, rrow, combt)


def _moe_ffn_kernel(bexp_ref, nv_ref, x_ref, wg_ref, wu_ref, wd_ref, o_ref, acc_ref):
    i, f = pl.program_id(0), pl.program_id(1)
    valid = i < nv_ref[0]

    @pl.when(valid)
    def _():
        @pl.when(f == 0)
        def _():
            acc_ref[...] = jnp.zeros(acc_ref.shape, F32)

        for r, y in _swiglu_rows(x_ref, wg_ref, wu_ref, wd_ref):
            acc_ref[r, :] += y

    @pl.when(f == pl.num_programs(1) - 1)
    def _():
        @pl.when(valid)
        def _():
            o_ref[...] = acc_ref[...].astype(BF16)

        @pl.when(jnp.logical_not(valid))
        def _():
            o_ref[...] = jnp.zeros(o_ref.shape, BF16)


def _moe_ffn(plan, xs, wg, wu, wd, li, bmf, tf):
    d = xs.shape[1]
    ff = wg.shape[3]
    nblk = plan["blk_exp"].shape[0]
    grid_spec = pltpu.PrefetchScalarGridSpec(
        num_scalar_prefetch=2,
        grid=(nblk, ff // tf),
        in_specs=[pl.BlockSpec((bmf, d), lambda i, f, be, nv: (i, 0)),
                  pl.BlockSpec((None, None, d, tf), lambda i, f, be, nv: (li, be[i], 0, f)),
                  pl.BlockSpec((None, None, d, tf), lambda i, f, be, nv: (li, be[i], 0, f)),
                  pl.BlockSpec((None, None, tf, d), lambda i, f, be, nv: (li, be[i], f, 0))],
        out_specs=pl.BlockSpec((bmf, d), lambda i, f, be, nv: (i, 0)),
        scratch_shapes=[pltpu.VMEM((bmf, d), F32)],
    )
    return pl.pallas_call(
        _moe_ffn_kernel,
        grid_spec=grid_spec,
        out_shape=jax.ShapeDtypeStruct((nblk * bmf, d), BF16),
        compiler_params=_params("parallel", "arbitrary"),
    )(plan["blk_exp"], plan["nvalid"], xs, wg, wu, wd)


def _moe_combine_kernel(blk_ref, tile_ref, exp_ref, delta_ref, first_ref, y_ref, x_ref, comb_ref,
                        rcol_ref, o_ref):
    g = pl.program_id(0)
    e = exp_ref[g]
    ts, bm = x_ref.shape[0], y_ref.shape[0]

    @pl.when(first_ref[g] == 1)
    def _():
        o_ref[...] = x_ref[...]

    lane = lax.broadcasted_iota(jnp.int32, comb_ref.shape, 1)
    sel = lane == e
    cw = jnp.sum(jnp.where(sel, comb_ref[...], 0.0), axis=-1, keepdims=True)
    pos = (jnp.sum(jnp.where(sel, rcol_ref[...], 0.0), axis=-1, keepdims=True)
           + delta_ref[g].astype(F32))
    slot = lax.broadcasted_iota(jnp.int32, (ts, bm), 1).astype(F32)
    onehot = jnp.where((pos == slot) & (cw > 0.0), 1.0, 0.0).astype(BF16)
    o_ref[...] += cw * jnp.dot(onehot, y_ref[...], preferred_element_type=F32)


def _moe_combine(plan, ys, x, comb, rcol, ts, bm):
    t, d = x.shape
    blk, tile, exp, delta, first = plan["combine"]
    grid_spec = pltpu.PrefetchScalarGridSpec(
        num_scalar_prefetch=5,
        grid=(blk.shape[0],),
        in_specs=[pl.BlockSpec((bm, d), lambda g, b_, t_, e_, d_, f_: (b_[g], 0)),
                  pl.BlockSpec((ts, d), lambda g, b_, t_, e_, d_, f_: (t_[g], 0)),
                  pl.BlockSpec((ts, LANES), lambda g, b_, t_, e_, d_, f_: (t_[g], 0)),
                  pl.BlockSpec((ts, LANES), lambda g, b_, t_, e_, d_, f_: (t_[g], 0))],
        out_specs=pl.BlockSpec((ts, d), lambda g, b_, t_, e_, d_, f_: (t_[g], 0)),
    )
    return pl.pallas_call(
        _moe_combine_kernel,
        grid_spec=grid_spec,
        out_shape=jax.ShapeDtypeStruct((t, d), F32),
        compiler_params=_params("arbitrary"),
    )(blk, tile, exp, delta, first, ys, x, comb, rcol)


def _moe(x, h, comb, combt, rcol, rrow, counts, wg, wu, wd, li, ts, tf, bm=MOE_BM, bmc=MOE_BMC,
         bmf=MOE_BMF):
    t = x.shape[0]
    plan = _moe_plan(counts, t // ts, wg.shape[1], t, bm, bmc, bmf)
    xs = _moe_gather(plan, h, rrow, combt, ts, bm)
    ys = _moe_ffn(plan, xs, wg, wu, wd, li, bmf, tf)
    return _moe_combine(plan, ys, x, comb, rcol, ts, bmc)


def kernel(x, norm1_g, w_in, da_q_norm, da_k_norm, da_lambda, da_out_norm, dl_q_norm, dl_k_norm, rw_mu, rw_w0, rw_w2, rw_a0, rw_a2, rw_g2, rw_k_k, rw_k_a, rw_r_k, rw_ln_g, rw_ln_b, w_out, norm2_g, ffn_w_gate, ffn_w_up, ffn_w_down, moe_router, moe_w_gate, moe_w_up, moe_w_down):
    b, s, d = x.shape
    depth = w_in.shape[0]
    t = b * s
    n_experts = moe_router.shape[-1]
    xt = x.reshape(t, d)

    qa, ka, va = 0, DA_HEADS, 2 * DA_HEADS
    qb, kb, vb = 3 * DA_HEADS, 3 * DA_HEADS + DL_HEADS, 3 * DA_HEADS + 2 * DL_HEADS
    rw = 3 * DA_HEADS + 3 * DL_HEADS

    tm = min(1024, t)
    tq = min(256, s)
    tf = 512
    assert ffn_w_gate.shape[2] % tf == 0 and moe_w_gate.shape[3] % tf == 0

    w_in, w_out = w_in.astype(BF16), w_out.astype(BF16)
    ffn_w = [w.astype(BF16) for w in (ffn_w_gate, ffn_w_up, ffn_w_down)]
    moe_w = [w.astype(BF16) for w in (moe_w_gate, moe_w_up, moe_w_down)]

    for l in range(depth):
        p = _inproj(xt, norm1_g[l], w_in, l, tm, 1280)

        lam_init = 0.8 - 0.6 * math.exp(-0.3 * l)
        oa = _diffattn(p, qa, ka, va, da_q_norm[l], da_k_norm[l], da_lambda[l], da_out_norm[l],
                       b, s, lam_init, tq)
        oc = _dilattn(p, qb, kb, vb, dl_q_norm[l], dl_k_norm[l], b, s, tq)

        r, lw, k2, v, na, bb, g = _rwprep(p, rw, s, rw_mu[l], rw_w0[l], rw_a0[l], rw_k_k[l],
                                          rw_k_a[l], rw_w2[l], rw_a2[l], rw_g2[l], min(256, s))
        ob = _rwkv(r, lw, k2, v, na, bb, g, rw_ln_g[l], rw_ln_b[l], rw_r_k[l].reshape(-1),
                   b, s, min(512, s), 16)

        xt = _outproj(xt, oa, oc, ob, w_out, l, min(512, t), d)

        i = l // 2
        if l % 2 == 0:
            xt = _ffn(xt, norm2_g[l], *ffn_w, i, min(512, t), tf)
        else:
            ts = min(1024, t)
            h, comb, combt, rcol, rrow, cnt = _router(xt, norm2_g[l], moe_router[i], ts)
            counts = cnt[:, 0, :n_experts].reshape(-1)
            xt = _moe(xt, h, comb, combt, rcol, rrow, counts, *moe_w, i, ts, tf)
    return xt.reshape(b, s, d)
```
